```python
import jax, jax.numpy as jnp
from jax import lax
import numpy as np


D_MODEL = 1024
BATCH = 2
SEQ = 8192
DEPTH = 1

GRID_W = 64
CTX_LEN = 256
EPS = 1e-6

MLSTM_HEADS = 4
MLSTM_DH = 128
MLSTM_W = MLSTM_HEADS * MLSTM_DH
MLSTM_GATES = 2 * 2 * MLSTM_HEADS
MLSTM_CHUNK = 128

ATTN_Q_HEADS = 8
ATTN_KV_HEADS = 2
ATTN_GROUP = ATTN_Q_HEADS // ATTN_KV_HEADS
ATTN_DH = 64
ATTN_Q_W = ATTN_Q_HEADS * ATTN_DH
ATTN_KV_W = ATTN_KV_HEADS * ATTN_DH
WINDOW = 128
ATTN_BLOCK = 128
ROPE_BASE = 10000.0
ROPE_AXIS_FREQS = ATTN_DH // 4

N_BRANCH = 2
IN_SIZES = (MLSTM_W, MLSTM_W, MLSTM_W, MLSTM_W, MLSTM_GATES,
            ATTN_Q_W, ATTN_KV_W, ATTN_KV_W, N_BRANCH * D_MODEL)
D_IN = 4 * MLSTM_W + MLSTM_GATES + ATTN_Q_W + 2 * ATTN_KV_W + N_BRANCH * D_MODEL

PEER_HEADS = 8
PEER_N_KEYS = 128
PEER_N_EXPERTS = PEER_N_KEYS * PEER_N_KEYS
PEER_KEY_DIM = 256
PEER_HALF = PEER_KEY_DIM // 2
PEER_TOPK = 16
PEER_BLOCK = 128

kernel_name = 'hybrid_mlstm_swa_peer_dit_block'


def _rms(x, g):
    xf = x.astype(jnp.float32)
    y = xf * lax.rsqrt(jnp.mean(xf * xf, axis=-1, keepdims=True) + EPS)
    return y.astype(x.dtype) * g


def _modulate(x, g, shift, scale):
    return _rms(x, g) * (1 + scale) + shift


def _adaln(cond, w, b):
    mod = jax.nn.silu(cond) @ w + b
    return tuple(m[:, None, :] for m in jnp.split(mod, 6, axis=-1))


def _split_in(p):
    idx = []
    acc = 0
    for s in IN_SIZES[:-1]:
        acc += s
        idx.append(acc)
    return jnp.split(p, idx, axis=-1)


def _axial_rope(T):
    rows = T // GRID_W
    row = jnp.broadcast_to(jnp.arange(rows)[:, None], (rows, GRID_W)).reshape(T)
    col = jnp.broadcast_to(jnp.arange(GRID_W)[None, :], (rows, GRID_W)).reshape(T)
    inv = ROPE_BASE ** (-jnp.arange(ROPE_AXIS_FREQS, dtype=jnp.float32) / ROPE_AXIS_FREQS)
    ang = jnp.concatenate([row[:, None].astype(jnp.float32) * inv,
                           col[:, None].astype(jnp.float32) * inv], axis=-1)
    return jnp.cos(ang), jnp.sin(ang)


def _rope(t, cos, sin):
    t1, t2 = jnp.split(t, 2, axis=-1)
    c = cos[None, :, None, :]
    s = sin[None, :, None, :]
    return jnp.concatenate([t1 * c - t2 * s, t2 * c + t1 * s], axis=-1).astype(t.dtype)


def _chunk_seq(t, reverse):
    if reverse:
        t = jnp.flip(t, axis=1)
    B, T = t.shape[:2]
    t = jnp.moveaxis(t, 2, 1)
    return t.reshape((B, t.shape[1], T // MLSTM_CHUNK, MLSTM_CHUNK) + t.shape[3:])


def _unchunk(h, reverse):
    B, H = h.shape[:2]
    h = jnp.moveaxis(h.reshape(B, H, -1, h.shape[-1]), 1, 2)
    return jnp.flip(h, axis=1) if reverse else h


def _mlstm_summaries(k, v, li, lf):
    b = jnp.cumsum(lf, axis=-1)
    a = b[..., -1]
    w_log = a[..., None] - b + li
    m_loc = jnp.max(w_log, axis=-1)
    w = jnp.exp(w_log - m_loc[..., None])
    C_loc = jnp.einsum('bhcs,bhcsk,bhcsv->bhckv', w, k, v)
    n_loc = jnp.einsum('bhcs,bhcsk->bhck', w, k)
    return b, a, m_loc, C_loc, n_loc


def _mlstm_scan(a, m_loc, C_loc, n_loc, state0):
    def step(carry, inp):
        C, n, m = carry
        a_c, m_c, C_c, n_c = inp
        m_new = jnp.maximum(a_c + m, m_c)
        d_old = jnp.exp(a_c + m - m_new)
        d_new = jnp.exp(m_c - m_new)
        C_new = d_old[..., None, None] * C + d_new[..., None, None] * C_c
        n_new = d_old[..., None] * n + d_new[..., None] * n_c
        return (C_new, n_new, m_new), (C, n, m)
    xs = tuple(jnp.moveaxis(t, 2, 0) for t in (a, m_loc, C_loc, n_loc))
    final, starts = lax.scan(step, state0, xs)
    starts = tuple(jnp.moveaxis(t, 0, 2) for t in starts)
    return starts, final


def _mlstm_outputs(q, k, v, li, b, starts):
    C_s, n_s, m_s = starts
    L = q.shape[-2]
    log_inter = b + m_s[..., None]
    log_d = b[..., :, None] - b[..., None, :] + li[..., None, :]
    seen = jnp.tril(jnp.ones((L, L), dtype=bool))
    log_d = jnp.where(seen, log_d, -jnp.inf)
    m_t = jnp.maximum(log_inter, jnp.max(log_d, axis=-1))
    s = jnp.einsum('bhctk,bhcsk->bhcts', q, k) * jnp.exp(log_d - m_t[..., None])
    inter = jnp.exp(log_inter - m_t)
    num = (jnp.einsum('bhcts,bhcsv->bhctv', s, v)
           + inter[..., None] * jnp.einsum('bhctk,bhckv->bhctv', q, C_s))
    den = jnp.sum(s, axis=-1) + inter * jnp.einsum('bhctk,bhck->bhct', q, n_s)
    return num / jnp.maximum(jnp.abs(den), jnp.exp(-m_t))[..., None]


def _mlstm_direction(ctx_in, lat_in, reverse, with_ctx_out):
    qc, kc, vc, ic, fc = (_chunk_seq(t, reverse) for t in ctx_in)
    ql, kl, vl, il, fl = (_chunk_seq(t, reverse) for t in lat_in)
    B, H = qc.shape[:2]
    state0 = (jnp.zeros((B, H, MLSTM_DH, MLSTM_DH), jnp.float32),
              jnp.zeros((B, H, MLSTM_DH), jnp.float32),
              jnp.zeros((B, H), jnp.float32))
    bc, ac, mc, Cc, nc = _mlstm_summaries(kc, vc, ic, fc)
    starts_c, final_c = _mlstm_scan(ac, mc, Cc, nc, state0)
    bl, al, ml, Cl, nl = _mlstm_summaries(kl, vl, il, fl)
    starts_l, _ = _mlstm_scan(al, ml, Cl, nl, final_c)
    h_lat = _unchunk(_mlstm_outputs(ql, kl, vl, il, bl, starts_l), reverse)
    h_ctx = _unchunk(_mlstm_outputs(qc, kc, vc, ic, bc, starts_c), reverse) if with_ctx_out else None
    return h_lat, h_ctx


def _mlstm_branch(parts_lat, parts_ctx, b_mgates, norm_g, with_ctx_out):
    def prep(parts):
        q, k, v, o, gates = parts
        B, T = q.shape[:2]
        hd = lambda t: t.astype(jnp.float32).reshape(B, T, MLSTM_HEADS, MLSTM_DH)
        g = (gates.astype(jnp.float32) + b_mgates.astype(jnp.float32)).reshape(B, T, 2, 2, MLSTM_HEADS)
        return hd(q), hd(k) * (MLSTM_DH ** -0.5), hd(v), o, g

    ql, kl, vl, ol, gl = prep(parts_lat)
    qc, kc, vc, oc, gc = prep(parts_ctx)
    hs_lat, hs_ctx = [], []
    for d, reverse in enumerate((False, True)):
        lat_in = (ql, kl, vl, gl[:, :, d, 0], jax.nn.log_sigmoid(gl[:, :, d, 1]))
        ctx_in = (qc, kc, vc, gc[:, :, d, 0], jax.nn.log_sigmoid(gc[:, :, d, 1]))
        hl, hc = _mlstm_direction(ctx_in, lat_in, reverse, with_ctx_out)
        hs_lat.append(hl)
        hs_ctx.append(hc)

    def finish(h, o):
        B, T = h.shape[:2]
        hn = h * lax.rsqrt(jnp.mean(h * h, axis=-1, keepdims=True) + EPS)
        return hn.reshape(B, T, MLSTM_W).astype(o.dtype) * norm_g * jax.nn.sigmoid(o)

    m_lat = finish(hs_lat[0] + hs_lat[1], ol)
    m_ctx = finish(hs_ctx[0] + hs_ctx[1], oc) if with_ctx_out else None
    return m_lat, m_ctx


def _softmax_with_sink(logits, sink_b):
    sink_col = jnp.broadcast_to(sink_b, logits.shape[:-1] + (1,))
    return jax.nn.softmax(jnp.concatenate([logits, sink_col], axis=-1), axis=-1)[..., :-1]


def _window_attention(q, k, v, kc, vc, sink):
    B, T = q.shape[:2]
    nb = T // ATTN_BLOCK
    scale = ATTN_DH ** -0.5
    qb = q.reshape(B, nb, ATTN_BLOCK, ATTN_KV_HEADS, ATTN_GROUP, ATTN_DH)

    def band(t):
        tp = jnp.pad(t, ((0, 0), (ATTN_BLOCK, ATTN_BLOCK), (0, 0), (0, 0)))
        tp = tp.reshape(B, nb + 2, ATTN_BLOCK, ATTN_KV_HEADS, ATTN_DH)
        return jnp.concatenate([tp[:, :-2], tp[:, 1:-1], tp[:, 2:]], axis=2)

    kw, vw = band(k), band(v)
    qi = jnp.arange(ATTN_BLOCK)[:, None]
    kj = jnp.arange(3 * ATTN_BLOCK)[None, :]
    key_pos = jnp.arange(nb)[:, None, None] * ATTN_BLOCK - ATTN_BLOCK + kj[None]
    mask = (jnp.abs(kj - ATTN_BLOCK - qi) <= WINDOW)[None] & (key_pos >= 0) & (key_pos < T)
    s_loc = jnp.einsum('bnqhgd,bnkhd->bnhgqk', qb, kw).astype(jnp.float32) * scale
    s_loc = jnp.where(mask[None, :, None, None], s_loc, -jnp.inf)
    s_ctx = jnp.einsum('bnqhgd,bchd->bnhgqc', qb, kc).astype(jnp.float32) * scale
    sink_b = sink.astype(jnp.float32).reshape(ATTN_KV_HEADS, ATTN_GROUP, 1, 1)
    p = _softmax_with_sink(jnp.concatenate([s_loc, s_ctx], axis=-1), sink_b)
    p_loc = p[..., :3 * ATTN_BLOCK].astype(v.dtype)
    p_ctx = p[..., 3 * ATTN_BLOCK:].astype(v.dtype)
    out = (jnp.einsum('bnhgqk,bnkhd->bnqhgd', p_loc, vw)
           + jnp.einsum('bnhgqc,bchd->bnqhgd', p_ctx, vc))
    return out.reshape(B, T, ATTN_Q_W)


def _context_attention(qc, kc, vc, sink):
    B, Lc = qc.shape[:2]
    qg = qc.reshape(B, Lc, ATTN_KV_HEADS, ATTN_GROUP, ATTN_DH)
    s = jnp.einsum('bqhgd,bkhd->bhgqk', qg, kc).astype(jnp.float32) * (ATTN_DH ** -0.5)
    sink_b = sink.astype(jnp.float32).reshape(ATTN_KV_HEADS, ATTN_GROUP, 1, 1)
    p = _softmax_with_sink(s, sink_b).astype(vc.dtype)
    return jnp.einsum('bhgqk,bkhd->bqhgd', p, vc).reshape(B, Lc, ATTN_Q_W)


def _attn_branch(parts_lat, parts_ctx, q_g, k_g, sink, cos, sin, with_ctx_out):
    def heads(t, h):
        return t.reshape(t.shape[0], t.shape[1], h, ATTN_DH)
    ql, kl, vl = parts_lat
    qc, kc, vc = parts_ctx
    ql = _rope(_rms(heads(ql, ATTN_Q_HEADS), q_g), cos, sin)
    kl = _rope(_rms(heads(kl, ATTN_KV_HEADS), k_g), cos, sin)
    vl = heads(vl, ATTN_KV_HEADS)
    kc = _rms(heads(kc, ATTN_KV_HEADS), k_g)
    vc = heads(vc, ATTN_KV_HEADS)
    a_lat = _window_attention(ql, kl, vl, kc, vc, sink)
    a_ctx = None
    if with_ctx_out:
        a_ctx = _context_attention(_rms(heads(qc, ATTN_Q_HEADS), q_g), kc, vc, sink)
    return a_lat, a_ctx


def _token_mixer(h_lat, h_ctx, w_in, b_mgates, mlstm_norm_g, q_g, k_g, sink,
                 w_bm, w_ba, w_out, cos, sin, with_ctx_out):
    pl = _split_in(h_lat @ w_in)
    pc = _split_in(h_ctx @ w_in)
    m_lat, m_ctx = _mlstm_branch(pl[:5], pc[:5], b_mgates, mlstm_norm_g, with_ctx_out)
    a_lat, a_ctx = _attn_branch(pl[5:8], pc[5:8], q_g, k_g, sink, cos, sin, with_ctx_out)

    def merge(m, a, gates):
        g_m, g_a = jnp.split(gates, 2, axis=-1)
        return (jax.nn.sigmoid(g_m) * (m @ w_bm) + jax.nn.sigmoid(g_a) * (a @ w_ba)) @ w_out

    y_lat = merge(m_lat, a_lat, pl[8])
    y_ctx = merge(m_ctx, a_ctx, pc[8]) if with_ctx_out else None
    return y_lat, y_ctx


def _peer(h, w_query, sub_keys, expert_u, expert_v):
    B, T, D = h.shape
    tok = h.reshape(-1, PEER_BLOCK, D)

    def block(xb):
        q = (xb @ w_query).reshape(PEER_BLOCK, PEER_HEADS, 2, PEER_HALF)
        s = jnp.einsum('mhpd,hpnd->mhpn', q, sub_keys).astype(jnp.float32)
        s1, i1 = lax.top_k(s[:, :, 0], PEER_TOPK)
        s2, i2 = lax.top_k(s[:, :, 1], PEER_TOPK)
        cand = (s1[..., :, None] + s2[..., None, :]).reshape(PEER_BLOCK, PEER_HEADS, PEER_TOPK * PEER_TOPK)
        cidx = (i1[..., :, None] * PEER_N_KEYS + i2[..., None, :]).reshape(PEER_BLOCK, PEER_HEADS, PEER_TOPK * PEER_TOPK)
        top, pos = lax.top_k(cand, PEER_TOPK)
        idx = jnp.take_along_axis(cidx, pos, axis=-1)
        g = jax.nn.softmax(top, axis=-1)
        act = jax.nn.gelu(jnp.einsum('mhkd,md->mhk', expert_u[idx], xb).astype(jnp.float32))
        w = (g * act).astype(xb.dtype)
        return jnp.einsum('mhk,mhkd->md', w, expert_v[idx])

    return lax.map(block, tok).reshape(B, T, D)


def setup_inputs(seed: int = 0) -> dict:
    key = jax.random.key(seed)
    ks = jax.random.split(key, 24)
    f32 = jnp.float32

    def nrm(k, shape, s):
        return jax.random.normal(k, shape, f32) * s

    gate_base = jnp.tile(jnp.concatenate([jnp.zeros((MLSTM_HEADS,), f32),
                                          jnp.linspace(3.0, 6.0, MLSTM_HEADS, dtype=f32)]), 2)
    return {
        'x': nrm(ks[0], (BATCH, SEQ, D_MODEL), 1.0),
        'c': nrm(ks[1], (BATCH, D_MODEL), 1.0),
        'ctx': nrm(ks[2], (BATCH, CTX_LEN, D_MODEL), 1.0),
        'c_ctx': nrm(ks[3], (D_MODEL,), 1.0),
        'w_ada': nrm(ks[4], (DEPTH, D_MODEL, 6 * D_MODEL), D_MODEL ** -0.5),
        'b_ada': nrm(ks[5], (DEPTH, 6 * D_MODEL), 0.02),
        'norm_mix_g': 1.0 + nrm(ks[6], (DEPTH, D_MODEL), 0.02),
        'norm_ffn_g': 1.0 + nrm(ks[7], (DEPTH, D_MODEL), 0.02),
        'w_in': nrm(ks[8], (DEPTH, D_MODEL, D_IN), D_MODEL ** -0.5),
        'b_mgates': gate_base + nrm(ks[9], (DEPTH, MLSTM_GATES), 0.1),
        'mlstm_norm_g': 1.0 + nrm(ks[10], (DEPTH, MLSTM_W), 0.02),
        'attn_q_norm_g': 1.0 + nrm(ks[11], (DEPTH, ATTN_DH), 0.02),
        'attn_k_norm_g': 1.0 + nrm(ks[12], (DEPTH, ATTN_DH), 0.02),
        'attn_sink': nrm(ks[13], (DEPTH, ATTN_Q_HEADS), 0.5),
        'w_branch_m': nrm(ks[14], (DEPTH, MLSTM_W, D_MODEL), MLSTM_W ** -0.5),
        'w_branch_a': nrm(ks[15], (DEPTH, ATTN_Q_W, D_MODEL), ATTN_Q_W ** -0.5),
        'w_out': nrm(ks[16], (DEPTH, D_MODEL, D_MODEL), D_MODEL ** -0.5),
        'peer_w_query': nrm(ks[17], (DEPTH, D_MODEL, PEER_HEADS * PEER_KEY_DIM), D_MODEL ** -0.5),
        'peer_sub_keys': nrm(ks[18], (DEPTH, PEER_HEADS, 2, PEER_N_KEYS, PEER_HALF), PEER_HALF ** -0.5),
        'peer_u': nrm(ks[19], (DEPTH, PEER_N_EXPERTS, D_MODEL), D_MODEL ** -0.5),
        'peer_v': nrm(ks[20], (DEPTH, PEER_N_EXPERTS, D_MODEL), 0.5),
    }


def reference(x, c, ctx, c_ctx, w_ada, b_ada, norm_mix_g, norm_ffn_g, w_in, b_mgates,
              mlstm_norm_g, attn_q_norm_g, attn_k_norm_g, attn_sink, w_branch_m, w_branch_a,
              w_out, peer_w_query, peer_sub_keys, peer_u, peer_v):
    T = x.shape[1]
    cos, sin = _axial_rope(T)
    for layer in range(DEPTH):
        update_ctx = layer + 1 < DEPTH
        sh1, sc1, g1, sh2, sc2, g2 = _adaln(c, w_ada[layer], b_ada[layer])
        csh1, csc1, cg1, csh2, csc2, cg2 = _adaln(c_ctx[None], w_ada[layer], b_ada[layer])
        h_lat = _modulate(x, norm_mix_g[layer], sh1, sc1)
        h_ctx = _modulate(ctx, norm_mix_g[layer], csh1, csc1)
        y_lat, y_ctx = _token_mixer(h_lat, h_ctx, w_in[layer], b_mgates[layer], mlstm_norm_g[layer],
                                    attn_q_norm_g[layer], attn_k_norm_g[layer], attn_sink[layer],
                                    w_branch_m[layer], w_branch_a[layer], w_out[layer],
                                    cos, sin, update_ctx)
        x = x + g1 * y_lat
        x = x + g2 * _peer(_modulate(x, norm_ffn_g[layer], sh2, sc2), peer_w_query[layer],
                           peer_sub_keys[layer], peer_u[layer], peer_v[layer])
        if update_ctx:
            ctx = ctx + cg1 * y_ctx
            ctx = ctx + cg2 * _peer(_modulate(ctx, norm_ffn_g[layer], csh2, csc2), peer_w_query[layer],
                                    peer_sub_keys[layer], peer_u[layer], peer_v[layer])
    return x
```

```python
import functools
import math

import jax
import jax.numpy as jnp
from jax import lax
from jax.experimental import pallas as pl
from jax.experimental.pallas import tpu as pltpu

F32 = jnp.float32
BF16 = jnp.bfloat16

EPS = 1e-6
GRID_W = 64
ROPE_BASE = 10000.0

MLSTM_HEADS = 4
MLSTM_DH = 128
MLSTM_W = MLSTM_HEADS * MLSTM_DH
MLSTM_CHUNK = 128
GATE_LANES = 128

ATTN_Q_HEADS = 8
ATTN_KV_HEADS = 2
ATTN_DH = 64
ATTN_Q_W = ATTN_Q_HEADS * ATTN_DH
ATTN_KV_W = ATTN_KV_HEADS * ATTN_DH
ATTN_BLOCK = 128

PEER_HEADS = 8
PEER_N_KEYS = 128
PEER_HALF = 128
PEER_TOPK = 16

VMEM_LIMIT_BYTES = 56 * 1024 * 1024

NEG_INF = float("-inf")
POS_INF = float("inf")


def _cparams(sem):
    return pltpu.CompilerParams(dimension_semantics=sem, vmem_limit_bytes=VMEM_LIMIT_BYTES)


def _dot(a, b):
    return jnp.dot(a, b, preferred_element_type=F32)


def _dot_nt(a, b):
    return lax.dot_general(a, b, (((1,), (1,)), ((), ())), preferred_element_type=F32)


def _dot_tn(a, b):
    return lax.dot_general(a, b, (((0,), (0,)), ((), ())), preferred_element_type=F32)


def _sigmoid(x):
    return 1.0 / (1.0 + jnp.exp(-x))


def _segsum(x2, bd):
    hi = x2.astype(BF16)
    lo = (x2 - hi.astype(F32)).astype(BF16)
    return _dot(hi, bd) + _dot(lo, bd)


def _modulated(x, g, shift, scale):
    ms = jnp.mean(x * x, axis=-1, keepdims=True)
    return (x * lax.rsqrt(ms + EPS)) * g * (1.0 + scale) + shift


def _adaln_kernel(c_ref, w_ref, b_ref, o_ref):
    c = c_ref[...]
    s = c * _sigmoid(c)
    o_ref[...] = jnp.dot(s, w_ref[...], preferred_element_type=F32,
                         precision=lax.Precision.HIGHEST) + b_ref[...]


def _adaln(cond8, w, b):
    d = cond8.shape[1]
    n = w.shape[1]
    bn = n // 6
    return pl.pallas_call(
        _adaln_kernel,
        grid=(n // bn,),
        in_specs=[pl.BlockSpec((8, d), lambda j: (0, 0)),
                  pl.BlockSpec((d, bn), lambda j: (0, j)),
                  pl.BlockSpec((1, bn), lambda j: (0, j))],
        out_specs=pl.BlockSpec((8, bn), lambda j: (0, j)),
        out_shape=jax.ShapeDtypeStruct((8, n), F32),
        compiler_params=_cparams(("arbitrary",)),
        name="adaln",
    )(cond8, w, b.reshape(1, n))


_C_MQ, _C_MK, _C_MV = 0, 512, 1024
_C_AQ, _C_AK, _C_AV, _C_MG = 1536, 2048, 2176, 2304
_C_END = 2432


def _rope(t, cos_t, sin_t, width):
    lane = lax.broadcasted_iota(jnp.int32, t.shape, 1)
    lower = (lane % ATTN_DH) < (ATTN_DH // 2)
    swapped = jnp.where(lower, pltpu.roll(t, width - ATTN_DH // 2, 1),
                        pltpu.roll(t, ATTN_DH // 2, 1))
    return t * cos_t + swapped * sin_t


def _proj_kernel(x_ref, sh_ref, sc_ref, g_ref, w_ref, bg_ref, qg_ref, kg_ref,
                 bdq_ref, bdk_ref, cos_ref, sin_ref,
                 mq_ref, mk_ref, mv_ref, mg_ref, aq_ref, ak_ref, av_ref, *, rope):
    x = x_ref[0]
    h = _modulated(x, g_ref[...], sh_ref[0], sc_ref[0]).astype(BF16)

    p = _dot(h, w_ref[:, _C_MQ:_C_AQ])
    mq_ref[0] = p[:, 0:512].astype(BF16)
    mk_ref[0] = (p[:, 512:1024] * (MLSTM_DH ** -0.5)).astype(BF16)
    mv_ref[0] = p[:, 1024:1536].astype(BF16)

    pq = _dot(h, w_ref[:, _C_AQ:_C_AK])
    ssq = _segsum(pq * pq, bdq_ref[...])
    q = pq * lax.rsqrt(ssq * (1.0 / ATTN_DH) + EPS) * qg_ref[...]
    if rope:
        cos_t = cos_ref[...]
        sin_t = sin_ref[...]
        q = _rope(q, jnp.concatenate([cos_t] * 4, axis=1), jnp.concatenate([sin_t] * 4, axis=1),
                  ATTN_Q_W)
    aq_ref[0] = (q * (ATTN_DH ** -0.5)).astype(BF16)

    pk = _dot(h, w_ref[:, _C_AK:_C_AV])
    ssk = _segsum(pk * pk, bdk_ref[...])
    k = pk * lax.rsqrt(ssk * (1.0 / ATTN_DH) + EPS) * kg_ref[...]
    if rope:
        k = _rope(k, cos_ref[...], sin_ref[...], ATTN_KV_W)
    ak_ref[0] = k.astype(BF16)

    av_ref[0] = _dot(h, w_ref[:, _C_AV:_C_MG]).astype(BF16)

    z = _dot(h, w_ref[:, _C_MG:_C_END]) + bg_ref[...]
    lane = lax.broadcasted_iota(jnp.int32, z.shape, 1)
    is_forget = (lane % 8) >= MLSTM_HEADS
    logsig = jnp.minimum(z, 0.0) - jnp.log1p(jnp.exp(-jnp.abs(z)))
    mg_ref[0] = jnp.where(is_forget, logsig, z)


def _proj(x, shift, scale, g, w_pack, bg, qg, kg, bdq, bdk, cos_t, sin_t, *, rope, tm):
    B, T, D = x.shape
    nt = T // tm
    full = lambda shape: pl.BlockSpec(shape, lambda b, t: (0,) * len(shape))
    tok = lambda w: pl.BlockSpec((1, tm, w), lambda b, t: (b, t, 0))
    outs = [(MLSTM_W, BF16), (MLSTM_W, BF16), (MLSTM_W, BF16), (GATE_LANES, F32),
            (ATTN_Q_W, BF16), (ATTN_KV_W, BF16), (ATTN_KV_W, BF16)]
    return pl.pallas_call(
        functools.partial(_proj_kernel, rope=rope),
        grid=(B, nt),
        in_specs=[tok(D),
                  pl.BlockSpec((1, 1, D), lambda b, t: (b, 0, 0)),
                  pl.BlockSpec((1, 1, D), lambda b, t: (b, 0, 0)),
                  full((1, D)), full(w_pack.shape), full((1, GATE_LANES)),
                  full((1, ATTN_Q_W)), full((1, ATTN_KV_W)),
                  full((ATTN_Q_W, ATTN_Q_W)), full((ATTN_KV_W, ATTN_KV_W)),
                  pl.BlockSpec((tm, ATTN_KV_W), lambda b, t: (t, 0)),
                  pl.BlockSpec((tm, ATTN_KV_W), lambda b, t: (t, 0))],
        out_specs=[tok(w) for w, _ in outs],
        out_shape=[jax.ShapeDtypeStruct((B, T, w), dt) for w, dt in outs],
        compiler_params=_cparams(("parallel", "parallel")),
        name="proj_rope" if rope else "proj_ctx",
    )(x, shift, scale, g, w_pack, bg, qg, kg, bdq, bdk, cos_t, sin_t)


def _mlstm_kernel(qf_ref, qb_ref, kf_ref, kb_ref, vf_ref, vb_ref, gf_ref, gb_ref,
                  hf_ref, hb_ref, c_ref, n_ref, m_ref):
    s = pl.program_id(1)
    L = MLSTM_CHUNK

    @pl.when(s == 0)
    def _():
        c_ref[...] = jnp.zeros_like(c_ref)
        n_ref[...] = jnp.zeros_like(n_ref)
        m_ref[...] = jnp.zeros_like(m_ref)

    row = lax.broadcasted_iota(jnp.int32, (L, L), 0)
    col = lax.broadcasted_iota(jnp.int32, (L, L), 1)
    tri_f = (col <= row).astype(F32)

    for d in range(2):
        q_all = (qf_ref, qb_ref)[d][0]
        k_all = (kf_ref, kb_ref)[d][0]
        v_all = (vf_ref, vb_ref)[d][0]
        g = (gf_ref, gb_ref)[d][0]
        out_ref = (hf_ref, hb_ref)[d]

        cum_f = jnp.dot(tri_f, g, preferred_element_type=F32, precision=lax.Precision.HIGHEST)
        if d == 0:
            b_all = cum_f
            seen = col <= row
            last = L - 1
        else:
            b_all = cum_f[L - 1:L, :] - cum_f + g
            seen = col >= row
            last = 0
        g_t = g.T
        b_t = b_all.T

        for hh in range(MLSTM_HEADS):
            r = d * MLSTM_HEADS + hh
            ci = d * 2 * MLSTM_HEADS + hh
            cf = ci + MLSTM_HEADS
            sl = slice(hh * MLSTM_DH, (hh + 1) * MLSTM_DH)
            q = q_all[:, sl]
            k = k_all[:, sl]
            v = v_all[:, sl]
            li_col = g[:, ci:ci + 1]
            b_col = b_all[:, cf:cf + 1]
            li_row = g_t[ci:ci + 1, :]
            b_row = b_t[cf:cf + 1, :]
            a = b_all[last:last + 1, cf:cf + 1]

            c_s = c_ref[r]
            n_s = n_ref[r]
            m_s = m_ref[r][:, 0:1]

            log_inter = b_col + m_s
            log_d = jnp.where(seen, b_col - b_row + li_row, NEG_INF)
            m_t = jnp.maximum(log_inter, jnp.max(log_d, axis=1, keepdims=True))
            sc = _dot_nt(q, k) * jnp.exp(log_d - m_t)
            inter = jnp.exp(log_inter - m_t)
            num = _dot(sc.astype(BF16), v) + inter * _dot(q, c_s.astype(BF16))
            qn = jnp.sum(q.astype(F32) * n_s, axis=1, keepdims=True)
            den = jnp.sum(sc, axis=1, keepdims=True) + inter * qn
            out_ref[0, :, sl] = num / jnp.maximum(jnp.abs(den), jnp.exp(-m_t))

            w_log = a - b_col + li_col
            m_loc = jnp.max(w_log, axis=0, keepdims=True)
            kw = k.astype(F32) * jnp.exp(w_log - m_loc)
            c_loc = _dot_tn(kw.astype(BF16), v)
            n_loc = jnp.sum(kw, axis=0, keepdims=True)
            m_new = jnp.maximum(a + m_s, m_loc)
            d_old = jnp.exp(a + m_s - m_new)
            d_new = jnp.exp(m_loc - m_new)
            c_ref[r] = d_old * c_s + d_new * c_loc
            n_ref[r] = d_old * n_s + d_new * n_loc
            m_ref[r] = jnp.broadcast_to(m_new, (1, MLSTM_DH))


def _mlstm(q_lat, k_all, v_all, g_all, n_ctx_chunks):
    B, T, _ = q_lat.shape
    L = MLSTM_CHUNK
    nl = T // L
    nc = n_ctx_chunks
    steps = nc + nl

    def fwd_c(b, s):
        return (b, s, 0)

    def bwd_c(b, s):
        return (b, jnp.where(s < nc, nc - 1 - s, steps + nc - 1 - s), 0)

    def fwd_q(b, s):
        return (b, jnp.maximum(s - nc, 0), 0)

    def bwd_q(b, s):
        return (b, jnp.minimum(steps - 1 - s, nl - 1), 0)

    blk = lambda w, im: pl.BlockSpec((1, L, w), im)
    nheads = 2 * MLSTM_HEADS
    return pl.pallas_call(
        _mlstm_kernel,
        grid=(B, steps),
        in_specs=[blk(MLSTM_W, fwd_q), blk(MLSTM_W, bwd_q),
                  blk(MLSTM_W, fwd_c), blk(MLSTM_W, bwd_c),
                  blk(MLSTM_W, fwd_c), blk(MLSTM_W, bwd_c),
                  blk(GATE_LANES, fwd_c), blk(GATE_LANES, bwd_c)],
        out_specs=[blk(MLSTM_W, fwd_q), blk(MLSTM_W, bwd_q)],
        out_shape=[jax.ShapeDtypeStruct((B, T, MLSTM_W), F32)] * 2,
        scratch_shapes=[pltpu.VMEM((nheads, MLSTM_DH, MLSTM_DH), F32),
                        pltpu.VMEM((nheads, 1, MLSTM_DH), F32),
                        pltpu.VMEM((nheads, 1, MLSTM_DH), F32)],
        compiler_params=_cparams(("parallel", "arbitrary")),
        name="mlstm",
    )(q_lat, q_lat, k_all, k_all, v_all, v_all, g_all, g_all)


def _attn_kernel(q_ref, kp_ref, kc_ref, kn_ref, vp_ref, vc_ref, vn_ref, kx_ref, vx_ref,
                 sink_ref, o_ref, *, nb):
    n = pl.program_id(1)
    blk = ATTN_BLOCK
    ctx = kx_ref.shape[1]
    kcat = jnp.concatenate([kp_ref[0], kc_ref[0], kn_ref[0], kx_ref[0]], axis=0)
    vcat = jnp.concatenate([vp_ref[0], vc_ref[0], vn_ref[0], vx_ref[0]], axis=0)

    nk = 3 * blk + ctx
    qi = lax.broadcasted_iota(jnp.int32, (blk, nk), 0)
    kj = lax.broadcasted_iota(jnp.int32, (blk, nk), 1)
    in_prev = (kj < blk) & (kj >= qi) & (n > 0)
    in_cur = (kj >= blk) & (kj < 2 * blk)
    in_next = (kj >= 2 * blk) & (kj < 3 * blk) & (kj - 2 * blk <= qi) & (n < nb - 1)
    valid = in_prev | in_cur | in_next | (kj >= 3 * blk)

    lane = lax.broadcasted_iota(jnp.int32, (blk, 2 * ATTN_DH), 1)
    low = lane < ATTN_DH
    for c in range(ATTN_Q_W // (2 * ATTN_DH)):
        qc = q_ref[0, :, c * 128:(c + 1) * 128]
        halves = []
        for half in range(2):
            qe = jnp.where(low if half == 0 else ~low, qc, jnp.zeros_like(qc))
            sc = jnp.where(valid, _dot_nt(qe, kcat), NEG_INF)
            sink = sink_ref[2 * c + half:2 * c + half + 1, 0:1]
            m = jnp.maximum(jnp.max(sc, axis=1, keepdims=True), sink)
            p = jnp.exp(sc - m)
            den = jnp.sum(p, axis=1, keepdims=True) + jnp.exp(sink - m)
            halves.append(_dot((p / den).astype(BF16), vcat))
        o_ref[0, :, c * 128:(c + 1) * 128] = jnp.where(low, halves[0], halves[1]).astype(BF16)


def _attn(aq, ak, av, akx, avx, sink8):
    B, T, _ = aq.shape
    blk = ATTN_BLOCK
    nb = T // blk
    ctx = akx.shape[1]
    kv = lambda im: pl.BlockSpec((1, blk, ATTN_KV_W), im)
    prev = lambda b, n: (b, jnp.maximum(n - 1, 0), 0)
    cur = lambda b, n: (b, n, 0)
    nxt = lambda b, n: (b, jnp.minimum(n + 1, nb - 1), 0)
    cx = pl.BlockSpec((1, ctx, ATTN_KV_W), lambda b, n: (b, 0, 0))
    return pl.pallas_call(
        functools.partial(_attn_kernel, nb=nb),
        grid=(B, nb),
        in_specs=[pl.BlockSpec((1, blk, ATTN_Q_W), cur),
                  kv(prev), kv(cur), kv(nxt), kv(prev), kv(cur), kv(nxt), cx, cx,
                  pl.BlockSpec((8, 128), lambda b, n: (0, 0))],
        out_specs=pl.BlockSpec((1, blk, ATTN_Q_W), cur),
        out_shape=jax.ShapeDtypeStruct((B, T, ATTN_Q_W), BF16),
        compiler_params=_cparams(("parallel", "parallel")),
        name="attn",
    )(aq, ak, ak, ak, av, av, av, akx, avx, sink8)


def _merge_kernel(x_ref, sh_ref, sc_ref, g_ref, g1_ref, hf_ref, hb_ref, a_ref,
                  wg_ref, ng_ref, bdm_ref, wbm_ref, wba_ref, wo_ref, o_ref):
    x = x_ref[0]
    h = _modulated(x, g_ref[...], sh_ref[0], sc_ref[0]).astype(BF16)
    pg = _dot(h, wg_ref[...])
    d = x.shape[1]
    hs = hf_ref[0] + hb_ref[0]
    ss = _segsum(hs * hs, bdm_ref[...])
    m = hs * lax.rsqrt(ss * (1.0 / MLSTM_DH) + EPS) * ng_ref[...] * _sigmoid(pg[:, 0:MLSTM_W])
    mm = _dot(m.astype(BF16), wbm_ref[...])
    aa = _dot(a_ref[0], wba_ref[...])
    y = _sigmoid(pg[:, MLSTM_W:MLSTM_W + d]) * mm + _sigmoid(pg[:, MLSTM_W + d:]) * aa
    o_ref[0] = x + g1_ref[0] * _dot(y.astype(BF16), wo_ref[...])


def _merge(x, shift, scale, g, g1, hf, hb, a, wg, ng, bdm, wbm, wba, wo, *, tm):
    B, T, D = x.shape
    full = lambda arr: pl.BlockSpec(arr.shape, lambda b, t: (0,) * arr.ndim)
    tok = lambda w: pl.BlockSpec((1, tm, w), lambda b, t: (b, t, 0))
    perb = pl.BlockSpec((1, 1, D), lambda b, t: (b, 0, 0))
    return pl.pallas_call(
        _merge_kernel,
        grid=(B, T // tm),
        in_specs=[tok(D), perb, perb, full(g), perb, tok(MLSTM_W), tok(MLSTM_W), tok(ATTN_Q_W),
                  full(wg), full(ng), full(bdm), full(wbm), full(wba), full(wo)],
        out_specs=tok(D),
        out_shape=jax.ShapeDtypeStruct((B, T, D), F32),
        compiler_params=_cparams(("parallel", "parallel")),
        name="merge",
    )(x, shift, scale, g, g1, hf, hb, a, wg, ng, bdm, wbm, wba, wo)


def _pquery_kernel(x_ref, sh_ref, sc_ref, g_ref, wq_ref, sk_ref, h_ref, s_ref):
    x = x_ref[0]
    h = _modulated(x, g_ref[...], sh_ref[0], sc_ref[0]).astype(BF16)
    h_ref[0] = h
    q = _dot(h, wq_ref[...]).astype(BF16)
    for j in range(2 * PEER_HEADS):
        s_ref[0, j] = _dot_nt(sk_ref[j], q[:, j * PEER_HALF:(j + 1) * PEER_HALF])


def _pquery(x, shift, scale, g, wq, sk, *, tm):
    B, T, D = x.shape
    nj = 2 * PEER_HEADS
    full = lambda arr: pl.BlockSpec(arr.shape, lambda b, t: (0,) * arr.ndim)
    perb = pl.BlockSpec((1, 1, D), lambda b, t: (b, 0, 0))
    return pl.pallas_call(
        _pquery_kernel,
        grid=(B, T // tm),
        in_specs=[pl.BlockSpec((1, tm, D), lambda b, t: (b, t, 0)), perb, perb,
                  full(g), full(wq), full(sk)],
        out_specs=[pl.BlockSpec((1, tm, D), lambda b, t: (b, t, 0)),
                   pl.BlockSpec((1, nj, PEER_N_KEYS, tm), lambda b, t: (b, 0, 0, t))],
        out_shape=[jax.ShapeDtypeStruct((B, T, D), BF16),
                   jax.ShapeDtypeStruct((B, nj, PEER_N_KEYS, T), F32)],
        compiler_params=_cparams(("parallel", "parallel")),
        name="pquery",
    )(x, shift, scale, g, wq, sk)


def _top_values(val, k):
    tops = []
    for _ in range(k):
        m = jnp.max(val, axis=0, keepdims=True)
        tops.append(m)
        val = jnp.where(val == m, NEG_INF, val)
    return tops


def _gelu_tanh(x):
    return 0.5 * x * (1.0 + jnp.tanh(math.sqrt(2.0 / math.pi) * (x + 0.044715 * (x * x * x))))


def _peer_router(s_ref, s1_ref, cut_ref, cw_ref, e2_ref):
    k = PEER_TOPK
    for h in range(PEER_HEADS):
        s1 = s_ref[0, 2 * h]
        s2 = s_ref[0, 2 * h + 1]
        a = _top_values(s1, k)
        b = _top_values(s2, k)
        cand = [a[p] + b[q] for p in range(k) for q in range(k) if (p + 1) * (q + 1) <= k]
        cv = jnp.concatenate(cand, axis=0)
        thr = _top_values(cv, k)[k - 1]
        top = a[0] + b[0]
        z = jnp.sum(jnp.where(cv >= thr, jnp.exp(cv - top), 0.0), axis=0, keepdims=True)
        cut = jnp.full(s2.shape, POS_INF, F32)
        for p in range(k):
            cut = jnp.where(a[p] + s2 >= thr, a[p], cut)
        s1_ref[h] = s1
        cut_ref[h] = cut
        cw_ref[h] = jnp.exp(s1 - a[0]) / z
        e2_ref[h] = jnp.exp(s2 - b[0])


def _peer_kernel(h_ref, s_ref, u_ref, vt_ref, x_ref, g2_ref, o_ref,
                 acc_ref, s1_ref, cut_ref, cw_ref, e2_ref, *, eb):
    e = pl.program_id(2)
    ne = pl.num_programs(2)

    @pl.when(e == 0)
    def _():
        acc_ref[...] = jnp.zeros_like(acc_ref)
        _peer_router(s_ref, s1_ref, cut_ref, cw_ref, e2_ref)

    act = _dot_nt(u_ref[...], h_ref[0])
    n_i = eb // PEER_N_KEYS
    w_rows = []
    for ii in range(n_i):
        i = e * n_i + ii
        w = None
        for h in range(PEER_HEADS):
            s1_i = s1_ref[h, pl.ds(i, 1), :]
            cw_i = cw_ref[h, pl.ds(i, 1), :]
            term = jnp.where(s1_i >= cut_ref[h], e2_ref[h], 0.0) * cw_i
            w = term if w is None else w + term
        w_rows.append((w * _gelu_tanh(act[ii * PEER_N_KEYS:(ii + 1) * PEER_N_KEYS])).astype(BF16))
    wt = jnp.concatenate(w_rows, axis=0)
    acc_ref[...] += _dot(vt_ref[...], wt)

    @pl.when(e == ne - 1)
    def _():
        o_ref[0] = x_ref[0] + g2_ref[0] * acc_ref[...].T


def _peer(h2, s_t, u, vt, x1, g2, *, tm, eb):
    B, T, D = x1.shape
    ne = u.shape[0] // eb
    nj = 2 * PEER_HEADS
    return pl.pallas_call(
        functools.partial(_peer_kernel, eb=eb),
        grid=(B, T // tm, ne),
        in_specs=[pl.BlockSpec((1, tm, D), lambda b, t, e: (b, t, 0)),
                  pl.BlockSpec((1, nj, PEER_N_KEYS, tm), lambda b, t, e: (b, 0, 0, t)),
                  pl.BlockSpec((eb, D), lambda b, t, e: (e, 0)),
                  pl.BlockSpec((D, eb), lambda b, t, e: (0, e)),
                  pl.BlockSpec((1, tm, D), lambda b, t, e: (b, t, 0)),
                  pl.BlockSpec((1, 1, D), lambda b, t, e: (b, 0, 0))],
        out_specs=pl.BlockSpec((1, tm, D), lambda b, t, e: (b, t, 0)),
        out_shape=jax.ShapeDtypeStruct((B, T, D), F32),
        scratch_shapes=[pltpu.VMEM((D, tm), F32)]
        + [pltpu.VMEM((PEER_HEADS, PEER_N_KEYS, tm), F32)] * 4,
        compiler_params=_cparams(("parallel", "parallel", "arbitrary")),
        name="peer",
    )(h2, s_t, u, vt, x1, g2)


_Q_HEAD_ORDER = (0, 4, 1, 5, 2, 6, 3, 7)


def _block_diag_ones(width, seg):
    i = jnp.arange(width)
    return (i[:, None] // seg == i[None, :] // seg).astype(BF16)


def _rope_tables(T):
    rows = T // GRID_W
    nf = ATTN_DH // 4
    row = jnp.broadcast_to(jnp.arange(rows)[:, None], (rows, GRID_W)).reshape(T)
    col = jnp.broadcast_to(jnp.arange(GRID_W)[None, :], (rows, GRID_W)).reshape(T)
    inv = ROPE_BASE ** (-jnp.arange(nf, dtype=F32) / nf)
    ang = jnp.concatenate([row[:, None].astype(F32) * inv, col[:, None].astype(F32) * inv], axis=-1)
    cos, sin = jnp.cos(ang), jnp.sin(ang)
    cos_t = jnp.concatenate([cos, cos] * ATTN_KV_HEADS, axis=-1)
    sin_t = jnp.concatenate([-sin, sin] * ATTN_KV_HEADS, axis=-1)
    return cos_t, sin_t


def _layer(x, ctx, mod, norm_mix_g, norm_ffn_g, w_in, b_mgates, mlstm_norm_g, attn_q_norm_g,
           attn_k_norm_g, attn_sink, w_branch_m, w_branch_a, w_out, peer_w_query, peer_sub_keys,
           peer_u, peer_v, cos_t, sin_t):
    B, T, D = x.shape
    n_ctx = ctx.shape[1]
    mods = jnp.split(mod, 6, axis=-1)
    lat = [m[0:B].reshape(B, 1, D) for m in mods]
    cxm = [jnp.broadcast_to(m[B:B + 1].reshape(1, 1, D), (B, 1, D)) for m in mods]
    sh1, sc1, g1, sh2, sc2, g2 = lat

    W = MLSTM_W
    o0 = 4 * W
    o1 = o0 + 2 * 2 * MLSTM_HEADS
    w_mq, w_mk, w_mv, w_mo = (w_in[:, i * W:(i + 1) * W] for i in range(4))
    w_g16 = w_in[:, o0:o1]
    w_aq = w_in[:, o1:o1 + ATTN_Q_W].reshape(D, ATTN_Q_HEADS, ATTN_DH)
    w_aq = w_aq[:, jnp.array(_Q_HEAD_ORDER), :].reshape(D, ATTN_Q_W)
    o2 = o1 + ATTN_Q_W
    w_ak = w_in[:, o2:o2 + ATTN_KV_W]
    w_av = w_in[:, o2 + ATTN_KV_W:o2 + 2 * ATTN_KV_W]
    w_gates = w_in[:, o2 + 2 * ATTN_KV_W:]
    w_g128 = jnp.pad(w_g16, ((0, 0), (0, GATE_LANES - w_g16.shape[1])))
    w_pack = jnp.concatenate([w_mq, w_mk, w_mv, w_aq, w_ak, w_av, w_g128], axis=1).astype(BF16)
    w_merge_in = jnp.concatenate([w_mo, w_gates], axis=1).astype(BF16)
    bg = jnp.pad(b_mgates, (0, GATE_LANES - b_mgates.shape[0])).reshape(1, GATE_LANES)
    qg = jnp.tile(attn_q_norm_g, ATTN_Q_HEADS).reshape(1, ATTN_Q_W)
    kg = jnp.tile(attn_k_norm_g, ATTN_KV_HEADS).reshape(1, ATTN_KV_W)
    bdq = _block_diag_ones(ATTN_Q_W, ATTN_DH)
    bdk = _block_diag_ones(ATTN_KV_W, ATTN_DH)
    bdm = _block_diag_ones(MLSTM_W, MLSTM_DH)
    gmix = norm_mix_g.reshape(1, D)

    tm_lat = min(512, T)
    mq, mk, mv, mg, aq, ak, av = _proj(x, sh1, sc1, gmix, w_pack, bg, qg, kg, bdq, bdk,
                                       cos_t, sin_t, rope=True, tm=tm_lat)
    _, mkc, mvc, mgc, _, akc, avc = _proj(ctx, cxm[0], cxm[1], gmix, w_pack, bg, qg, kg, bdq, bdk,
                                          cos_t, sin_t, rope=False, tm=n_ctx)

    k_all = jnp.concatenate([mkc, mk], axis=1)
    v_all = jnp.concatenate([mvc, mv], axis=1)
    g_all = jnp.concatenate([mgc, mg], axis=1)
    hf, hb = _mlstm(mq, k_all, v_all, g_all, n_ctx // MLSTM_CHUNK)

    sink8 = jnp.broadcast_to(attn_sink[jnp.array(_Q_HEAD_ORDER)].astype(F32)[:, None], (8, 128))
    a = _attn(aq, ak, av, akc, avc, sink8)

    wba = w_branch_a.reshape(ATTN_Q_HEADS, ATTN_DH, D)[jnp.array(_Q_HEAD_ORDER)]
    wba = wba.reshape(ATTN_Q_W, D).astype(BF16)
    x1 = _merge(x, sh1, sc1, gmix, g1, hf, hb, a, w_merge_in, mlstm_norm_g.reshape(1, MLSTM_W),
                bdm, w_branch_m.astype(BF16), wba, w_out.astype(BF16), tm=min(256, T))

    sk = peer_sub_keys.reshape(2 * PEER_HEADS, PEER_N_KEYS, PEER_HALF).astype(BF16)
    h2, s_t = _pquery(x1, sh2, sc2, norm_ffn_g.reshape(1, D), peer_w_query.astype(BF16), sk,
                      tm=min(512, T))
    u = peer_u.astype(BF16)
    vt = peer_v.T.astype(BF16)
    return _peer(h2, s_t, u, vt, x1, g2, tm=min(512, T), eb=512)


def kernel(x, c, ctx, c_ctx, w_ada, b_ada, norm_mix_g, norm_ffn_g, w_in, b_mgates, mlstm_norm_g,
           attn_q_norm_g, attn_k_norm_g, attn_sink, w_branch_m, w_branch_a, w_out, peer_w_query,
           peer_sub_keys, peer_u, peer_v):
    B, T, D = x.shape
    depth = w_ada.shape[0]
    assert depth == 1, "context-stream update for deeper stacks is not implemented"
    cos_t, sin_t = _rope_tables(T)
    cond8 = jnp.zeros((8, D), F32).at[0:B].set(c).at[B].set(c_ctx)
    for layer in range(depth):
        mod = _adaln(cond8, w_ada[layer], b_ada[layer])
        x = _layer(x, ctx, mod, norm_mix_g[layer], norm_ffn_g[layer], w_in[layer], b_mgates[layer],
                   mlstm_norm_g[layer], attn_q_norm_g[layer], attn_k_norm_g[layer], attn_sink[layer],
                   w_branch_m[layer], w_branch_a[layer], w_out[layer], peer_w_query[layer],
                   peer_sub_keys[layer], peer_u[layer], peer_v[layer], cos_t, sin_t)
    return x
```

```python
import functools
import math

import jax
import jax.numpy as jnp
from jax import lax
from jax.experimental import pallas as pl
from jax.experimental.pallas import tpu as pltpu

F32 = jnp.float32
BF16 = jnp.bfloat16

EPS = 1e-6
GRID_W = 64
ROPE_BASE = 10000.0

MLSTM_HEADS = 4
MLSTM_DH = 128
MLSTM_W = MLSTM_HEADS * MLSTM_DH
MLSTM_CHUNK = 128
GATE_LANES = 128

ATTN_Q_HEADS = 8
ATTN_KV_HEADS = 2
ATTN_DH = 64
ATTN_Q_W = ATTN_Q_HEADS * ATTN_DH
ATTN_KV_W = ATTN_KV_HEADS * ATTN_DH
ATTN_BLOCK = 128

PEER_HEADS = 8
PEER_N_KEYS = 128
PEER_HALF = 128
PEER_TOPK = 16

VMEM_LIMIT_BYTES = 56 * 1024 * 1024

NEG_INF = float("-inf")
POS_INF = float("inf")


def _cparams(sem):
    return pltpu.CompilerParams(dimension_semantics=sem, vmem_limit_bytes=VMEM_LIMIT_BYTES)


def _dot(a, b):
    return jnp.dot(a, b, preferred_element_type=F32)


def _dot_nt(a, b):
    return lax.dot_general(a, b, (((1,), (1,)), ((), ())), preferred_element_type=F32)


def _dot_tn(a, b):
    return lax.dot_general(a, b, (((0,), (0,)), ((), ())), preferred_element_type=F32)


def _sigmoid(x):
    return 1.0 / (1.0 + jnp.exp(-x))


def _segsum(x2, bd):
    hi = x2.astype(BF16)
    lo = (x2 - hi.astype(F32)).astype(BF16)
    return _dot(hi, bd) + _dot(lo, bd)


def _modulated(x, g, shift, scale):
    ms = jnp.mean(x * x, axis=-1, keepdims=True)
    return (x * lax.rsqrt(ms + EPS)) * g * (1.0 + scale) + shift


def _adaln_kernel(c_ref, w_ref, b_ref, o_ref):
    c = c_ref[...]
    s = c * _sigmoid(c)
    o_ref[...] = jnp.dot(s, w_ref[...], preferred_element_type=F32,
                         precision=lax.Precision.HIGHEST) + b_ref[...]


def _adaln(cond8, w, b):
    d = cond8.shape[1]
    n = w.shape[1]
    bn = n // 6
    return pl.pallas_call(
        _adaln_kernel,
        grid=(n // bn,),
        in_specs=[pl.BlockSpec((8, d), lambda j: (0, 0)),
                  pl.BlockSpec((d, bn), lambda j: (0, j)),
                  pl.BlockSpec((1, bn), lambda j: (0, j))],
        out_specs=pl.BlockSpec((8, bn), lambda j: (0, j)),
        out_shape=jax.ShapeDtypeStruct((8, n), F32),
        compiler_params=_cparams(("arbitrary",)),
        name="adaln",
    )(cond8, w, b.reshape(1, n))


_C_MQ, _C_MK, _C_MV = 0, 512, 1024
_C_AQ, _C_AK, _C_AV, _C_MG = 1536, 2048, 2176, 2304
_C_END = 2432


def _rope(t, cos_t, sin_t, width):
    lane = lax.broadcasted_iota(jnp.int32, t.shape, 1)
    lower = (lane % ATTN_DH) < (ATTN_DH // 2)
    swapped = jnp.where(lower, pltpu.roll(t, width - ATTN_DH // 2, 1),
                        pltpu.roll(t, ATTN_DH // 2, 1))
    return t * cos_t + swapped * sin_t


def _proj_kernel(x_ref, sh_ref, sc_ref, g_ref, w_ref, bg_ref, qg_ref, kg_ref,
                 bdq_ref, bdk_ref, cos_ref, sin_ref,
                 mq_ref, mk_ref, mv_ref, mg_ref, aq_ref, ak_ref, av_ref, *, rope):
    x = x_ref[0]
    h = _modulated(x, g_ref[...], sh_ref[0], sc_ref[0]).astype(BF16)

    p = _dot(h, w_ref[:, _C_MQ:_C_AQ])
    mq_ref[0] = p[:, 0:512].astype(BF16)
    mk_ref[0] = (p[:, 512:1024] * (MLSTM_DH ** -0.5)).astype(BF16)
    mv_ref[0] = p[:, 1024:1536].astype(BF16)

    pq = _dot(h, w_ref[:, _C_AQ:_C_AK])
    ssq = _segsum(pq * pq, bdq_ref[...])
    q = pq * lax.rsqrt(ssq * (1.0 / ATTN_DH) + EPS) * qg_ref[...]
    if rope:
        cos_t = cos_ref[...]
        sin_t = sin_ref[...]
        q = _rope(q, jnp.concatenate([cos_t] * 4, axis=1), jnp.concatenate([sin_t] * 4, axis=1),
                  ATTN_Q_W)
    aq_ref[0] = (q * (ATTN_DH ** -0.5)).astype(BF16)

    pk = _dot(h, w_ref[:, _C_AK:_C_AV])
    ssk = _segsum(pk * pk, bdk_ref[...])
    k = pk * lax.rsqrt(ssk * (1.0 / ATTN_DH) + EPS) * kg_ref[...]
    if rope:
        k = _rope(k, cos_ref[...], sin_ref[...], ATTN_KV_W)
    ak_ref[0] = k.astype(BF16)

    av_ref[0] = _dot(h, w_ref[:, _C_AV:_C_MG]).astype(BF16)

    z = _dot(h, w_ref[:, _C_MG:_C_END]) + bg_ref[...]
    lane = lax.broadcasted_iota(jnp.int32, z.shape, 1)
    is_forget = (lane % 8) >= MLSTM_HEADS
    logsig = jnp.minimum(z, 0.0) - jnp.log1p(jnp.exp(-jnp.abs(z)))
    mg_ref[0] = jnp.where(is_forget, logsig, z)


def _proj(x, shift, scale, g, w_pack, bg, qg, kg, bdq, bdk, cos_t, sin_t, *, rope, tm):
    B, T, D = x.shape
    nt = T // tm
    full = lambda shape: pl.BlockSpec(shape, lambda b, t: (0,) * len(shape))
    tok = lambda w: pl.BlockSpec((1, tm, w), lambda b, t: (b, t, 0))
    outs = [(MLSTM_W, BF16), (MLSTM_W, BF16), (MLSTM_W, BF16), (GATE_LANES, F32),
            (ATTN_Q_W, BF16), (ATTN_KV_W, BF16), (ATTN_KV_W, BF16)]
    return pl.pallas_call(
        functools.partial(_proj_kernel, rope=rope),
        grid=(B, nt),
        in_specs=[tok(D),
                  pl.BlockSpec((1, 1, D), lambda b, t: (b, 0, 0)),
                  pl.BlockSpec((1, 1, D), lambda b, t: (b, 0, 0)),
                  full((1, D)), full(w_pack.shape), full((1, GATE_LANES)),
                  full((1, ATTN_Q_W)), full((1, ATTN_KV_W)),
                  full((ATTN_Q_W, ATTN_Q_W)), full((ATTN_KV_W, ATTN_KV_W)),
                  pl.BlockSpec((tm, ATTN_KV_W), lambda b, t: (t, 0)),
                  pl.BlockSpec((tm, ATTN_KV_W), lambda b, t: (t, 0))],
        out_specs=[tok(w) for w, _ in outs],
        out_shape=[jax.ShapeDtypeStruct((B, T, w), dt) for w, dt in outs],
        compiler_params=_cparams(("parallel", "parallel")),
        name="proj_rope" if rope else "proj_ctx",
    )(x, shift, scale, g, w_pack, bg, qg, kg, bdq, bdk, cos_t, sin_t)


def _mlstm_kernel(qf_ref, qb_ref, kf_ref, kb_ref, vf_ref, vb_ref, gf_ref, gb_ref,
                  hf_ref, hb_ref, c_ref, n_ref, m_ref):
    s = pl.program_id(1)
    L = MLSTM_CHUNK

    @pl.when(s == 0)
    def _():
        c_ref[...] = jnp.zeros_like(c_ref)
        n_ref[...] = jnp.zeros_like(n_ref)
        m_ref[...] = jnp.zeros_like(m_ref)

    row = lax.broadcasted_iota(jnp.int32, (L, L), 0)
    col = lax.broadcasted_iota(jnp.int32, (L, L), 1)
    tri_f = (col <= row).astype(F32)

    for d in range(2):
        q_all = (qf_ref, qb_ref)[d][0]
        k_all = (kf_ref, kb_ref)[d][0]
        v_all = (vf_ref, vb_ref)[d][0]
        g = (gf_ref, gb_ref)[d][0]
        out_ref = (hf_ref, hb_ref)[d]

        cum_f = jnp.dot(tri_f, g, preferred_element_type=F32, precision=lax.Precision.HIGHEST)
        if d == 0:
            b_all = cum_f
            seen = col <= row
            last = L - 1
        else:
            b_all = cum_f[L - 1:L, :] - cum_f + g
            seen = col >= row
            last = 0
        g_t = g.T
        b_t = b_all.T

        for hh in range(MLSTM_HEADS):
            r = d * MLSTM_HEADS + hh
            ci = d * 2 * MLSTM_HEADS + hh
            cf = ci + MLSTM_HEADS
            sl = slice(hh * MLSTM_DH, (hh + 1) * MLSTM_DH)
            q = q_all[:, sl]
            k = k_all[:, sl]
            v = v_all[:, sl]
            li_col = g[:, ci:ci + 1]
            b_col = b_all[:, cf:cf + 1]
            li_row = g_t[ci:ci + 1, :]
            b_row = b_t[cf:cf + 1, :]
            a = b_all[last:last + 1, cf:cf + 1]

            c_s = c_ref[r]
            n_s = n_ref[r]
            m_s = m_ref[r][:, 0:1]

            log_inter = b_col + m_s
            log_d = jnp.where(seen, b_col - b_row + li_row, NEG_INF)
            m_t = jnp.maximum(log_inter, jnp.max(log_d, axis=1, keepdims=True))
            sc = _dot_nt(q, k) * jnp.exp(log_d - m_t)
            inter = jnp.exp(log_inter - m_t)
            num = _dot(sc.astype(BF16), v) + inter * _dot(q, c_s.astype(BF16))
            qn = jnp.sum(q.astype(F32) * n_s, axis=1, keepdims=True)
            den = jnp.sum(sc, axis=1, keepdims=True) + inter * qn
            out_ref[0, :, sl] = num / jnp.maximum(jnp.abs(den), jnp.exp(-m_t))

            w_log = a - b_col + li_col
            m_loc = jnp.max(w_log, axis=0, keepdims=True)
            kw = k.astype(F32) * jnp.exp(w_log - m_loc)
            c_loc = _dot_tn(kw.astype(BF16), v)
            n_loc = jnp.sum(kw, axis=0, keepdims=True)
            m_new = jnp.maximum(a + m_s, m_loc)
            d_old = jnp.exp(a + m_s - m_new)
            d_new = jnp.exp(m_loc - m_new)
            c_ref[r] = d_old * c_s + d_new * c_loc
            n_ref[r] = d_old * n_s + d_new * n_loc
            m_ref[r] = jnp.broadcast_to(m_new, (1, MLSTM_DH))


def _mlstm(q_lat, k_all, v_all, g_all, n_ctx_chunks):
    B, T, _ = q_lat.shape
    L = MLSTM_CHUNK
    nl = T // L
    nc = n_ctx_chunks
    steps = nc + nl

    def fwd_c(b, s):
        return (b, s, 0)

    def bwd_c(b, s):
        return (b, jnp.where(s < nc, nc - 1 - s, steps + nc - 1 - s), 0)

    def fwd_q(b, s):
        return (b, jnp.maximum(s - nc, 0), 0)

    def bwd_q(b, s):
        return (b, jnp.minimum(steps - 1 - s, nl - 1), 0)

    blk = lambda w, im: pl.BlockSpec((1, L, w), im)
    nheads = 2 * MLSTM_HEADS
    return pl.pallas_call(
        _mlstm_kernel,
        grid=(B, steps),
        in_specs=[blk(MLSTM_W, fwd_q), blk(MLSTM_W, bwd_q),
                  blk(MLSTM_W, fwd_c), blk(MLSTM_W, bwd_c),
                  blk(MLSTM_W, fwd_c), blk(MLSTM_W, bwd_c),
                  blk(GATE_LANES, fwd_c), blk(GATE_LANES, bwd_c)],
        out_specs=[blk(MLSTM_W, fwd_q), blk(MLSTM_W, bwd_q)],
        out_shape=[jax.ShapeDtypeStruct((B, T, MLSTM_W), F32)] * 2,
        scratch_shapes=[pltpu.VMEM((nheads, MLSTM_DH, MLSTM_DH), F32),
                        pltpu.VMEM((nheads, 1, MLSTM_DH), F32),
                        pltpu.VMEM((nheads, 1, MLSTM_DH), F32)],
        compiler_params=_cparams(("parallel", "arbitrary")),
        name="mlstm",
    )(q_lat, q_lat, k_all, k_all, v_all, v_all, g_all, g_all)


def _attn_kernel(q_ref, kp_ref, kc_ref, kn_ref, vp_ref, vc_ref, vn_ref, kx_ref, vx_ref,
                 sink_ref, o_ref, *, nb):
    n = pl.program_id(1)
    blk = ATTN_BLOCK
    ctx = kx_ref.shape[1]
    kcat = jnp.concatenate([kp_ref[0], kc_ref[0], kn_ref[0], kx_ref[0]], axis=0)
    vcat = jnp.concatenate([vp_ref[0], vc_ref[0], vn_ref[0], vx_ref[0]], axis=0)

    nk = 3 * blk + ctx
    qi = lax.broadcasted_iota(jnp.int32, (blk, nk), 0)
    kj = lax.broadcasted_iota(jnp.int32, (blk, nk), 1)
    in_prev = (kj < blk) & (kj >= qi) & (n > 0)
    in_cur = (kj >= blk) & (kj < 2 * blk)
    in_next = (kj >= 2 * blk) & (kj < 3 * blk) & (kj - 2 * blk <= qi) & (n < nb - 1)
    valid = in_prev | in_cur | in_next | (kj >= 3 * blk)

    lane = lax.broadcasted_iota(jnp.int32, (blk, 2 * ATTN_DH), 1)
    low = lane < ATTN_DH
    for c in range(ATTN_Q_W // (2 * ATTN_DH)):
        qc = q_ref[0, :, c * 128:(c + 1) * 128]
        halves = []
        for half in range(2):
            qe = jnp.where(low if half == 0 else ~low, qc, jnp.zeros_like(qc))
            sc = jnp.where(valid, _dot_nt(qe, kcat), NEG_INF)
            sink = sink_ref[2 * c + half:2 * c + half + 1, 0:1]
            m = jnp.maximum(jnp.max(sc, axis=1, keepdims=True), sink)
            p = jnp.exp(sc - m)
            den = jnp.sum(p, axis=1, keepdims=True) + jnp.exp(sink - m)
            halves.append(_dot((p / den).astype(BF16), vcat))
        o_ref[0, :, c * 128:(c + 1) * 128] = jnp.where(low, halves[0], halves[1]).astype(BF16)


def _attn(aq, ak, av, akx, avx, sink8):
    B, T, _ = aq.shape
    blk = ATTN_BLOCK
    nb = T // blk
    ctx = akx.shape[1]
    kv = lambda im: pl.BlockSpec((1, blk, ATTN_KV_W), im)
    prev = lambda b, n: (b, jnp.maximum(n - 1, 0), 0)
    cur = lambda b, n: (b, n, 0)
    nxt = lambda b, n: (b, jnp.minimum(n + 1, nb - 1), 0)
    cx = pl.BlockSpec((1, ctx, ATTN_KV_W), lambda b, n: (b, 0, 0))
    return pl.pallas_call(
        functools.partial(_attn_kernel, nb=nb),
        grid=(B, nb),
        in_specs=[pl.BlockSpec((1, blk, ATTN_Q_W), cur),
                  kv(prev), kv(cur), kv(nxt), kv(prev), kv(cur), kv(nxt), cx, cx,
                  pl.BlockSpec((8, 128), lambda b, n: (0, 0))],
        out_specs=pl.BlockSpec((1, blk, ATTN_Q_W), cur),
        out_shape=jax.ShapeDtypeStruct((B, T, ATTN_Q_W), BF16),
        compiler_params=_cparams(("parallel", "parallel")),
        name="attn",
    )(aq, ak, ak, ak, av, av, av, akx, avx, sink8)


def _merge_kernel(x_ref, sh_ref, sc_ref, g_ref, g1_ref, hf_ref, hb_ref, a_ref,
                  wg_ref, ng_ref, bdm_ref, wbm_ref, wba_ref, wo_ref, o_ref):
    x = x_ref[0]
    h = _modulated(x, g_ref[...], sh_ref[0], sc_ref[0]).astype(BF16)
    pg = _dot(h, wg_ref[...])
    d = x.shape[1]
    hs = hf_ref[0] + hb_ref[0]
    ss = _segsum(hs * hs, bdm_ref[...])
    m = hs * lax.rsqrt(ss * (1.0 / MLSTM_DH) + EPS) * ng_ref[...] * _sigmoid(pg[:, 0:MLSTM_W])
    mm = _dot(m.astype(BF16), wbm_ref[...])
    aa = _dot(a_ref[0], wba_ref[...])
    y = _sigmoid(pg[:, MLSTM_W:MLSTM_W + d]) * mm + _sigmoid(pg[:, MLSTM_W + d:]) * aa
    o_ref[0] = x + g1_ref[0] * _dot(y.astype(BF16), wo_ref[...])


def _merge(x, shift, scale, g, g1, hf, hb, a, wg, ng, bdm, wbm, wba, wo, *, tm):
    B, T, D = x.shape
    full = lambda arr: pl.BlockSpec(arr.shape, lambda b, t: (0,) * arr.ndim)
    tok = lambda w: pl.BlockSpec((1, tm, w), lambda b, t: (b, t, 0))
    perb = pl.BlockSpec((1, 1, D), lambda b, t: (b, 0, 0))
    return pl.pallas_call(
        _merge_kernel,
        grid=(B, T // tm),
        in_specs=[tok(D), perb, perb, full(g), perb, tok(MLSTM_W), tok(MLSTM_W), tok(ATTN_Q_W),
                  full(wg), full(ng), full(bdm), full(wbm), full(wba), full(wo)],
        out_specs=tok(D),
        out_shape=jax.ShapeDtypeStruct((B, T, D), F32),
        compiler_params=_cparams(("parallel", "parallel")),
        name="merge",
    )(x, shift, scale, g, g1, hf, hb, a, wg, ng, bdm, wbm, wba, wo)


def _pquery_kernel(x_ref, sh_ref, sc_ref, g_ref, wq_ref, sk_ref, h_ref, s_ref):
    x = x_ref[0]
    h = _modulated(x, g_ref[...], sh_ref[0], sc_ref[0]).astype(BF16)
    h_ref[0] = h
    q = _dot(h, wq_ref[...]).astype(BF16)
    for j in range(2 * PEER_HEADS):
        s_ref[0, j] = _dot_nt(sk_ref[j], q[:, j * PEER_HALF:(j + 1) * PEER_HALF])


def _pquery(x, shift, scale, g, wq, sk, *, tm):
    B, T, D = x.shape
    nj = 2 * PEER_HEADS
    full = lambda arr: pl.BlockSpec(arr.shape, lambda b, t: (0,) * arr.ndim)
    perb = pl.BlockSpec((1, 1, D), lambda b, t: (b, 0, 0))
    return pl.pallas_call(
        _pquery_kernel,
        grid=(B, T // tm),
        in_specs=[pl.BlockSpec((1, tm, D), lambda b, t: (b, t, 0)), perb, perb,
                  full(g), full(wq), full(sk)],
        out_specs=[pl.BlockSpec((1, tm, D), lambda b, t: (b, t, 0)),
                   pl.BlockSpec((1, nj, PEER_N_KEYS, tm), lambda b, t: (b, 0, 0, t))],
        out_shape=[jax.ShapeDtypeStruct((B, T, D), BF16),
                   jax.ShapeDtypeStruct((B, nj, PEER_N_KEYS, T), F32)],
        compiler_params=_cparams(("parallel", "parallel")),
        name="pquery",
    )(x, shift, scale, g, wq, sk)


def _top_values(val, k, with_rank=False):
    tops = []
    rank = jnp.full(val.shape, float(k), F32) if with_rank else None
    for p in range(k):
        m = jnp.max(val, axis=0, keepdims=True)
        tops.append(m)
        hit = val == m
        if with_rank:
            rank = jnp.where(hit, float(p), rank)
        val = jnp.where(hit, NEG_INF, val)
    return (tops, rank) if with_rank else tops


_GELU_K0 = -2.0 * math.sqrt(2.0 / math.pi) * math.log2(math.e)
_GELU_K1 = 0.044715 * _GELU_K0


def _gelu_tanh(x):
    return x / (1.0 + jnp.exp2(x * (x * x * _GELU_K1 + _GELU_K0)))


BF16_ROWS = 16


def _row_tile_bf16(ref, h, i, tm):
    row = jnp.broadcast_to(ref[h, pl.ds(i, 1), :], (BF16_ROWS, tm)).astype(BF16)
    return jnp.concatenate([row] * (PEER_N_KEYS // BF16_ROWS), axis=0)


def _peer_router(s_ref, rank_ref, cw_ref, cnt_ref, e2_ref):
    k = PEER_TOPK
    for h in range(PEER_HEADS):
        s1 = s_ref[0, 2 * h]
        s2 = s_ref[0, 2 * h + 1]
        a, rank = _top_values(s1, k, with_rank=True)
        b = _top_values(s2, k)
        cand = [a[p] + b[q] for p in range(k) for q in range(k) if (p + 1) * (q + 1) <= k]
        cv = jnp.concatenate(cand, axis=0)
        thr = _top_values(cv, k)[k - 1]
        top = a[0] + b[0]
        z = jnp.sum(jnp.where(cv >= thr, jnp.exp(cv - top), 0.0), axis=0, keepdims=True)
        cnt = jnp.zeros(s2.shape, F32)
        for p in range(k):
            cnt = jnp.where(a[p] + s2 >= thr, float(p + 1), cnt)
        rank_ref[h] = rank
        cw_ref[h] = jnp.exp(s1 - a[0]) / z
        cnt_ref[h] = cnt.astype(BF16)
        e2_ref[h] = jnp.exp(s2 - b[0]).astype(BF16)


def _peer_kernel(h_ref, s_ref, u_ref, vt_ref, x_ref, g2_ref, o_ref,
                 acc_ref, rank_ref, cw_ref, cnt_ref, e2_ref, *, eb, sb):
    e = pl.program_id(2)
    ne = pl.num_programs(2)
    tm = h_ref.shape[1]

    @pl.when(e == 0)
    def _():
        acc_ref[...] = jnp.zeros_like(acc_ref)
        _peer_router(s_ref, rank_ref, cw_ref, cnt_ref, e2_ref)

    hq = h_ref[0]
    n_i = sb // PEER_N_KEYS
    n_sub = eb // sb
    scores = lambda j: _dot_nt(u_ref[j * sb:(j + 1) * sb, :], hq)
    zero = jnp.zeros((), BF16)
    act_next = scores(0)
    for j in range(n_sub):
        act = act_next
        if j + 1 < n_sub:
            act_next = scores(j + 1)
        w_rows = []
        for ii in range(n_i):
            i = e * (eb // PEER_N_KEYS) + j * n_i + ii
            w = None
            for h in range(PEER_HEADS):
                rank_i = _row_tile_bf16(rank_ref, h, i, tm)
                cw_i = _row_tile_bf16(cw_ref, h, i, tm)
                term = jnp.where(rank_i < cnt_ref[h], e2_ref[h], zero) * cw_i
                w = term if w is None else w + term
            g = _gelu_tanh(act[ii * PEER_N_KEYS:(ii + 1) * PEER_N_KEYS])
            w_rows.append(w * g.astype(BF16))
        wt = jnp.concatenate(w_rows, axis=0)
        acc_ref[...] += _dot(vt_ref[:, j * sb:(j + 1) * sb], wt)

    @pl.when(e == ne - 1)
    def _():
        o_ref[0] = x_ref[0] + g2_ref[0] * acc_ref[...].T


def _peer(h2, s_t, u, vt, x1, g2, *, tm, eb, sb):
    B, T, D = x1.shape
    ne = u.shape[0] // eb
    nj = 2 * PEER_HEADS
    return pl.pallas_call(
        functools.partial(_peer_kernel, eb=eb, sb=sb),
        grid=(B, T // tm, ne),
        in_specs=[pl.BlockSpec((1, tm, D), lambda b, t, e: (b, t, 0)),
                  pl.BlockSpec((1, nj, PEER_N_KEYS, tm), lambda b, t, e: (b, 0, 0, t)),
                  pl.BlockSpec((eb, D), lambda b, t, e: (e, 0)),
                  pl.BlockSpec((D, eb), lambda b, t, e: (0, e)),
                  pl.BlockSpec((1, tm, D), lambda b, t, e: (b, t, 0)),
                  pl.BlockSpec((1, 1, D), lambda b, t, e: (b, 0, 0))],
        out_specs=pl.BlockSpec((1, tm, D), lambda b, t, e: (b, t, 0)),
        out_shape=jax.ShapeDtypeStruct((B, T, D), F32),
        scratch_shapes=[pltpu.VMEM((D, tm), F32),
                        pltpu.VMEM((PEER_HEADS, PEER_N_KEYS, tm), F32),
                        pltpu.VMEM((PEER_HEADS, PEER_N_KEYS, tm), F32),
                        pltpu.VMEM((PEER_HEADS, PEER_N_KEYS, tm), BF16),
                        pltpu.VMEM((PEER_HEADS, PEER_N_KEYS, tm), BF16)],
        compiler_params=_cparams(("parallel", "parallel", "arbitrary")),
        name="peer",
    )(h2, s_t, u, vt, x1, g2)


_Q_HEAD_ORDER = (0, 4, 1, 5, 2, 6, 3, 7)


def _block_diag_ones(width, seg):
    i = jnp.arange(width)
    return (i[:, None] // seg == i[None, :] // seg).astype(BF16)


def _rope_tables(T):
    rows = T // GRID_W
    nf = ATTN_DH // 4
    row = jnp.broadcast_to(jnp.arange(rows)[:, None], (rows, GRID_W)).reshape(T)
    col = jnp.broadcast_to(jnp.arange(GRID_W)[None, :], (rows, GRID_W)).reshape(T)
    inv = ROPE_BASE ** (-jnp.arange(nf, dtype=F32) / nf)
    ang = jnp.concatenate([row[:, None].astype(F32) * inv, col[:, None].astype(F32) * inv], axis=-1)
    cos, sin = jnp.cos(ang), jnp.sin(ang)
    cos_t = jnp.concatenate([cos, cos] * ATTN_KV_HEADS, axis=-1)
    sin_t = jnp.concatenate([-sin, sin] * ATTN_KV_HEADS, axis=-1)
    return cos_t, sin_t


def _layer(x, ctx, mod, norm_mix_g, norm_ffn_g, w_in, b_mgates, mlstm_norm_g, attn_q_norm_g,
           attn_k_norm_g, attn_sink, w_branch_m, w_branch_a, w_out, peer_w_query, peer_sub_keys,
           peer_u, peer_v, cos_t, sin_t):
    B, T, D = x.shape
    n_ctx = ctx.shape[1]
    mods = jnp.split(mod, 6, axis=-1)
    lat = [m[0:B].reshape(B, 1, D) for m in mods]
    cxm = [jnp.broadcast_to(m[B:B + 1].reshape(1, 1, D), (B, 1, D)) for m in mods]
    sh1, sc1, g1, sh2, sc2, g2 = lat

    W = MLSTM_W
    o0 = 4 * W
    o1 = o0 + 2 * 2 * MLSTM_HEADS
    w_mq, w_mk, w_mv, w_mo = (w_in[:, i * W:(i + 1) * W] for i in range(4))
    w_g16 = w_in[:, o0:o1]
    w_aq = w_in[:, o1:o1 + ATTN_Q_W].reshape(D, ATTN_Q_HEADS, ATTN_DH)
    w_aq = w_aq[:, jnp.array(_Q_HEAD_ORDER), :].reshape(D, ATTN_Q_W)
    o2 = o1 + ATTN_Q_W
    w_ak = w_in[:, o2:o2 + ATTN_KV_W]
    w_av = w_in[:, o2 + ATTN_KV_W:o2 + 2 * ATTN_KV_W]
    w_gates = w_in[:, o2 + 2 * ATTN_KV_W:]
    w_g128 = jnp.pad(w_g16, ((0, 0), (0, GATE_LANES - w_g16.shape[1])))
    w_pack = jnp.concatenate([w_mq, w_mk, w_mv, w_aq, w_ak, w_av, w_g128], axis=1).astype(BF16)
    w_merge_in = jnp.concatenate([w_mo, w_gates], axis=1).astype(BF16)
    bg = jnp.pad(b_mgates, (0, GATE_LANES - b_mgates.shape[0])).reshape(1, GATE_LANES)
    qg = jnp.tile(attn_q_norm_g, ATTN_Q_HEADS).reshape(1, ATTN_Q_W)
    kg = jnp.tile(attn_k_norm_g, ATTN_KV_HEADS).reshape(1, ATTN_KV_W)
    bdq = _block_diag_ones(ATTN_Q_W, ATTN_DH)
    bdk = _block_diag_ones(ATTN_KV_W, ATTN_DH)
    bdm = _block_diag_ones(MLSTM_W, MLSTM_DH)
    gmix = norm_mix_g.reshape(1, D)

    tm_lat = min(512, T)
    mq, mk, mv, mg, aq, ak, av = _proj(x, sh1, sc1, gmix, w_pack, bg, qg, kg, bdq, bdk,
                                       cos_t, sin_t, rope=True, tm=tm_lat)
    _, mkc, mvc, mgc, _, akc, avc = _proj(ctx, cxm[0], cxm[1], gmix, w_pack, bg, qg, kg, bdq, bdk,
                                          cos_t, sin_t, rope=False, tm=n_ctx)

    k_all = jnp.concatenate([mkc, mk], axis=1)
    v_all = jnp.concatenate([mvc, mv], axis=1)
    g_all = jnp.concatenate([mgc, mg], axis=1)
    hf, hb = _mlstm(mq, k_all, v_all, g_all, n_ctx // MLSTM_CHUNK)

    sink8 = jnp.broadcast_to(attn_sink[jnp.array(_Q_HEAD_ORDER)].astype(F32)[:, None], (8, 128))
    a = _attn(aq, ak, av, akc, avc, sink8)

    wba = w_branch_a.reshape(ATTN_Q_HEADS, ATTN_DH, D)[jnp.array(_Q_HEAD_ORDER)]
    wba = wba.reshape(ATTN_Q_W, D).astype(BF16)
    x1 = _merge(x, sh1, sc1, gmix, g1, hf, hb, a, w_merge_in, mlstm_norm_g.reshape(1, MLSTM_W),
                bdm, w_branch_m.astype(BF16), wba, w_out.astype(BF16), tm=min(256, T))

    sk = peer_sub_keys.reshape(2 * PEER_HEADS, PEER_N_KEYS, PEER_HALF).astype(BF16)
    h2, s_t = _pquery(x1, sh2, sc2, norm_ffn_g.reshape(1, D), peer_w_query.astype(BF16), sk,
                      tm=min(512, T))
    u = peer_u.astype(BF16)
    vt = peer_v.T.astype(BF16)
    return _peer(h2, s_t, u, vt, x1, g2, tm=min(512, T), eb=2048, sb=512)


def kernel(x, c, ctx, c_ctx, w_ada, b_ada, norm_mix_g, norm_ffn_g, w_in, b_mgates, mlstm_norm_g,
           attn_q_norm_g, attn_k_norm_g, attn_sink, w_branch_m, w_branch_a, w_out, peer_w_query,
           peer_sub_keys, peer_u, peer_v):
    B, T, D = x.shape
    depth = w_ada.shape[0]
    assert depth == 1, "context-stream update for deeper stacks is not implemented"
    cos_t, sin_t = _rope_tables(T)
    cond8 = jnp.zeros((8, D), F32).at[0:B].set(c).at[B].set(c_ctx)
    for layer in range(depth):
        mod = _adaln(cond8, w_ada[layer], b_ada[layer])
        x = _layer(x, ctx, mod, norm_mix_g[layer], norm_ffn_g[layer], w_in[layer], b_mgates[layer],
                   mlstm_norm_g[layer], attn_q_norm_g[layer], attn_k_norm_g[layer], attn_sink[layer],
                   w_branch_m[layer], w_branch_a[layer], w_out[layer], peer_w_query[layer],
                   peer_sub_keys[layer], peer_u[layer], peer_v[layer], cos_t, sin_t)
    return x
```

```python
import functools
import math

import jax
import jax.numpy as jnp
from jax import lax
from jax.experimental import pallas as pl
from jax.experimental.pallas import tpu as pltpu

F32 = jnp.float32
BF16 = jnp.bfloat16

EPS = 1e-6
GRID_W = 64
ROPE_BASE = 10000.0

MLSTM_HEADS = 4
MLSTM_DH = 128
MLSTM_W = MLSTM_HEADS * MLSTM_DH
MLSTM_CHUNK = 128
GATE_LANES = 128

ATTN_Q_HEADS = 8
ATTN_KV_HEADS = 2
ATTN_DH = 64
ATTN_Q_W = ATTN_Q_HEADS * ATTN_DH
ATTN_KV_W = ATTN_KV_HEADS * ATTN_DH
ATTN_BLOCK = 128

PEER_HEADS = 8
PEER_N_KEYS = 128
PEER_HALF = 128
PEER_TOPK = 16

VMEM_LIMIT_BYTES = 56 * 1024 * 1024

NEG_INF = float("-inf")
POS_INF = float("inf")


def _cparams(sem):
    return pltpu.CompilerParams(dimension_semantics=sem, vmem_limit_bytes=VMEM_LIMIT_BYTES)


def _dot(a, b):
    return jnp.dot(a, b, preferred_element_type=F32)


def _dot_nt(a, b):
    return lax.dot_general(a, b, (((1,), (1,)), ((), ())), preferred_element_type=F32)


def _dot_tn(a, b):
    return lax.dot_general(a, b, (((0,), (0,)), ((), ())), preferred_element_type=F32)


def _sigmoid(x):
    return 1.0 / (1.0 + jnp.exp(-x))


def _segsum(x2, bd):
    hi = x2.astype(BF16)
    lo = (x2 - hi.astype(F32)).astype(BF16)
    return _dot(hi, bd) + _dot(lo, bd)


def _modulated(x, g, shift, scale):
    ms = jnp.mean(x * x, axis=-1, keepdims=True)
    return (x * lax.rsqrt(ms + EPS)) * g * (1.0 + scale) + shift


def _adaln_kernel(c_ref, w_ref, b_ref, o_ref):
    c = c_ref[...]
    s = c * _sigmoid(c)
    o_ref[...] = jnp.dot(s, w_ref[...], preferred_element_type=F32,
                         precision=lax.Precision.HIGHEST) + b_ref[...]


def _adaln(cond8, w, b):
    d = cond8.shape[1]
    n = w.shape[1]
    bn = n // 6
    return pl.pallas_call(
        _adaln_kernel,
        grid=(n // bn,),
        in_specs=[pl.BlockSpec((8, d), lambda j: (0, 0)),
                  pl.BlockSpec((d, bn), lambda j: (0, j)),
                  pl.BlockSpec((1, bn), lambda j: (0, j))],
        out_specs=pl.BlockSpec((8, bn), lambda j: (0, j)),
        out_shape=jax.ShapeDtypeStruct((8, n), F32),
        compiler_params=_cparams(("arbitrary",)),
        name="adaln",
    )(cond8, w, b.reshape(1, n))


_C_MQ, _C_MK, _C_MV = 0, 512, 1024
_C_AQ, _C_AK, _C_AV, _C_MG = 1536, 2048, 2176, 2304
_C_END = 2432


def _rope(t, cos_t, sin_t, width):
    lane = lax.broadcasted_iota(jnp.int32, t.shape, 1)
    lower = (lane % ATTN_DH) < (ATTN_DH // 2)
    swapped = jnp.where(lower, pltpu.roll(t, width - ATTN_DH // 2, 1),
                        pltpu.roll(t, ATTN_DH // 2, 1))
    return t * cos_t + swapped * sin_t


def _proj_kernel(x_ref, sh_ref, sc_ref, g_ref, w_ref, bg_ref, qg_ref, kg_ref,
                 bdq_ref, bdk_ref, cos_ref, sin_ref,
                 mq_ref, mk_ref, mv_ref, mg_ref, aq_ref, ak_ref, av_ref, *, rope):
    x = x_ref[0]
    h = _modulated(x, g_ref[...], sh_ref[0], sc_ref[0]).astype(BF16)

    p = _dot(h, w_ref[:, _C_MQ:_C_AQ])
    mq_ref[0] = p[:, 0:512].astype(BF16)
    mk_ref[0] = (p[:, 512:1024] * (MLSTM_DH ** -0.5)).astype(BF16)
    mv_ref[0] = p[:, 1024:1536].astype(BF16)

    pq = _dot(h, w_ref[:, _C_AQ:_C_AK])
    ssq = _segsum(pq * pq, bdq_ref[...])
    q = pq * lax.rsqrt(ssq * (1.0 / ATTN_DH) + EPS) * qg_ref[...]
    if rope:
        cos_t = cos_ref[...]
        sin_t = sin_ref[...]
        q = _rope(q, jnp.concatenate([cos_t] * 4, axis=1), jnp.concatenate([sin_t] * 4, axis=1),
                  ATTN_Q_W)
    aq_ref[0] = (q * (ATTN_DH ** -0.5)).astype(BF16)

    pk = _dot(h, w_ref[:, _C_AK:_C_AV])
    ssk = _segsum(pk * pk, bdk_ref[...])
    k = pk * lax.rsqrt(ssk * (1.0 / ATTN_DH) + EPS) * kg_ref[...]
    if rope:
        k = _rope(k, cos_ref[...], sin_ref[...], ATTN_KV_W)
    ak_ref[0] = k.astype(BF16)

    av_ref[0] = _dot(h, w_ref[:, _C_AV:_C_MG]).astype(BF16)

    z = _dot(h, w_ref[:, _C_MG:_C_END]) + bg_ref[...]
    lane = lax.broadcasted_iota(jnp.int32, z.shape, 1)
    is_forget = (lane % 8) >= MLSTM_HEADS
    logsig = jnp.minimum(z, 0.0) - jnp.log1p(jnp.exp(-jnp.abs(z)))
    mg_ref[0] = jnp.where(is_forget, logsig, z)


def _proj(x, shift, scale, g, w_pack, bg, qg, kg, bdq, bdk, cos_t, sin_t, *, rope, tm):
    B, T, D = x.shape
    nt = T // tm
    full = lambda shape: pl.BlockSpec(shape, lambda b, t: (0,) * len(shape))
    tok = lambda w: pl.BlockSpec((1, tm, w), lambda b, t: (b, t, 0))
    outs = [(MLSTM_W, BF16), (MLSTM_W, BF16), (MLSTM_W, BF16), (GATE_LANES, F32),
            (ATTN_Q_W, BF16), (ATTN_KV_W, BF16), (ATTN_KV_W, BF16)]
    return pl.pallas_call(
        functools.partial(_proj_kernel, rope=rope),
        grid=(B, nt),
        in_specs=[tok(D),
                  pl.BlockSpec((1, 1, D), lambda b, t: (b, 0, 0)),
                  pl.BlockSpec((1, 1, D), lambda b, t: (b, 0, 0)),
                  full((1, D)), full(w_pack.shape), full((1, GATE_LANES)),
                  full((1, ATTN_Q_W)), full((1, ATTN_KV_W)),
                  full((ATTN_Q_W, ATTN_Q_W)), full((ATTN_KV_W, ATTN_KV_W)),
                  pl.BlockSpec((tm, ATTN_KV_W), lambda b, t: (t, 0)),
                  pl.BlockSpec((tm, ATTN_KV_W), lambda b, t: (t, 0))],
        out_specs=[tok(w) for w, _ in outs],
        out_shape=[jax.ShapeDtypeStruct((B, T, w), dt) for w, dt in outs],
        compiler_params=_cparams(("parallel", "parallel")),
        name="proj_rope" if rope else "proj_ctx",
    )(x, shift, scale, g, w_pack, bg, qg, kg, bdq, bdk, cos_t, sin_t)


def _mlstm_kernel(qf_ref, qb_ref, kcf_ref, kcb_ref, klf_ref, klb_ref, vcf_ref, vcb_ref,
                  vlf_ref, vlb_ref, gcf_ref, gcb_ref, glf_ref, glb_ref,
                  hf_ref, hb_ref, c_ref, n_ref, m_ref, *, n_ctx_chunks):
    s = pl.program_id(1)
    L = MLSTM_CHUNK
    in_ctx = s < n_ctx_chunks

    @pl.when(s == 0)
    def _():
        c_ref[...] = jnp.zeros_like(c_ref)
        n_ref[...] = jnp.zeros_like(n_ref)
        m_ref[...] = jnp.zeros_like(m_ref)

    row = lax.broadcasted_iota(jnp.int32, (L, L), 0)
    col = lax.broadcasted_iota(jnp.int32, (L, L), 1)
    tri_f = (col <= row).astype(F32)
    pick = lambda c_ref_, l_ref_: jnp.where(in_ctx, c_ref_[0], l_ref_[0])

    dirs = []
    for d in range(2):
        q_all = (qf_ref, qb_ref)[d][0]
        k_all = pick((kcf_ref, kcb_ref)[d], (klf_ref, klb_ref)[d])
        v_all = pick((vcf_ref, vcb_ref)[d], (vlf_ref, vlb_ref)[d])
        g = pick((gcf_ref, gcb_ref)[d], (glf_ref, glb_ref)[d])
        cum_f = jnp.dot(tri_f, g, preferred_element_type=F32, precision=lax.Precision.HIGHEST)
        if d == 0:
            b_all, seen, last = cum_f, col <= row, L - 1
        else:
            b_all, seen, last = cum_f[L - 1:L, :] - cum_f + g, col >= row, 0
        dirs.append((q_all, k_all, v_all, g, b_all, g.T, b_all.T, seen, last))

    heads = [(d, hh) for d in range(2) for hh in range(MLSTM_HEADS)]
    hsl = lambda hh: slice(hh * MLSTM_DH, (hh + 1) * MLSTM_DH)

    qk, qc, state = {}, {}, {}
    for d, hh in heads:
        q_all, k_all = dirs[d][0], dirs[d][1]
        r = d * MLSTM_HEADS + hh
        c_s = c_ref[r]
        state[d, hh] = (c_s, n_ref[r], m_ref[r][:, 0:1])
        qk[d, hh] = _dot_nt(q_all[:, hsl(hh)], k_all[:, hsl(hh)])
        qc[d, hh] = _dot(q_all[:, hsl(hh)], c_s.astype(BF16))


    summ = {}
    for d, hh in heads:
        _, k_all, _, g, b_all, _, _, _, last = dirs[d]
        ci = d * 2 * MLSTM_HEADS + hh
        cf = ci + MLSTM_HEADS
        a = b_all[last:last + 1, cf:cf + 1]
        w_log = a - b_all[:, cf:cf + 1] + g[:, ci:ci + 1]
        m_loc = jnp.max(w_log, axis=0, keepdims=True)
        kw = k_all[:, hsl(hh)].astype(F32) * jnp.exp(w_log - m_loc)
        summ[d, hh] = (a, m_loc, kw)
    for d, hh in heads:
        v_all = dirs[d][2]
        r = d * MLSTM_HEADS + hh
        c_s, n_s, m_s = state[d, hh]
        a, m_loc, kw = summ[d, hh]
        c_loc = _dot_tn(kw.astype(BF16), v_all[:, hsl(hh)])
        n_loc = jnp.sum(kw, axis=0, keepdims=True)
        m_new = jnp.maximum(a + m_s, m_loc)
        d_old = jnp.exp(a + m_s - m_new)
        d_new = jnp.exp(m_loc - m_new)
        c_ref[r] = d_old * c_s + d_new * c_loc
        n_ref[r] = d_old * n_s + d_new * n_loc
        m_ref[r] = jnp.broadcast_to(m_new, (1, MLSTM_DH))

    rel, m_rel, sc, inter, den = {}, {}, {}, {}, {}
    for d, hh in heads:
        _, _, _, _, _, g_t, b_t, seen, _ = dirs[d]
        ci = d * 2 * MLSTM_HEADS + hh
        cf = ci + MLSTM_HEADS
        rel[d, hh] = jnp.where(seen, g_t[ci:ci + 1, :] - b_t[cf:cf + 1, :], NEG_INF)
        m_rel[d, hh] = jnp.maximum(state[d, hh][2], jnp.max(rel[d, hh], axis=1, keepdims=True))
    for d, hh in heads:
        m_s = state[d, hh][2]
        sc[d, hh] = qk[d, hh] * jnp.exp(rel[d, hh] - m_rel[d, hh])
        inter[d, hh] = jnp.broadcast_to(jnp.exp(m_s - m_rel[d, hh]), (L, MLSTM_DH))
    for d, hh in heads:
        qn = dirs[d][0][:, hsl(hh)].astype(F32) * state[d, hh][1]
        den[d, hh] = jnp.sum(sc[d, hh] + inter[d, hh] * qn, axis=1, keepdims=True)
    for d, hh in heads:
        _, _, v_all, _, b_all, _, _, _, _ = dirs[d]
        out_ref = (hf_ref, hb_ref)[d]
        cf = d * 2 * MLSTM_HEADS + hh + MLSTM_HEADS
        num = _dot(sc[d, hh].astype(BF16), v_all[:, hsl(hh)]) + inter[d, hh] * qc[d, hh]
        floor = jnp.exp(-(b_all[:, cf:cf + 1] + m_rel[d, hh]))
        out_ref[0, :, hsl(hh)] = num * (1.0 / jnp.maximum(jnp.abs(den[d, hh]), floor))


def _mlstm(q_lat, k_ctx, v_ctx, g_ctx, k_lat, v_lat, g_lat):
    B, T, _ = q_lat.shape
    L = MLSTM_CHUNK
    nl = T // L
    nc = k_ctx.shape[1] // L
    steps = nc + nl

    fwd_c = lambda b, s: (b, jnp.minimum(s, nc - 1), 0)
    bwd_c = lambda b, s: (b, jnp.maximum(nc - 1 - s, 0), 0)
    fwd_l = lambda b, s: (b, jnp.maximum(s - nc, 0), 0)
    bwd_l = lambda b, s: (b, jnp.minimum(steps - 1 - s, nl - 1), 0)

    blk = lambda w, im: pl.BlockSpec((1, L, w), im)
    pair = lambda w, f, g: [blk(w, f), blk(w, g)]
    nheads = 2 * MLSTM_HEADS
    return pl.pallas_call(
        functools.partial(_mlstm_kernel, n_ctx_chunks=nc),
        grid=(B, steps),
        in_specs=(pair(MLSTM_W, fwd_l, bwd_l)
                  + pair(MLSTM_W, fwd_c, bwd_c) + pair(MLSTM_W, fwd_l, bwd_l)
                  + pair(MLSTM_W, fwd_c, bwd_c) + pair(MLSTM_W, fwd_l, bwd_l)
                  + pair(GATE_LANES, fwd_c, bwd_c) + pair(GATE_LANES, fwd_l, bwd_l)),
        out_specs=pair(MLSTM_W, fwd_l, bwd_l),
        out_shape=[jax.ShapeDtypeStruct((B, T, MLSTM_W), F32)] * 2,
        scratch_shapes=[pltpu.VMEM((nheads, MLSTM_DH, MLSTM_DH), F32),
                        pltpu.VMEM((nheads, 1, MLSTM_DH), F32),
                        pltpu.VMEM((nheads, 1, MLSTM_DH), F32)],
        compiler_params=_cparams(("parallel", "arbitrary")),
        name="mlstm",
    )(q_lat, q_lat, k_ctx, k_ctx, k_lat, k_lat, v_ctx, v_ctx, v_lat, v_lat,
      g_ctx, g_ctx, g_lat, g_lat)


def _attn_kernel(q_ref, kp_ref, kc_ref, kn_ref, vp_ref, vc_ref, vn_ref, kx_ref, vx_ref,
                 sink_ref, o_ref, *, nb):
    n = pl.program_id(1)
    blk = ATTN_BLOCK
    kcat = jnp.concatenate([kp_ref[0], kc_ref[0], kn_ref[0], kx_ref[0]], axis=0)
    vcat = jnp.concatenate([vp_ref[0], vc_ref[0], vn_ref[0], vx_ref[0]], axis=0)

    qi = lax.broadcasted_iota(jnp.int32, (blk, blk), 0)
    kj = lax.broadcasted_iota(jnp.int32, (blk, blk), 1)
    ok_prev = (kj >= qi) & (n > 0)
    ok_next = (kj <= qi) & (n < nb - 1)

    lane = lax.broadcasted_iota(jnp.int32, (blk, 2 * ATTN_DH), 1)
    low = lane < ATTN_DH
    n_pairs = ATTN_Q_W // (2 * ATTN_DH)
    pieces = []
    for c in range(n_pairs):
        qc = q_ref[0, :, c * 128:(c + 1) * 128]
        pieces += [jnp.where(low, qc, jnp.zeros_like(qc)), jnp.where(low, jnp.zeros_like(qc), qc)]
    s_all = _dot_nt(jnp.concatenate(pieces, axis=0), kcat)

    def fold(op, cols):
        groups = [c[:, i:i + blk] for c in cols for i in range(0, c.shape[1], blk)]
        return functools.reduce(op, groups)

    parts, maxes, probs, dens = [], [], [], []
    for j in range(2 * n_pairs):
        sc = s_all[j * blk:(j + 1) * blk]
        parts.append([jnp.where(ok_prev, sc[:, 0:blk], NEG_INF), sc[:, blk:2 * blk],
                      jnp.where(ok_next, sc[:, 2 * blk:3 * blk], NEG_INF), sc[:, 3 * blk:]])
        row_max = jnp.max(fold(jnp.maximum, parts[j]), axis=1, keepdims=True)
        maxes.append(jnp.maximum(row_max, sink_ref[j:j + 1, 0:1]))
    for j in range(2 * n_pairs):
        exps = [jnp.exp(part - maxes[j]) for part in parts[j]]
        row_sum = jnp.sum(fold(jnp.add, exps), axis=1, keepdims=True)
        dens.append(row_sum + jnp.exp(sink_ref[j:j + 1, 0:1] - maxes[j]))
        probs.append(jnp.concatenate([p.astype(BF16) for p in exps], axis=1))
    o_all = _dot(jnp.concatenate(probs, axis=0), vcat)
    for c in range(n_pairs):
        o_lo = o_all[(2 * c) * blk:(2 * c + 1) * blk] / dens[2 * c]
        o_hi = o_all[(2 * c + 1) * blk:(2 * c + 2) * blk] / dens[2 * c + 1]
        o_ref[0, :, c * 128:(c + 1) * 128] = jnp.where(low, o_lo, o_hi).astype(BF16)


def _attn(aq, ak, av, akx, avx, sink8):
    B, T, _ = aq.shape
    blk = ATTN_BLOCK
    nb = T // blk
    ctx = akx.shape[1]
    kv = lambda im: pl.BlockSpec((1, blk, ATTN_KV_W), im)
    prev = lambda b, n: (b, jnp.maximum(n - 1, 0), 0)
    cur = lambda b, n: (b, n, 0)
    nxt = lambda b, n: (b, jnp.minimum(n + 1, nb - 1), 0)
    cx = pl.BlockSpec((1, ctx, ATTN_KV_W), lambda b, n: (b, 0, 0))
    return pl.pallas_call(
        functools.partial(_attn_kernel, nb=nb),
        grid=(B, nb),
        in_specs=[pl.BlockSpec((1, blk, ATTN_Q_W), cur),
                  kv(prev), kv(cur), kv(nxt), kv(prev), kv(cur), kv(nxt), cx, cx,
                  pl.BlockSpec((8, 128), lambda b, n: (0, 0))],
        out_specs=pl.BlockSpec((1, blk, ATTN_Q_W), cur),
        out_shape=jax.ShapeDtypeStruct((B, T, ATTN_Q_W), BF16),
        compiler_params=_cparams(("parallel", "parallel")),
        name="attn",
    )(aq, ak, ak, ak, av, av, av, akx, avx, sink8)


def _merge_kernel(x_ref, sh_ref, sc_ref, g_ref, g1_ref, hf_ref, hb_ref, a_ref,
                  wg_ref, ng_ref, bdm_ref, wbm_ref, wba_ref, wo_ref, o_ref):
    x = x_ref[0]
    h = _modulated(x, g_ref[...], sh_ref[0], sc_ref[0]).astype(BF16)
    pg = _dot(h, wg_ref[...])
    d = x.shape[1]
    hs = hf_ref[0] + hb_ref[0]
    ss = _segsum(hs * hs, bdm_ref[...])
    m = hs * lax.rsqrt(ss * (1.0 / MLSTM_DH) + EPS) * ng_ref[...] * _sigmoid(pg[:, 0:MLSTM_W])
    mm = _dot(m.astype(BF16), wbm_ref[...])
    aa = _dot(a_ref[0], wba_ref[...])
    y = _sigmoid(pg[:, MLSTM_W:MLSTM_W + d]) * mm + _sigmoid(pg[:, MLSTM_W + d:]) * aa
    o_ref[0] = x + g1_ref[0] * _dot(y.astype(BF16), wo_ref[...])


def _merge(x, shift, scale, g, g1, hf, hb, a, wg, ng, bdm, wbm, wba, wo, *, tm):
    B, T, D = x.shape
    full = lambda arr: pl.BlockSpec(arr.shape, lambda b, t: (0,) * arr.ndim)
    tok = lambda w: pl.BlockSpec((1, tm, w), lambda b, t: (b, t, 0))
    perb = pl.BlockSpec((1, 1, D), lambda b, t: (b, 0, 0))
    return pl.pallas_call(
        _merge_kernel,
        grid=(B, T // tm),
        in_specs=[tok(D), perb, perb, full(g), perb, tok(MLSTM_W), tok(MLSTM_W), tok(ATTN_Q_W),
                  full(wg), full(ng), full(bdm), full(wbm), full(wba), full(wo)],
        out_specs=tok(D),
        out_shape=jax.ShapeDtypeStruct((B, T, D), F32),
        compiler_params=_cparams(("parallel", "parallel")),
        name="merge",
    )(x, shift, scale, g, g1, hf, hb, a, wg, ng, bdm, wbm, wba, wo)


def _pquery_kernel(x_ref, sh_ref, sc_ref, g_ref, wq_ref, sk_ref, h_ref, s_ref):
    x = x_ref[0]
    h = _modulated(x, g_ref[...], sh_ref[0], sc_ref[0]).astype(BF16)
    h_ref[0] = h
    q = _dot(h, wq_ref[...]).astype(BF16)
    for j in range(2 * PEER_HEADS):
        s_ref[0, j] = _dot_nt(sk_ref[j], q[:, j * PEER_HALF:(j + 1) * PEER_HALF])


def _pquery(x, shift, scale, g, wq, sk, *, tm):
    B, T, D = x.shape
    nj = 2 * PEER_HEADS
    full = lambda arr: pl.BlockSpec(arr.shape, lambda b, t: (0,) * arr.ndim)
    perb = pl.BlockSpec((1, 1, D), lambda b, t: (b, 0, 0))
    return pl.pallas_call(
        _pquery_kernel,
        grid=(B, T // tm),
        in_specs=[pl.BlockSpec((1, tm, D), lambda b, t: (b, t, 0)), perb, perb,
                  full(g), full(wq), full(sk)],
        out_specs=[pl.BlockSpec((1, tm, D), lambda b, t: (b, t, 0)),
                   pl.BlockSpec((1, nj, PEER_N_KEYS, tm), lambda b, t: (b, 0, 0, t))],
        out_shape=[jax.ShapeDtypeStruct((B, T, D), BF16),
                   jax.ShapeDtypeStruct((B, nj, PEER_N_KEYS, T), F32)],
        compiler_params=_cparams(("parallel", "parallel")),
        name="pquery",
    )(x, shift, scale, g, wq, sk)


def _top_values(val, k, with_rank=False):
    tops = []
    rank = jnp.full(val.shape, float(k), F32) if with_rank else None
    for p in range(k):
        m = jnp.max(val, axis=0, keepdims=True)
        tops.append(m)
        hit = val == m
        if with_rank:
            rank = jnp.where(hit, float(p), rank)
        val = jnp.where(hit, NEG_INF, val)
    return (tops, rank) if with_rank else tops


_GELU_K0 = -2.0 * math.sqrt(2.0 / math.pi) * math.log2(math.e)
_GELU_K1 = 0.044715 * _GELU_K0


def _gelu_tanh(x, k0, k1):
    one = jnp.asarray(1.0, x.dtype)
    return x / (one + jnp.exp2(x * (x * x * k1 + k0)))


BF16_ROWS = 16


def _rows_bf16(row, tm):
    tile = jnp.broadcast_to(row, (BF16_ROWS, tm)).astype(BF16)
    return jnp.concatenate([tile] * (PEER_N_KEYS // BF16_ROWS), axis=0)


def _row_tile_bf16(ref, h, i, tm):
    return _rows_bf16(ref[h, pl.ds(i, 1), :], tm)


def _peer_router(s_ref, rank_ref, cw_ref, cnt_ref, e2_ref):
    k = PEER_TOPK
    for h in range(PEER_HEADS):
        s1 = s_ref[0, 2 * h]
        s2 = s_ref[0, 2 * h + 1]
        a, rank = _top_values(s1, k, with_rank=True)
        b = _top_values(s2, k)
        cand = [a[p] + b[q] for p in range(k) for q in range(k) if (p + 1) * (q + 1) <= k]
        cv = jnp.concatenate(cand, axis=0)
        thr = _top_values(cv, k)[k - 1]
        top = a[0] + b[0]
        z = jnp.sum(jnp.where(cv >= thr, jnp.exp(cv - top), 0.0), axis=0, keepdims=True)
        cnt = jnp.zeros(s2.shape, F32)
        for p in range(k):
            cnt = jnp.where(a[p] + s2 >= thr, float(p + 1), cnt)
        rank_ref[h] = rank
        cw_ref[h] = jnp.exp(s1 - a[0]) / z
        cnt_ref[h] = cnt.astype(BF16)
        e2_ref[h] = jnp.exp(s2 - b[0]).astype(BF16)


def _peer_kernel(h_ref, s_ref, u_ref, vt_ref, x_ref, g2_ref, gk_ref, o_ref,
                 acc_ref, rank_ref, cw_ref, cnt_ref, e2_ref, *, eb, sb):
    e = pl.program_id(2)
    ne = pl.num_programs(2)
    tm = h_ref.shape[1]

    @pl.when(e == 0)
    def _():
        acc_ref[...] = jnp.zeros_like(acc_ref)
        _peer_router(s_ref, rank_ref, cw_ref, cnt_ref, e2_ref)

    hq = h_ref[0]
    n_i = sb // PEER_N_KEYS
    n_sub = eb // sb
    scores = lambda j: _dot_nt(u_ref[j * sb:(j + 1) * sb, :], hq)
    zero = jnp.zeros((), BF16)
    gk0 = _rows_bf16(gk_ref[0:1, :], tm)
    gk1 = _rows_bf16(gk_ref[1:2, :], tm)
    act_next = scores(0)
    for j in range(n_sub):
        act = act_next
        if j + 1 < n_sub:
            act_next = scores(j + 1)
        w_rows = []
        for ii in range(n_i):
            i = e * (eb // PEER_N_KEYS) + j * n_i + ii
            w = None
            for h in range(PEER_HEADS):
                rank_i = _row_tile_bf16(rank_ref, h, i, tm)
                cw_i = _row_tile_bf16(cw_ref, h, i, tm)
                term = jnp.where(rank_i < cnt_ref[h], e2_ref[h], zero) * cw_i
                w = term if w is None else w + term
            act_i = act[ii * PEER_N_KEYS:(ii + 1) * PEER_N_KEYS].astype(BF16)
            w_rows.append(w * _gelu_tanh(act_i, gk0, gk1))
        wt = jnp.concatenate(w_rows, axis=0)
        acc_ref[...] += _dot(vt_ref[:, j * sb:(j + 1) * sb], wt)

    @pl.when(e == ne - 1)
    def _():
        o_ref[0] = x_ref[0] + g2_ref[0] * acc_ref[...].T


def _peer(h2, s_t, u, vt, x1, g2, *, tm, eb, sb):
    B, T, D = x1.shape
    ne = u.shape[0] // eb
    nj = 2 * PEER_HEADS
    gelu_k = jnp.broadcast_to(jnp.array([[_GELU_K0], [_GELU_K1]], F32), (2, tm))
    return pl.pallas_call(
        functools.partial(_peer_kernel, eb=eb, sb=sb),
        grid=(B, T // tm, ne),
        in_specs=[pl.BlockSpec((1, tm, D), lambda b, t, e: (b, t, 0)),
                  pl.BlockSpec((1, nj, PEER_N_KEYS, tm), lambda b, t, e: (b, 0, 0, t)),
                  pl.BlockSpec((eb, D), lambda b, t, e: (e, 0)),
                  pl.BlockSpec((D, eb), lambda b, t, e: (0, e)),
                  pl.BlockSpec((1, tm, D), lambda b, t, e: (b, t, 0)),
                  pl.BlockSpec((1, 1, D), lambda b, t, e: (b, 0, 0)),
                  pl.BlockSpec((2, tm), lambda b, t, e: (0, 0))],
        out_specs=pl.BlockSpec((1, tm, D), lambda b, t, e: (b, t, 0)),
        out_shape=jax.ShapeDtypeStruct((B, T, D), F32),
        scratch_shapes=[pltpu.VMEM((D, tm), F32),
                        pltpu.VMEM((PEER_HEADS, PEER_N_KEYS, tm), F32),
                        pltpu.VMEM((PEER_HEADS, PEER_N_KEYS, tm), F32),
                        pltpu.VMEM((PEER_HEADS, PEER_N_KEYS, tm), BF16),
                        pltpu.VMEM((PEER_HEADS, PEER_N_KEYS, tm), BF16)],
        compiler_params=_cparams(("parallel", "parallel", "arbitrary")),
        name="peer",
    )(h2, s_t, u, vt, x1, g2, gelu_k)


_Q_HEAD_ORDER = (0, 4, 1, 5, 2, 6, 3, 7)


def _block_diag_ones(width, seg):
    i = jnp.arange(width)
    return (i[:, None] // seg == i[None, :] // seg).astype(BF16)


def _rope_tables(T):
    rows = T // GRID_W
    nf = ATTN_DH // 4
    row = jnp.broadcast_to(jnp.arange(rows)[:, None], (rows, GRID_W)).reshape(T)
    col = jnp.broadcast_to(jnp.arange(GRID_W)[None, :], (rows, GRID_W)).reshape(T)
    inv = ROPE_BASE ** (-jnp.arange(nf, dtype=F32) / nf)
    ang = jnp.concatenate([row[:, None].astype(F32) * inv, col[:, None].astype(F32) * inv], axis=-1)
    cos, sin = jnp.cos(ang), jnp.sin(ang)
    cos_t = jnp.concatenate([cos, cos] * ATTN_KV_HEADS, axis=-1)
    sin_t = jnp.concatenate([-sin, sin] * ATTN_KV_HEADS, axis=-1)
    return cos_t, sin_t


def _layer(x, ctx, mod, norm_mix_g, norm_ffn_g, w_in, b_mgates, mlstm_norm_g, attn_q_norm_g,
           attn_k_norm_g, attn_sink, w_branch_m, w_branch_a, w_out, peer_w_query, peer_sub_keys,
           peer_u, peer_v, cos_t, sin_t):
    B, T, D = x.shape
    n_ctx = ctx.shape[1]
    mods = jnp.split(mod, 6, axis=-1)
    lat = [m[0:B].reshape(B, 1, D) for m in mods]
    cxm = [jnp.broadcast_to(m[B:B + 1].reshape(1, 1, D), (B, 1, D)) for m in mods]
    sh1, sc1, g1, sh2, sc2, g2 = lat

    W = MLSTM_W
    o0 = 4 * W
    o1 = o0 + 2 * 2 * MLSTM_HEADS
    w_mq, w_mk, w_mv, w_mo = (w_in[:, i * W:(i + 1) * W] for i in range(4))
    w_g16 = w_in[:, o0:o1]
    w_aq = w_in[:, o1:o1 + ATTN_Q_W].reshape(D, ATTN_Q_HEADS, ATTN_DH)
    w_aq = w_aq[:, jnp.array(_Q_HEAD_ORDER), :].reshape(D, ATTN_Q_W)
    o2 = o1 + ATTN_Q_W
    w_ak = w_in[:, o2:o2 + ATTN_KV_W]
    w_av = w_in[:, o2 + ATTN_KV_W:o2 + 2 * ATTN_KV_W]
    w_gates = w_in[:, o2 + 2 * ATTN_KV_W:]
    w_g128 = jnp.pad(w_g16, ((0, 0), (0, GATE_LANES - w_g16.shape[1])))
    w_pack = jnp.concatenate([w_mq, w_mk, w_mv, w_aq, w_ak, w_av, w_g128], axis=1).astype(BF16)
    w_merge_in = jnp.concatenate([w_mo, w_gates], axis=1).astype(BF16)
    bg = jnp.pad(b_mgates, (0, GATE_LANES - b_mgates.shape[0])).reshape(1, GATE_LANES)
    qg = jnp.tile(attn_q_norm_g, ATTN_Q_HEADS).reshape(1, ATTN_Q_W)
    kg = jnp.tile(attn_k_norm_g, ATTN_KV_HEADS).reshape(1, ATTN_KV_W)
    bdq = _block_diag_ones(ATTN_Q_W, ATTN_DH)
    bdk = _block_diag_ones(ATTN_KV_W, ATTN_DH)
    bdm = _block_diag_ones(MLSTM_W, MLSTM_DH)
    gmix = norm_mix_g.reshape(1, D)

    tm_lat = min(512, T)
    mq, mk, mv, mg, aq, ak, av = _proj(x, sh1, sc1, gmix, w_pack, bg, qg, kg, bdq, bdk,
                                       cos_t, sin_t, rope=True, tm=tm_lat)
    _, mkc, mvc, mgc, _, akc, avc = _proj(ctx, cxm[0], cxm[1], gmix, w_pack, bg, qg, kg, bdq, bdk,
                                          cos_t, sin_t, rope=False, tm=n_ctx)

    hf, hb = _mlstm(mq, mkc, mvc, mgc, mk, mv, mg)

    sink8 = jnp.broadcast_to(attn_sink[jnp.array(_Q_HEAD_ORDER)].astype(F32)[:, None], (8, 128))
    a = _attn(aq, ak, av, akc, avc, sink8)

    wba = w_branch_a.reshape(ATTN_Q_HEADS, ATTN_DH, D)[jnp.array(_Q_HEAD_ORDER)]
    wba = wba.reshape(ATTN_Q_W, D).astype(BF16)
    x1 = _merge(x, sh1, sc1, gmix, g1, hf, hb, a, w_merge_in, mlstm_norm_g.reshape(1, MLSTM_W),
                bdm, w_branch_m.astype(BF16), wba, w_out.astype(BF16), tm=min(256, T))

    sk = peer_sub_keys.reshape(2 * PEER_HEADS, PEER_N_KEYS, PEER_HALF).astype(BF16)
    h2, s_t = _pquery(x1, sh2, sc2, norm_ffn_g.reshape(1, D), peer_w_query.astype(BF16), sk,
                      tm=min(512, T))
    u = peer_u.astype(BF16)
    vt = peer_v.T.astype(BF16)
    return _peer(h2, s_t, u, vt, x1, g2, tm=min(512, T), eb=2048, sb=512)


def kernel(x, c, ctx, c_ctx, w_ada, b_ada, norm_mix_g, norm_ffn_g, w_in, b_mgates, mlstm_norm_g,
           attn_q_norm_g, attn_k_norm_g, attn_sink, w_branch_m, w_branch_a, w_out, peer_w_query,
           peer_sub_keys, peer_u, peer_v):
    B, T, D = x.shape
    depth = w_ada.shape[0]
    assert depth == 1, "context-stream update for deeper stacks is not implemented"
    cos_t, sin_t = _rope_tables(T)
    cond8 = jnp.zeros((8, D), F32).at[0:B].set(c).at[B].set(c_ctx)
    for layer in range(depth):
        mod = _adaln(cond8, w_ada[layer], b_ada[layer])
        x = _layer(x, ctx, mod, norm_mix_g[layer], norm_ffn_g[layer], w_in[layer], b_mgates[layer],
                   mlstm_norm_g[layer], attn_q_norm_g[layer], attn_k_norm_g[layer], attn_sink[layer],
                   w_branch_m[layer], w_branch_a[layer], w_out[layer], peer_w_query[layer],
                   peer_sub_keys[layer], peer_u[layer], peer_v[layer], cos_t, sin_t)
    return x
```

```python
import functools
import math

import jax
import jax.numpy as jnp
from jax import lax
from jax.experimental import pallas as pl
from jax.experimental.pallas import tpu as pltpu

F32 = jnp.float32
BF16 = jnp.bfloat16

EPS = 1e-6
GRID_W = 64
ROPE_BASE = 10000.0

MLSTM_HEADS = 4
MLSTM_DH = 128
MLSTM_W = MLSTM_HEADS * MLSTM_DH
MLSTM_CHUNK = 128
GATE_LANES = 128

ATTN_Q_HEADS = 8
ATTN_KV_HEADS = 2
ATTN_DH = 64
ATTN_Q_W = ATTN_Q_HEADS * ATTN_DH
ATTN_KV_W = ATTN_KV_HEADS * ATTN_DH
ATTN_BLOCK = 128

PEER_HEADS = 8
PEER_N_KEYS = 128
PEER_HALF = 128
PEER_TOPK = 16
PEER_SUB_BLOCK = 512

VMEM_LIMIT_BYTES = 56 * 1024 * 1024

NEG_INF = float("-inf")
POS_INF = float("inf")


def _cparams(sem):
    return pltpu.CompilerParams(dimension_semantics=sem, vmem_limit_bytes=VMEM_LIMIT_BYTES)


def _dot(a, b):
    return jnp.dot(a, b, preferred_element_type=F32)


def _dot_nt(a, b):
    return lax.dot_general(a, b, (((1,), (1,)), ((), ())), preferred_element_type=F32)


def _dot_tn(a, b):
    return lax.dot_general(a, b, (((0,), (0,)), ((), ())), preferred_element_type=F32)


def _sigmoid(x):
    return 1.0 / (1.0 + jnp.exp(-x))


def _segsum(x2, bd):
    hi = x2.astype(BF16)
    lo = (x2 - hi.astype(F32)).astype(BF16)
    return _dot(hi, bd) + _dot(lo, bd)


def _modulated(x, g, shift, scale):
    ms = jnp.mean(x * x, axis=-1, keepdims=True)
    return (x * lax.rsqrt(ms + EPS)) * g * (1.0 + scale) + shift


def _adaln_kernel(c_ref, w_ref, b_ref, o_ref):
    c = c_ref[...]
    s = c * _sigmoid(c)
    o_ref[...] = jnp.dot(s, w_ref[...], preferred_element_type=F32,
                         precision=lax.Precision.HIGHEST) + b_ref[...]


def _adaln(cond8, w, b):
    d = cond8.shape[1]
    n = w.shape[1]
    bn = n // 6
    return pl.pallas_call(
        _adaln_kernel,
        grid=(n // bn,),
        in_specs=[pl.BlockSpec((8, d), lambda j: (0, 0)),
                  pl.BlockSpec((d, bn), lambda j: (0, j)),
                  pl.BlockSpec((1, bn), lambda j: (0, j))],
        out_specs=pl.BlockSpec((8, bn), lambda j: (0, j)),
        out_shape=jax.ShapeDtypeStruct((8, n), F32),
        compiler_params=_cparams(("arbitrary",)),
        name="adaln",
    )(cond8, w, b.reshape(1, n))


_C_MQ, _C_MK, _C_MV = 0, 512, 1024
_C_AQ, _C_AK, _C_AV, _C_MG = 1536, 2048, 2176, 2304
_C_END = 2432


def _rope(t, cos_t, sin_t, width):
    lane = lax.broadcasted_iota(jnp.int32, t.shape, 1)
    lower = (lane % ATTN_DH) < (ATTN_DH // 2)
    swapped = jnp.where(lower, pltpu.roll(t, width - ATTN_DH // 2, 1),
                        pltpu.roll(t, ATTN_DH // 2, 1))
    return t * cos_t + swapped * sin_t


def _proj_kernel(x_ref, sh_ref, sc_ref, g_ref, w_ref, bg_ref, qg_ref, kg_ref,
                 bdq_ref, bdk_ref, cos_ref, sin_ref,
                 mq_ref, mk_ref, mv_ref, mg_ref, aq_ref, ak_ref, av_ref, *, rope):
    x = x_ref[0]
    h = _modulated(x, g_ref[...], sh_ref[0], sc_ref[0]).astype(BF16)

    p = _dot(h, w_ref[:, _C_MQ:_C_AQ])
    mq_ref[0] = p[:, 0:512].astype(BF16)
    mk_ref[0] = (p[:, 512:1024] * (MLSTM_DH ** -0.5)).astype(BF16)
    mv_ref[0] = p[:, 1024:1536].astype(BF16)

    pq = _dot(h, w_ref[:, _C_AQ:_C_AK])
    ssq = _segsum(pq * pq, bdq_ref[...])
    q = pq * lax.rsqrt(ssq * (1.0 / ATTN_DH) + EPS) * qg_ref[...]
    if rope:
        cos_t = cos_ref[...]
        sin_t = sin_ref[...]
        q = _rope(q, jnp.concatenate([cos_t] * 4, axis=1), jnp.concatenate([sin_t] * 4, axis=1),
                  ATTN_Q_W)
    aq_ref[0] = (q * (ATTN_DH ** -0.5)).astype(BF16)

    pk = _dot(h, w_ref[:, _C_AK:_C_AV])
    ssk = _segsum(pk * pk, bdk_ref[...])
    k = pk * lax.rsqrt(ssk * (1.0 / ATTN_DH) + EPS) * kg_ref[...]
    if rope:
        k = _rope(k, cos_ref[...], sin_ref[...], ATTN_KV_W)
    ak_ref[0] = k.astype(BF16)

    av_ref[0] = _dot(h, w_ref[:, _C_AV:_C_MG]).astype(BF16)

    z = _dot(h, w_ref[:, _C_MG:_C_END]) + bg_ref[...]
    lane = lax.broadcasted_iota(jnp.int32, z.shape, 1)
    is_forget = (lane % 8) >= MLSTM_HEADS
    logsig = jnp.minimum(z, 0.0) - jnp.log1p(jnp.exp(-jnp.abs(z)))
    mg_ref[0] = jnp.where(is_forget, logsig, z)


def _proj(x, shift, scale, g, w_pack, bg, qg, kg, bdq, bdk, cos_t, sin_t, *, rope, tm):
    B, T, D = x.shape
    nt = T // tm
    full = lambda shape: pl.BlockSpec(shape, lambda b, t: (0,) * len(shape))
    tok = lambda w: pl.BlockSpec((1, tm, w), lambda b, t: (b, t, 0))
    outs = [(MLSTM_W, BF16), (MLSTM_W, BF16), (MLSTM_W, BF16), (GATE_LANES, F32),
            (ATTN_Q_W, BF16), (ATTN_KV_W, BF16), (ATTN_KV_W, BF16)]
    return pl.pallas_call(
        functools.partial(_proj_kernel, rope=rope),
        grid=(B, nt),
        in_specs=[tok(D),
                  pl.BlockSpec((1, 1, D), lambda b, t: (b, 0, 0)),
                  pl.BlockSpec((1, 1, D), lambda b, t: (b, 0, 0)),
                  full((1, D)), full(w_pack.shape), full((1, GATE_LANES)),
                  full((1, ATTN_Q_W)), full((1, ATTN_KV_W)),
                  full((ATTN_Q_W, ATTN_Q_W)), full((ATTN_KV_W, ATTN_KV_W)),
                  pl.BlockSpec((tm, ATTN_KV_W), lambda b, t: (t, 0)),
                  pl.BlockSpec((tm, ATTN_KV_W), lambda b, t: (t, 0))],
        out_specs=[tok(w) for w, _ in outs],
        out_shape=[jax.ShapeDtypeStruct((B, T, w), dt) for w, dt in outs],
        compiler_params=_cparams(("parallel", "parallel")),
        name="proj_rope" if rope else "proj_ctx",
    )(x, shift, scale, g, w_pack, bg, qg, kg, bdq, bdk, cos_t, sin_t)


def _mlstm_kernel(qf_ref, qb_ref, kcf_ref, kcb_ref, klf_ref, klb_ref, vcf_ref, vcb_ref,
                  vlf_ref, vlb_ref, gcf_ref, gcb_ref, glf_ref, glb_ref,
                  hf_ref, hb_ref, c_ref, n_ref, m_ref, *, n_ctx_chunks):
    s = pl.program_id(1)
    L = MLSTM_CHUNK
    in_ctx = s < n_ctx_chunks

    @pl.when(s == 0)
    def _():
        c_ref[...] = jnp.zeros_like(c_ref)
        n_ref[...] = jnp.zeros_like(n_ref)
        m_ref[...] = jnp.zeros_like(m_ref)

    row = lax.broadcasted_iota(jnp.int32, (L, L), 0)
    col = lax.broadcasted_iota(jnp.int32, (L, L), 1)
    tri_f = (col <= row).astype(F32)
    pick = lambda c_ref_, l_ref_: jnp.where(in_ctx, c_ref_[0], l_ref_[0])

    dirs = []
    for d in range(2):
        q_all = (qf_ref, qb_ref)[d][0]
        k_all = pick((kcf_ref, kcb_ref)[d], (klf_ref, klb_ref)[d])
        v_all = pick((vcf_ref, vcb_ref)[d], (vlf_ref, vlb_ref)[d])
        g = pick((gcf_ref, gcb_ref)[d], (glf_ref, glb_ref)[d])
        cum_f = jnp.dot(tri_f, g, preferred_element_type=F32, precision=lax.Precision.HIGHEST)
        if d == 0:
            b_all, seen, last = cum_f, col <= row, L - 1
        else:
            b_all, seen, last = cum_f[L - 1:L, :] - cum_f + g, col >= row, 0
        dirs.append((q_all, k_all, v_all, g, b_all, g.T, b_all.T, seen, last))

    heads = [(d, hh) for d in range(2) for hh in range(MLSTM_HEADS)]
    hsl = lambda hh: slice(hh * MLSTM_DH, (hh + 1) * MLSTM_DH)

    qk, qc, state = {}, {}, {}
    for d, hh in heads:
        q_all, k_all = dirs[d][0], dirs[d][1]
        r = d * MLSTM_HEADS + hh
        c_s = c_ref[r]
        state[d, hh] = (c_s, n_ref[r], m_ref[r][:, 0:1])
        qk[d, hh] = _dot_nt(q_all[:, hsl(hh)], k_all[:, hsl(hh)])
        qc[d, hh] = _dot(q_all[:, hsl(hh)], c_s.astype(BF16))


    summ = {}
    for d, hh in heads:
        _, k_all, _, g, b_all, _, _, _, last = dirs[d]
        ci = d * 2 * MLSTM_HEADS + hh
        cf = ci + MLSTM_HEADS
        a = b_all[last:last + 1, cf:cf + 1]
        w_log = a - b_all[:, cf:cf + 1] + g[:, ci:ci + 1]
        m_loc = jnp.max(w_log, axis=0, keepdims=True)
        kw = k_all[:, hsl(hh)].astype(F32) * jnp.exp(w_log - m_loc)
        summ[d, hh] = (a, m_loc, kw)
    for d, hh in heads:
        v_all = dirs[d][2]
        r = d * MLSTM_HEADS + hh
        c_s, n_s, m_s = state[d, hh]
        a, m_loc, kw = summ[d, hh]
        c_loc = _dot_tn(kw.astype(BF16), v_all[:, hsl(hh)])
        n_loc = jnp.sum(kw, axis=0, keepdims=True)
        m_new = jnp.maximum(a + m_s, m_loc)
        d_old = jnp.exp(a + m_s - m_new)
        d_new = jnp.exp(m_loc - m_new)
        c_ref[r] = d_old * c_s + d_new * c_loc
        n_ref[r] = d_old * n_s + d_new * n_loc
        m_ref[r] = jnp.broadcast_to(m_new, (1, MLSTM_DH))

    rel, m_rel, sc, inter, den = {}, {}, {}, {}, {}
    for d, hh in heads:
        _, _, _, _, _, g_t, b_t, seen, _ = dirs[d]
        ci = d * 2 * MLSTM_HEADS + hh
        cf = ci + MLSTM_HEADS
        rel[d, hh] = jnp.where(seen, g_t[ci:ci + 1, :] - b_t[cf:cf + 1, :], NEG_INF)
        m_rel[d, hh] = jnp.maximum(state[d, hh][2], jnp.max(rel[d, hh], axis=1, keepdims=True))
    for d, hh in heads:
        m_s = state[d, hh][2]
        sc[d, hh] = qk[d, hh] * jnp.exp(rel[d, hh] - m_rel[d, hh])
        inter[d, hh] = jnp.broadcast_to(jnp.exp(m_s - m_rel[d, hh]), (L, MLSTM_DH))
    for d, hh in heads:
        qn = dirs[d][0][:, hsl(hh)].astype(F32) * state[d, hh][1]
        den[d, hh] = jnp.sum(sc[d, hh] + inter[d, hh] * qn, axis=1, keepdims=True)
    for d, hh in heads:
        _, _, v_all, _, b_all, _, _, _, _ = dirs[d]
        out_ref = (hf_ref, hb_ref)[d]
        cf = d * 2 * MLSTM_HEADS + hh + MLSTM_HEADS
        num = _dot(sc[d, hh].astype(BF16), v_all[:, hsl(hh)]) + inter[d, hh] * qc[d, hh]
        floor = jnp.exp(-(b_all[:, cf:cf + 1] + m_rel[d, hh]))
        out_ref[0, :, hsl(hh)] = num * (1.0 / jnp.maximum(jnp.abs(den[d, hh]), floor))


def _mlstm(q_lat, k_ctx, v_ctx, g_ctx, k_lat, v_lat, g_lat):
    B, T, _ = q_lat.shape
    L = MLSTM_CHUNK
    nl = T // L
    nc = k_ctx.shape[1] // L
    steps = nc + nl

    fwd_c = lambda b, s: (b, jnp.minimum(s, nc - 1), 0)
    bwd_c = lambda b, s: (b, jnp.maximum(nc - 1 - s, 0), 0)
    fwd_l = lambda b, s: (b, jnp.maximum(s - nc, 0), 0)
    bwd_l = lambda b, s: (b, jnp.minimum(steps - 1 - s, nl - 1), 0)

    blk = lambda w, im: pl.BlockSpec((1, L, w), im)
    pair = lambda w, f, g: [blk(w, f), blk(w, g)]
    nheads = 2 * MLSTM_HEADS
    return pl.pallas_call(
        functools.partial(_mlstm_kernel, n_ctx_chunks=nc),
        grid=(B, steps),
        in_specs=(pair(MLSTM_W, fwd_l, bwd_l)
                  + pair(MLSTM_W, fwd_c, bwd_c) + pair(MLSTM_W, fwd_l, bwd_l)
                  + pair(MLSTM_W, fwd_c, bwd_c) + pair(MLSTM_W, fwd_l, bwd_l)
                  + pair(GATE_LANES, fwd_c, bwd_c) + pair(GATE_LANES, fwd_l, bwd_l)),
        out_specs=pair(MLSTM_W, fwd_l, bwd_l),
        out_shape=[jax.ShapeDtypeStruct((B, T, MLSTM_W), F32)] * 2,
        scratch_shapes=[pltpu.VMEM((nheads, MLSTM_DH, MLSTM_DH), F32),
                        pltpu.VMEM((nheads, 1, MLSTM_DH), F32),
                        pltpu.VMEM((nheads, 1, MLSTM_DH), F32)],
        compiler_params=_cparams(("parallel", "arbitrary")),
        name="mlstm",
    )(q_lat, q_lat, k_ctx, k_ctx, k_lat, k_lat, v_ctx, v_ctx, v_lat, v_lat,
      g_ctx, g_ctx, g_lat, g_lat)


def _attn_kernel(q_ref, kp_ref, kc_ref, kn_ref, vp_ref, vc_ref, vn_ref, kx_ref, vx_ref,
                 sink_ref, o_ref, *, nb):
    n = pl.program_id(1)
    blk = ATTN_BLOCK
    kcat = jnp.concatenate([kp_ref[0], kc_ref[0], kn_ref[0], kx_ref[0]], axis=0)
    vcat = jnp.concatenate([vp_ref[0], vc_ref[0], vn_ref[0], vx_ref[0]], axis=0)

    qi = lax.broadcasted_iota(jnp.int32, (blk, blk), 0)
    kj = lax.broadcasted_iota(jnp.int32, (blk, blk), 1)
    ok_prev = (kj >= qi) & (n > 0)
    ok_next = (kj <= qi) & (n < nb - 1)

    lane = lax.broadcasted_iota(jnp.int32, (blk, 2 * ATTN_DH), 1)
    low = lane < ATTN_DH
    n_pairs = ATTN_Q_W // (2 * ATTN_DH)
    pieces = []
    for c in range(n_pairs):
        qc = q_ref[0, :, c * 128:(c + 1) * 128]
        pieces += [jnp.where(low, qc, jnp.zeros_like(qc)), jnp.where(low, jnp.zeros_like(qc), qc)]
    s_all = _dot_nt(jnp.concatenate(pieces, axis=0), kcat)

    def fold(op, cols):
        groups = [c[:, i:i + blk] for c in cols for i in range(0, c.shape[1], blk)]
        return functools.reduce(op, groups)

    parts, maxes, probs, dens = [], [], [], []
    for j in range(2 * n_pairs):
        sc = s_all[j * blk:(j + 1) * blk]
        parts.append([jnp.where(ok_prev, sc[:, 0:blk], NEG_INF), sc[:, blk:2 * blk],
                      jnp.where(ok_next, sc[:, 2 * blk:3 * blk], NEG_INF), sc[:, 3 * blk:]])
        row_max = jnp.max(fold(jnp.maximum, parts[j]), axis=1, keepdims=True)
        maxes.append(jnp.maximum(row_max, sink_ref[j:j + 1, 0:1]))
    for j in range(2 * n_pairs):
        exps = [jnp.exp(part - maxes[j]) for part in parts[j]]
        row_sum = jnp.sum(fold(jnp.add, exps), axis=1, keepdims=True)
        dens.append(row_sum + jnp.exp(sink_ref[j:j + 1, 0:1] - maxes[j]))
        probs.append(jnp.concatenate([p.astype(BF16) for p in exps], axis=1))
    o_all = _dot(jnp.concatenate(probs, axis=0), vcat)
    for c in range(n_pairs):
        o_lo = o_all[(2 * c) * blk:(2 * c + 1) * blk] / dens[2 * c]
        o_hi = o_all[(2 * c + 1) * blk:(2 * c + 2) * blk] / dens[2 * c + 1]
        o_ref[0, :, c * 128:(c + 1) * 128] = jnp.where(low, o_lo, o_hi).astype(BF16)


def _attn(aq, ak, av, akx, avx, sink8):
    B, T, _ = aq.shape
    blk = ATTN_BLOCK
    nb = T // blk
    ctx = akx.shape[1]
    kv = lambda im: pl.BlockSpec((1, blk, ATTN_KV_W), im)
    prev = lambda b, n: (b, jnp.maximum(n - 1, 0), 0)
    cur = lambda b, n: (b, n, 0)
    nxt = lambda b, n: (b, jnp.minimum(n + 1, nb - 1), 0)
    cx = pl.BlockSpec((1, ctx, ATTN_KV_W), lambda b, n: (b, 0, 0))
    return pl.pallas_call(
        functools.partial(_attn_kernel, nb=nb),
        grid=(B, nb),
        in_specs=[pl.BlockSpec((1, blk, ATTN_Q_W), cur),
                  kv(prev), kv(cur), kv(nxt), kv(prev), kv(cur), kv(nxt), cx, cx,
                  pl.BlockSpec((8, 128), lambda b, n: (0, 0))],
        out_specs=pl.BlockSpec((1, blk, ATTN_Q_W), cur),
        out_shape=jax.ShapeDtypeStruct((B, T, ATTN_Q_W), BF16),
        compiler_params=_cparams(("parallel", "parallel")),
        name="attn",
    )(aq, ak, ak, ak, av, av, av, akx, avx, sink8)


def _merge_kernel(x_ref, sh_ref, sc_ref, g_ref, g1_ref, hf_ref, hb_ref, a_ref,
                  wg_ref, ng_ref, bdm_ref, wbm_ref, wba_ref, wo_ref, o_ref):
    x = x_ref[0]
    h = _modulated(x, g_ref[...], sh_ref[0], sc_ref[0]).astype(BF16)
    pg = _dot(h, wg_ref[...])
    d = x.shape[1]
    hs = hf_ref[0] + hb_ref[0]
    ss = _segsum(hs * hs, bdm_ref[...])
    m = hs * lax.rsqrt(ss * (1.0 / MLSTM_DH) + EPS) * ng_ref[...] * _sigmoid(pg[:, 0:MLSTM_W])
    mm = _dot(m.astype(BF16), wbm_ref[...])
    aa = _dot(a_ref[0], wba_ref[...])
    y = _sigmoid(pg[:, MLSTM_W:MLSTM_W + d]) * mm + _sigmoid(pg[:, MLSTM_W + d:]) * aa
    o_ref[0] = x + g1_ref[0] * _dot(y.astype(BF16), wo_ref[...])


def _merge(x, shift, scale, g, g1, hf, hb, a, wg, ng, bdm, wbm, wba, wo, *, tm):
    B, T, D = x.shape
    full = lambda arr: pl.BlockSpec(arr.shape, lambda b, t: (0,) * arr.ndim)
    tok = lambda w: pl.BlockSpec((1, tm, w), lambda b, t: (b, t, 0))
    perb = pl.BlockSpec((1, 1, D), lambda b, t: (b, 0, 0))
    return pl.pallas_call(
        _merge_kernel,
        grid=(B, T // tm),
        in_specs=[tok(D), perb, perb, full(g), perb, tok(MLSTM_W), tok(MLSTM_W), tok(ATTN_Q_W),
                  full(wg), full(ng), full(bdm), full(wbm), full(wba), full(wo)],
        out_specs=tok(D),
        out_shape=jax.ShapeDtypeStruct((B, T, D), F32),
        compiler_params=_cparams(("parallel", "parallel")),
        name="merge",
    )(x, shift, scale, g, g1, hf, hb, a, wg, ng, bdm, wbm, wba, wo)


def _pquery_kernel(x_ref, sh_ref, sc_ref, g_ref, wq_ref, sk_ref, h_ref, s_ref):
    x = x_ref[0]
    h = _modulated(x, g_ref[...], sh_ref[0], sc_ref[0]).astype(BF16)
    h_ref[0] = h
    q = _dot(h, wq_ref[...]).astype(BF16)
    for j in range(2 * PEER_HEADS):
        s_ref[0, j] = _dot_nt(sk_ref[j], q[:, j * PEER_HALF:(j + 1) * PEER_HALF])


def _pquery(x, shift, scale, g, wq, sk, *, tm):
    B, T, D = x.shape
    nj = 2 * PEER_HEADS
    full = lambda arr: pl.BlockSpec(arr.shape, lambda b, t: (0,) * arr.ndim)
    perb = pl.BlockSpec((1, 1, D), lambda b, t: (b, 0, 0))
    return pl.pallas_call(
        _pquery_kernel,
        grid=(B, T // tm),
        in_specs=[pl.BlockSpec((1, tm, D), lambda b, t: (b, t, 0)), perb, perb,
                  full(g), full(wq), full(sk)],
        out_specs=[pl.BlockSpec((1, tm, D), lambda b, t: (b, t, 0)),
                   pl.BlockSpec((1, nj, PEER_N_KEYS, tm), lambda b, t: (b, 0, 0, t))],
        out_shape=[jax.ShapeDtypeStruct((B, T, D), BF16),
                   jax.ShapeDtypeStruct((B, nj, PEER_N_KEYS, T), F32)],
        compiler_params=_cparams(("parallel", "parallel")),
        name="pquery",
    )(x, shift, scale, g, wq, sk)


def _top_values(val, k, with_rank=False):
    tops = []
    rank = jnp.full(val.shape, float(k), F32) if with_rank else None
    for p in range(k):
        m = jnp.max(val, axis=0, keepdims=True)
        tops.append(m)
        hit = val == m
        if with_rank:
            rank = jnp.where(hit, float(p), rank)
        val = jnp.where(hit, NEG_INF, val)
    return (tops, rank) if with_rank else tops


_GELU_K0 = -2.0 * math.sqrt(2.0 / math.pi) * math.log2(math.e)
_GELU_K1 = 0.044715 * _GELU_K0


def _gelu_tanh(x, k0, k1):
    one = jnp.asarray(1.0, x.dtype)
    return x / (one + jnp.exp2(x * (x * x * k1 + k0)))


BF16_ROWS = 16


def _rows_bf16(row, tm):
    tile = jnp.broadcast_to(row, (BF16_ROWS, tm)).astype(BF16)
    return jnp.concatenate([tile] * (PEER_N_KEYS // BF16_ROWS), axis=0)


def _row_tile_bf16(ref, h, i, tm):
    return _rows_bf16(ref[h, pl.ds(i, 1), :], tm)


def _peer_router(s_ref, rank_ref, cw_ref, cnt_ref, e2_ref):
    k = PEER_TOPK
    for h in range(PEER_HEADS):
        s1 = s_ref[0, 2 * h]
        s2 = s_ref[0, 2 * h + 1]
        a, rank = _top_values(s1, k, with_rank=True)
        b = _top_values(s2, k)
        cand = [a[p] + b[q] for p in range(k) for q in range(k) if (p + 1) * (q + 1) <= k]
        cv = jnp.concatenate(cand, axis=0)
        thr = _top_values(cv, k)[k - 1]
        top = a[0] + b[0]
        z = jnp.sum(jnp.where(cv >= thr, jnp.exp(cv - top), 0.0), axis=0, keepdims=True)
        cnt = jnp.zeros(s2.shape, F32)
        for p in range(k):
            cnt = jnp.where(a[p] + s2 >= thr, float(p + 1), cnt)
        rank_ref[h] = rank
        cw_ref[h] = jnp.exp(s1 - a[0]) / z
        cnt_ref[h] = cnt.astype(BF16)
        e2_ref[h] = jnp.exp(s2 - b[0]).astype(BF16)


PEER_TRIP = 4
PEER_RING = 2 * PEER_TRIP


def _peer_kernel(h_ref, s_ref, x_ref, g2_ref, gk_ref, u_hbm, vt_hbm, o_ref,
                 acc_ref, rank_ref, cw_ref, cnt_ref, e2_ref, wt_ref, act_ref,
                 ubuf, vbuf, usem, vsem, *, sb):
    tm = h_ref.shape[1]
    n_sub = u_hbm.shape[0]
    n_i = sb // PEER_N_KEYS
    assert n_sub % PEER_TRIP == 0 and n_sub >= 2 * PEER_TRIP

    def u_copy(j, slot):
        return pltpu.make_async_copy(u_hbm.at[j], ubuf.at[slot], usem.at[slot])

    def v_copy(j, slot):
        return pltpu.make_async_copy(vt_hbm.at[j], vbuf.at[slot], vsem.at[slot])

    for k in range(PEER_TRIP + 1):
        u_copy(k, k).start()
    v_copy(0, PEER_RING - 1).start()
    for k in range(PEER_TRIP - 1):
        v_copy(k, k).start()

    _peer_router(s_ref, rank_ref, cw_ref, cnt_ref, e2_ref)
    acc_ref[...] = jnp.zeros_like(acc_ref)
    wt_ref[1] = jnp.zeros(wt_ref.shape[1:], BF16)

    hq = h_ref[0]
    zero = jnp.zeros((), BF16)
    gk0 = _rows_bf16(gk_ref[0:1, :], tm)
    gk1 = _rows_bf16(gk_ref[1:2, :], tm)

    def scores(slot):
        return _dot_nt(ubuf[slot], hq)

    def weight_row(j, q, ii):
        i = j * n_i + ii
        w = None
        for h in range(PEER_HEADS):
            rank_i = _row_tile_bf16(rank_ref, h, i, tm)
            cw_i = _row_tile_bf16(cw_ref, h, i, tm)
            term = jnp.where(rank_i < cnt_ref[h], e2_ref[h], zero) * cw_i
            w = term if w is None else w + term
        rows = slice(ii * PEER_N_KEYS, (ii + 1) * PEER_N_KEYS)
        act_i = act_ref[q % 2, rows, :].astype(BF16)
        wt_ref[q % 2, rows, :] = w * _gelu_tanh(act_i, gk0, gk1)

    last = n_sub - 1
    slot = lambda k: lax.rem(k + PEER_RING, PEER_RING)
    clamp = lambda k: jnp.clip(k, 0, last)

    def trip(j0, *, is_last):
        for r in range(PEER_TRIP):
            u_copy(clamp(j0 + 1 + r), slot(j0 + 1 + r)).wait()
            v_copy(clamp(j0 - 1 + r), slot(j0 - 1 + r)).wait()
        for r in range(PEER_TRIP):
            if not is_last:
                u_copy(clamp(j0 + PEER_TRIP + 1 + r), slot(j0 + PEER_TRIP + 1 + r)).start()
            if not is_last or j0 + PEER_TRIP - 1 + r <= last:
                v_copy(clamp(j0 + PEER_TRIP - 1 + r), slot(j0 + PEER_TRIP - 1 + r)).start()
        for q in range(PEER_TRIP):
            j = j0 + q
            if not is_last or j + 1 <= last:
                act_ref[(q + 1) % 2] = scores(slot(j + 1))
            for ii in range(n_i):
                if ii == n_i // 2:
                    acc_ref[...] += _dot(vbuf[slot(j - 1)], wt_ref[(q + 1) % 2])
                weight_row(j, q, ii)

    u_copy(0, 0).wait()
    act_ref[0] = scores(0)

    n_trips = n_sub // PEER_TRIP

    def loop_trip(t, carry):
        trip(t * PEER_TRIP, is_last=False)
        return carry

    lax.fori_loop(0, n_trips - 1, loop_trip, 0)
    trip((n_trips - 1) * PEER_TRIP, is_last=True)
    v_copy(last, last % PEER_RING).wait()
    acc_ref[...] += _dot(vbuf[last % PEER_RING], wt_ref[last % 2])

    o_ref[0] = x_ref[0] + g2_ref[0] * acc_ref[...].T


def _peer(h2, s_t, u3, vt3, x1, g2, *, tm):
    B, T, D = x1.shape
    n_sub, sb, _ = u3.shape
    nj = 2 * PEER_HEADS
    gelu_k = jnp.broadcast_to(jnp.array([[_GELU_K0], [_GELU_K1]], F32), (2, tm))
    table = lambda dt: pltpu.VMEM((PEER_HEADS, PEER_N_KEYS, tm), dt)
    return pl.pallas_call(
        functools.partial(_peer_kernel, sb=sb),
        grid=(B, T // tm),
        in_specs=[pl.BlockSpec((1, tm, D), lambda b, t: (b, t, 0)),
                  pl.BlockSpec((1, nj, PEER_N_KEYS, tm), lambda b, t: (b, 0, 0, t)),
                  pl.BlockSpec((1, tm, D), lambda b, t: (b, t, 0)),
                  pl.BlockSpec((1, 1, D), lambda b, t: (b, 0, 0)),
                  pl.BlockSpec((2, tm), lambda b, t: (0, 0)),
                  pl.BlockSpec(memory_space=pl.ANY),
                  pl.BlockSpec(memory_space=pl.ANY)],
        out_specs=pl.BlockSpec((1, tm, D), lambda b, t: (b, t, 0)),
        out_shape=jax.ShapeDtypeStruct((B, T, D), F32),
        scratch_shapes=[pltpu.VMEM((D, tm), F32),
                        table(F32), table(F32), table(BF16), table(BF16),
                        pltpu.VMEM((2, sb, tm), BF16),
                        pltpu.VMEM((2, sb, tm), F32),
                        pltpu.VMEM((PEER_RING, sb, D), BF16),
                        pltpu.VMEM((PEER_RING, D, sb), BF16),
                        pltpu.SemaphoreType.DMA((PEER_RING,)),
                        pltpu.SemaphoreType.DMA((PEER_RING,))],
        compiler_params=_cparams(("parallel", "parallel")),
        name="peer",
    )(h2, s_t, x1, g2, gelu_k, u3, vt3)


_Q_HEAD_ORDER = (0, 4, 1, 5, 2, 6, 3, 7)


def _block_diag_ones(width, seg):
    i = jnp.arange(width)
    return (i[:, None] // seg == i[None, :] // seg).astype(BF16)


def _rope_tables(T):
    rows = T // GRID_W
    nf = ATTN_DH // 4
    row = jnp.broadcast_to(jnp.arange(rows)[:, None], (rows, GRID_W)).reshape(T)
    col = jnp.broadcast_to(jnp.arange(GRID_W)[None, :], (rows, GRID_W)).reshape(T)
    inv = ROPE_BASE ** (-jnp.arange(nf, dtype=F32) / nf)
    ang = jnp.concatenate([row[:, None].astype(F32) * inv, col[:, None].astype(F32) * inv], axis=-1)
    cos, sin = jnp.cos(ang), jnp.sin(ang)
    cos_t = jnp.concatenate([cos, cos] * ATTN_KV_HEADS, axis=-1)
    sin_t = jnp.concatenate([-sin, sin] * ATTN_KV_HEADS, axis=-1)
    return cos_t, sin_t


def _layer(x, ctx, mod, norm_mix_g, norm_ffn_g, w_in, b_mgates, mlstm_norm_g, attn_q_norm_g,
           attn_k_norm_g, attn_sink, w_branch_m, w_branch_a, w_out, peer_w_query, peer_sub_keys,
           peer_u, peer_v, cos_t, sin_t):
    B, T, D = x.shape
    n_ctx = ctx.shape[1]
    mods = jnp.split(mod, 6, axis=-1)
    lat = [m[0:B].reshape(B, 1, D) for m in mods]
    cxm = [jnp.broadcast_to(m[B:B + 1].reshape(1, 1, D), (B, 1, D)) for m in mods]
    sh1, sc1, g1, sh2, sc2, g2 = lat

    W = MLSTM_W
    o0 = 4 * W
    o1 = o0 + 2 * 2 * MLSTM_HEADS
    w_mq, w_mk, w_mv, w_mo = (w_in[:, i * W:(i + 1) * W] for i in range(4))
    w_g16 = w_in[:, o0:o1]
    w_aq = w_in[:, o1:o1 + ATTN_Q_W].reshape(D, ATTN_Q_HEADS, ATTN_DH)
    w_aq = w_aq[:, jnp.array(_Q_HEAD_ORDER), :].reshape(D, ATTN_Q_W)
    o2 = o1 + ATTN_Q_W
    w_ak = w_in[:, o2:o2 + ATTN_KV_W]
    w_av = w_in[:, o2 + ATTN_KV_W:o2 + 2 * ATTN_KV_W]
    w_gates = w_in[:, o2 + 2 * ATTN_KV_W:]
    w_g128 = jnp.pad(w_g16, ((0, 0), (0, GATE_LANES - w_g16.shape[1])))
    w_pack = jnp.concatenate([w_mq, w_mk, w_mv, w_aq, w_ak, w_av, w_g128], axis=1).astype(BF16)
    w_merge_in = jnp.concatenate([w_mo, w_gates], axis=1).astype(BF16)
    bg = jnp.pad(b_mgates, (0, GATE_LANES - b_mgates.shape[0])).reshape(1, GATE_LANES)
    qg = jnp.tile(attn_q_norm_g, ATTN_Q_HEADS).reshape(1, ATTN_Q_W)
    kg = jnp.tile(attn_k_norm_g, ATTN_KV_HEADS).reshape(1, ATTN_KV_W)
    bdq = _block_diag_ones(ATTN_Q_W, ATTN_DH)
    bdk = _block_diag_ones(ATTN_KV_W, ATTN_DH)
    bdm = _block_diag_ones(MLSTM_W, MLSTM_DH)
    gmix = norm_mix_g.reshape(1, D)

    tm_lat = min(512, T)
    mq, mk, mv, mg, aq, ak, av = _proj(x, sh1, sc1, gmix, w_pack, bg, qg, kg, bdq, bdk,
                                       cos_t, sin_t, rope=True, tm=tm_lat)
    _, mkc, mvc, mgc, _, akc, avc = _proj(ctx, cxm[0], cxm[1], gmix, w_pack, bg, qg, kg, bdq, bdk,
                                          cos_t, sin_t, rope=False, tm=n_ctx)

    hf, hb = _mlstm(mq, mkc, mvc, mgc, mk, mv, mg)

    sink8 = jnp.broadcast_to(attn_sink[jnp.array(_Q_HEAD_ORDER)].astype(F32)[:, None], (8, 128))
    a = _attn(aq, ak, av, akc, avc, sink8)

    wba = w_branch_a.reshape(ATTN_Q_HEADS, ATTN_DH, D)[jnp.array(_Q_HEAD_ORDER)]
    wba = wba.reshape(ATTN_Q_W, D).astype(BF16)
    x1 = _merge(x, sh1, sc1, gmix, g1, hf, hb, a, w_merge_in, mlstm_norm_g.reshape(1, MLSTM_W),
                bdm, w_branch_m.astype(BF16), wba, w_out.astype(BF16), tm=min(256, T))

    sk = peer_sub_keys.reshape(2 * PEER_HEADS, PEER_N_KEYS, PEER_HALF).astype(BF16)
    h2, s_t = _pquery(x1, sh2, sc2, norm_ffn_g.reshape(1, D), peer_w_query.astype(BF16), sk,
                      tm=min(512, T))
    sb = PEER_SUB_BLOCK
    u3 = peer_u.astype(BF16).reshape(-1, sb, D)
    vt3 = peer_v.astype(BF16).reshape(-1, sb, D).transpose(0, 2, 1)
    return _peer(h2, s_t, u3, vt3, x1, g2, tm=min(512, T))


def kernel(x, c, ctx, c_ctx, w_ada, b_ada, norm_mix_g, norm_ffn_g, w_in, b_mgates, mlstm_norm_g,
           attn_q_norm_g, attn_k_norm_g, attn_sink, w_branch_m, w_branch_a, w_out, peer_w_query,
           peer_sub_keys, peer_u, peer_v):
    B, T, D = x.shape
    depth = w_ada.shape[0]
    assert depth == 1, "context-stream update for deeper stacks is not implemented"
    cos_t, sin_t = _rope_tables(T)
    cond8 = jnp.zeros((8, D), F32).at[0:B].set(c).at[B].set(c_ctx)
    for layer in range(depth):
        mod = _adaln(cond8, w_ada[layer], b_ada[layer])
        x = _layer(x, ctx, mod, norm_mix_g[layer], norm_ffn_g[layer], w_in[layer], b_mgates[layer],
                   mlstm_norm_g[layer], attn_q_norm_g[layer], attn_k_norm_g[layer], attn_sink[layer],
                   w_branch_m[layer], w_branch_a[layer], w_out[layer], peer_w_query[layer],
                   peer_sub_keys[layer], peer_u[layer], peer_v[layer], cos_t, sin_t)
    return x
```

```python
import functools
import math

import jax
import jax.numpy as jnp
from jax import lax
from jax.experimental import pallas as pl
from jax.experimental.pallas import tpu as pltpu

F32 = jnp.float32
BF16 = jnp.bfloat16

EPS = 1e-6
GRID_W = 64
ROPE_BASE = 10000.0

MLSTM_HEADS = 4
MLSTM_DH = 128
MLSTM_W = MLSTM_HEADS * MLSTM_DH
MLSTM_CHUNK = 128
GATE_LANES = 128

ATTN_Q_HEADS = 8
ATTN_KV_HEADS = 2
ATTN_DH = 64
ATTN_Q_W = ATTN_Q_HEADS * ATTN_DH
ATTN_KV_W = ATTN_KV_HEADS * ATTN_DH
ATTN_BLOCK = 128

PEER_HEADS = 8
PEER_N_KEYS = 128
PEER_HALF = 128
PEER_TOPK = 16

VMEM_LIMIT_BYTES = 56 * 1024 * 1024

NEG_INF = float("-inf")
POS_INF = float("inf")


def _cparams(sem):
    return pltpu.CompilerParams(dimension_semantics=sem, vmem_limit_bytes=VMEM_LIMIT_BYTES)


def _dot(a, b):
    return jnp.dot(a, b, preferred_element_type=F32)


def _dot_nt(a, b):
    return lax.dot_general(a, b, (((1,), (1,)), ((), ())), preferred_element_type=F32)


def _dot_tn(a, b):
    return lax.dot_general(a, b, (((0,), (0,)), ((), ())), preferred_element_type=F32)


def _sigmoid(x):
    return 1.0 / (1.0 + jnp.exp(-x))


def _segsum(x2, bd):
    hi = x2.astype(BF16)
    lo = (x2 - hi.astype(F32)).astype(BF16)
    return _dot(hi, bd) + _dot(lo, bd)


def _modulated(x, g, shift, scale):
    ms = jnp.mean(x * x, axis=-1, keepdims=True)
    return (x * lax.rsqrt(ms + EPS)) * g * (1.0 + scale) + shift


def _adaln_kernel(c_ref, w_ref, b_ref, o_ref):
    c = c_ref[...]
    s = c * _sigmoid(c)
    o_ref[...] = jnp.dot(s, w_ref[...], preferred_element_type=F32,
                         precision=lax.Precision.HIGHEST) + b_ref[...]


def _adaln(cond8, w, b):
    d = cond8.shape[1]
    n = w.shape[1]
    bn = n // 6
    return pl.pallas_call(
        _adaln_kernel,
        grid=(n // bn,),
        in_specs=[pl.BlockSpec((8, d), lambda j: (0, 0)),
                  pl.BlockSpec((d, bn), lambda j: (0, j)),
                  pl.BlockSpec((1, bn), lambda j: (0, j))],
        out_specs=pl.BlockSpec((8, bn), lambda j: (0, j)),
        out_shape=jax.ShapeDtypeStruct((8, n), F32),
        compiler_params=_cparams(("arbitrary",)),
        name="adaln",
    )(cond8, w, b.reshape(1, n))


_C_MQ, _C_MK, _C_MV = 0, 512, 1024
_C_AQ, _C_AK, _C_AV, _C_MG = 1536, 2048, 2176, 2304
_C_END = 2432


def _rope(t, cos_t, sin_t, width):
    lane = lax.broadcasted_iota(jnp.int32, t.shape, 1)
    lower = (lane % ATTN_DH) < (ATTN_DH // 2)
    swapped = jnp.where(lower, pltpu.roll(t, width - ATTN_DH // 2, 1),
                        pltpu.roll(t, ATTN_DH // 2, 1))
    return t * cos_t + swapped * sin_t


def _proj_kernel(x_ref, sh_ref, sc_ref, g_ref, w_ref, bg_ref, qg_ref, kg_ref,
                 bdq_ref, bdk_ref, cos_ref, sin_ref,
                 mq_ref, mk_ref, mv_ref, mg_ref, aq_ref, ak_ref, av_ref, *, rope):
    x = x_ref[0]
    h = _modulated(x, g_ref[...], sh_ref[0], sc_ref[0]).astype(BF16)

    p = _dot(h, w_ref[:, _C_MQ:_C_AQ])
    mq_ref[0] = p[:, 0:512].astype(BF16)
    mk_ref[0] = (p[:, 512:1024] * (MLSTM_DH ** -0.5)).astype(BF16)
    mv_ref[0] = p[:, 1024:1536].astype(BF16)

    pq = _dot(h, w_ref[:, _C_AQ:_C_AK])
    ssq = _segsum(pq * pq, bdq_ref[...])
    q = pq * lax.rsqrt(ssq * (1.0 / ATTN_DH) + EPS) * qg_ref[...]
    if rope:
        cos_t = cos_ref[...]
        sin_t = sin_ref[...]
        q = _rope(q, jnp.concatenate([cos_t] * 4, axis=1), jnp.concatenate([sin_t] * 4, axis=1),
                  ATTN_Q_W)
    aq_ref[0] = (q * (ATTN_DH ** -0.5)).astype(BF16)

    pk = _dot(h, w_ref[:, _C_AK:_C_AV])
    ssk = _segsum(pk * pk, bdk_ref[...])
    k = pk * lax.rsqrt(ssk * (1.0 / ATTN_DH) + EPS) * kg_ref[...]
    if rope:
        k = _rope(k, cos_ref[...], sin_ref[...], ATTN_KV_W)
    ak_ref[0] = k.astype(BF16)

    av_ref[0] = _dot(h, w_ref[:, _C_AV:_C_MG]).astype(BF16)

    z = _dot(h, w_ref[:, _C_MG:_C_END]) + bg_ref[...]
    lane = lax.broadcasted_iota(jnp.int32, z.shape, 1)
    is_forget = (lane % 8) >= MLSTM_HEADS
    logsig = jnp.minimum(z, 0.0) - jnp.log1p(jnp.exp(-jnp.abs(z)))
    mg_ref[0] = jnp.where(is_forget, logsig, z)


def _proj(x, shift, scale, g, w_pack, bg, qg, kg, bdq, bdk, cos_t, sin_t, *, rope, tm):
    B, T, D = x.shape
    nt = T // tm
    full = lambda shape: pl.BlockSpec(shape, lambda b, t: (0,) * len(shape))
    tok = lambda w: pl.BlockSpec((1, tm, w), lambda b, t: (b, t, 0))
    outs = [(MLSTM_W, BF16), (MLSTM_W, BF16), (MLSTM_W, BF16), (GATE_LANES, F32),
            (ATTN_Q_W, BF16), (ATTN_KV_W, BF16), (ATTN_KV_W, BF16)]
    return pl.pallas_call(
        functools.partial(_proj_kernel, rope=rope),
        grid=(B, nt),
        in_specs=[tok(D),
                  pl.BlockSpec((1, 1, D), lambda b, t: (b, 0, 0)),
                  pl.BlockSpec((1, 1, D), lambda b, t: (b, 0, 0)),
                  full((1, D)), full(w_pack.shape), full((1, GATE_LANES)),
                  full((1, ATTN_Q_W)), full((1, ATTN_KV_W)),
                  full((ATTN_Q_W, ATTN_Q_W)), full((ATTN_KV_W, ATTN_KV_W)),
                  pl.BlockSpec((tm, ATTN_KV_W), lambda b, t: (t, 0)),
                  pl.BlockSpec((tm, ATTN_KV_W), lambda b, t: (t, 0))],
        out_specs=[tok(w) for w, _ in outs],
        out_shape=[jax.ShapeDtypeStruct((B, T, w), dt) for w, dt in outs],
        compiler_params=_cparams(("parallel", "parallel")),
        name="proj_rope" if rope else "proj_ctx",
    )(x, shift, scale, g, w_pack, bg, qg, kg, bdq, bdk, cos_t, sin_t)


def _mlstm_kernel(qf_ref, qb_ref, kcf_ref, kcb_ref, klf_ref, klb_ref, vcf_ref, vcb_ref,
                  vlf_ref, vlb_ref, gcf_ref, gcb_ref, glf_ref, glb_ref,
                  hf_ref, hb_ref, c_ref, n_ref, m_ref, *, n_ctx_chunks):
    s = pl.program_id(0)
    L = MLSTM_CHUNK
    n_batch = qf_ref.shape[0]
    in_ctx = s < n_ctx_chunks

    @pl.when(s == 0)
    def _():
        c_ref[...] = jnp.zeros_like(c_ref)
        n_ref[...] = jnp.zeros_like(n_ref)
        m_ref[...] = jnp.zeros_like(m_ref)

    row = lax.broadcasted_iota(jnp.int32, (L, L), 0)
    col = lax.broadcasted_iota(jnp.int32, (L, L), 1)
    tri_f = (col <= row).astype(F32)
    pick = lambda c_ref_, l_ref_, bb: jnp.where(in_ctx, c_ref_[bb], l_ref_[bb])


    dirs = {}
    for bb in range(n_batch):
        for d in range(2):
            q_all = (qf_ref, qb_ref)[d][bb]
            k_all = pick((kcf_ref, kcb_ref)[d], (klf_ref, klb_ref)[d], bb)
            v_all = pick((vcf_ref, vcb_ref)[d], (vlf_ref, vlb_ref)[d], bb)
            g = pick((gcf_ref, gcb_ref)[d], (glf_ref, glb_ref)[d], bb)
            cum_f = jnp.dot(tri_f, g, preferred_element_type=F32, precision=lax.Precision.HIGHEST)
            if d == 0:
                b_all, seen, last = cum_f, col <= row, L - 1
            else:
                b_all, seen, last = cum_f[L - 1:L, :] - cum_f + g, col >= row, 0
            dirs[bb, d] = (q_all, k_all, v_all, g, b_all, g.T, b_all.T, seen, last)

    heads = [(bb, d, hh) for bb in range(n_batch) for d in range(2) for hh in range(MLSTM_HEADS)]
    hsl = lambda hh: slice(hh * MLSTM_DH, (hh + 1) * MLSTM_DH)
    sidx = lambda bb, d, hh: (bb * 2 + d) * MLSTM_HEADS + hh
    gate_cols = lambda d, hh: (d * 2 * MLSTM_HEADS + hh, d * 2 * MLSTM_HEADS + hh + MLSTM_HEADS)

    qk, qc, state = {}, {}, {}
    for key in heads:
        bb, d, hh = key
        q_all, k_all = dirs[bb, d][0], dirs[bb, d][1]
        r = sidx(*key)
        c_s = c_ref[r]
        state[key] = (c_s, n_ref[r], m_ref[r][:, 0:1])
        qk[key] = _dot_nt(q_all[:, hsl(hh)], k_all[:, hsl(hh)])
        qc[key] = _dot(q_all[:, hsl(hh)], c_s.astype(BF16))

    summ = {}
    for key in heads:
        bb, d, hh = key
        _, k_all, _, g, b_all, _, _, _, last = dirs[bb, d]
        ci, cf = gate_cols(d, hh)
        a = b_all[last:last + 1, cf:cf + 1]
        w_log = a - b_all[:, cf:cf + 1] + g[:, ci:ci + 1]
        m_loc = jnp.max(w_log, axis=0, keepdims=True)
        kw = k_all[:, hsl(hh)].astype(F32) * jnp.exp(w_log - m_loc)
        summ[key] = (a, m_loc, kw)
    for key in heads:
        bb, d, hh = key
        v_all = dirs[bb, d][2]
        r = sidx(*key)
        c_s, n_s, m_s = state[key]
        a, m_loc, kw = summ[key]
        c_loc = _dot_tn(kw.astype(BF16), v_all[:, hsl(hh)])
        n_loc = jnp.sum(kw, axis=0, keepdims=True)
        m_new = jnp.maximum(a + m_s, m_loc)
        d_old = jnp.exp(a + m_s - m_new)
        d_new = jnp.exp(m_loc - m_new)
        c_ref[r] = d_old * c_s + d_new * c_loc
        n_ref[r] = d_old * n_s + d_new * n_loc
        m_ref[r] = jnp.broadcast_to(m_new, (1, MLSTM_DH))

    rel, m_rel, sc, inter, den = {}, {}, {}, {}, {}
    for key in heads:
        bb, d, hh = key
        _, _, _, _, _, g_t, b_t, seen, _ = dirs[bb, d]
        ci, cf = gate_cols(d, hh)
        rel[key] = jnp.where(seen, g_t[ci:ci + 1, :] - b_t[cf:cf + 1, :], NEG_INF)
        m_rel[key] = jnp.maximum(state[key][2], jnp.max(rel[key], axis=1, keepdims=True))
    for key in heads:
        m_s = state[key][2]
        sc[key] = qk[key] * jnp.exp(rel[key] - m_rel[key])
        inter[key] = jnp.broadcast_to(jnp.exp(m_s - m_rel[key]), (L, MLSTM_DH))
    for key in heads:
        bb, d, hh = key
        qn = dirs[bb, d][0][:, hsl(hh)].astype(F32) * state[key][1]
        den[key] = jnp.sum(sc[key] + inter[key] * qn, axis=1, keepdims=True)
    for key in heads:
        bb, d, hh = key
        _, _, v_all, _, b_all, _, _, _, _ = dirs[bb, d]
        out_ref = (hf_ref, hb_ref)[d]
        cf = gate_cols(d, hh)[1]
        num = _dot(sc[key].astype(BF16), v_all[:, hsl(hh)]) + inter[key] * qc[key]
        floor = jnp.exp(-(b_all[:, cf:cf + 1] + m_rel[key]))
        out_ref[bb, :, hsl(hh)] = num * (1.0 / jnp.maximum(jnp.abs(den[key]), floor))


def _mlstm(q_lat, k_ctx, v_ctx, g_ctx, k_lat, v_lat, g_lat):
    B, T, _ = q_lat.shape
    L = MLSTM_CHUNK
    nl = T // L
    nc = k_ctx.shape[1] // L
    steps = nc + nl

    fwd_c = lambda s: (0, jnp.minimum(s, nc - 1), 0)
    bwd_c = lambda s: (0, jnp.maximum(nc - 1 - s, 0), 0)
    fwd_l = lambda s: (0, jnp.maximum(s - nc, 0), 0)
    bwd_l = lambda s: (0, jnp.minimum(steps - 1 - s, nl - 1), 0)

    blk = lambda w, im: pl.BlockSpec((B, L, w), im)
    pair = lambda w, f, g: [blk(w, f), blk(w, g)]
    nheads = B * 2 * MLSTM_HEADS
    return pl.pallas_call(
        functools.partial(_mlstm_kernel, n_ctx_chunks=nc),
        grid=(steps,),
        in_specs=(pair(MLSTM_W, fwd_l, bwd_l)
                  + pair(MLSTM_W, fwd_c, bwd_c) + pair(MLSTM_W, fwd_l, bwd_l)
                  + pair(MLSTM_W, fwd_c, bwd_c) + pair(MLSTM_W, fwd_l, bwd_l)
                  + pair(GATE_LANES, fwd_c, bwd_c) + pair(GATE_LANES, fwd_l, bwd_l)),
        out_specs=pair(MLSTM_W, fwd_l, bwd_l),
        out_shape=[jax.ShapeDtypeStruct((B, T, MLSTM_W), F32)] * 2,
        scratch_shapes=[pltpu.VMEM((nheads, MLSTM_DH, MLSTM_DH), F32),
                        pltpu.VMEM((nheads, 1, MLSTM_DH), F32),
                        pltpu.VMEM((nheads, 1, MLSTM_DH), F32)],
        compiler_params=_cparams(("arbitrary",)),
        name="mlstm",
    )(q_lat, q_lat, k_ctx, k_ctx, k_lat, k_lat, v_ctx, v_ctx, v_lat, v_lat,
      g_ctx, g_ctx, g_lat, g_lat)


def _attn_kernel(q_ref, kp_ref, kc_ref, kn_ref, vp_ref, vc_ref, vn_ref, kx_ref, vx_ref,
                 sink_ref, o_ref, *, nb):
    n = pl.program_id(0)
    blk = ATTN_BLOCK
    n_batch = q_ref.shape[0]

    qi = lax.broadcasted_iota(jnp.int32, (blk, blk), 0)
    kj = lax.broadcasted_iota(jnp.int32, (blk, blk), 1)
    ok_prev = (kj >= qi) & (n > 0)
    ok_next = (kj <= qi) & (n < nb - 1)
    lane = lax.broadcasted_iota(jnp.int32, (blk, 2 * ATTN_DH), 1)
    low = lane < ATTN_DH
    n_pairs = ATTN_Q_W // (2 * ATTN_DH)
    n_heads = 2 * n_pairs

    def fold(op, cols):
        groups = [c[:, i:i + blk] for c in cols for i in range(0, c.shape[1], blk)]
        return functools.reduce(op, groups)

    s_all, vcat = [], []
    for bb in range(n_batch):
        kcat = jnp.concatenate([kp_ref[bb], kc_ref[bb], kn_ref[bb], kx_ref[bb]], axis=0)
        vcat.append(jnp.concatenate([vp_ref[bb], vc_ref[bb], vn_ref[bb], vx_ref[bb]], axis=0))
        pieces = []
        for c in range(n_pairs):
            qc = q_ref[bb, :, c * 128:(c + 1) * 128]
            pieces += [jnp.where(low, qc, jnp.zeros_like(qc)), jnp.where(low, jnp.zeros_like(qc), qc)]
        s_all.append(_dot_nt(jnp.concatenate(pieces, axis=0), kcat))

    parts, maxes, probs, dens = {}, {}, {}, {}
    keys = [(bb, j) for bb in range(n_batch) for j in range(n_heads)]
    for bb, j in keys:
        sc = s_all[bb][j * blk:(j + 1) * blk]
        parts[bb, j] = [jnp.where(ok_prev, sc[:, 0:blk], NEG_INF), sc[:, blk:2 * blk],
                        jnp.where(ok_next, sc[:, 2 * blk:3 * blk], NEG_INF), sc[:, 3 * blk:]]
        row_max = jnp.max(fold(jnp.maximum, parts[bb, j]), axis=1, keepdims=True)
        maxes[bb, j] = jnp.maximum(row_max, sink_ref[j:j + 1, 0:1])
    for bb, j in keys:
        exps = [jnp.exp(part - maxes[bb, j]) for part in parts[bb, j]]
        row_sum = jnp.sum(fold(jnp.add, exps), axis=1, keepdims=True)
        dens[bb, j] = row_sum + jnp.exp(sink_ref[j:j + 1, 0:1] - maxes[bb, j])
        probs[bb, j] = jnp.concatenate([p.astype(BF16) for p in exps], axis=1)
    for bb in range(n_batch):
        o_all = _dot(jnp.concatenate([probs[bb, j] for j in range(n_heads)], axis=0), vcat[bb])
        for c in range(n_pairs):
            o_lo = o_all[(2 * c) * blk:(2 * c + 1) * blk] / dens[bb, 2 * c]
            o_hi = o_all[(2 * c + 1) * blk:(2 * c + 2) * blk] / dens[bb, 2 * c + 1]
            o_ref[bb, :, c * 128:(c + 1) * 128] = jnp.where(low, o_lo, o_hi).astype(BF16)


def _attn(aq, ak, av, akx, avx, sink8):
    B, T, _ = aq.shape
    blk = ATTN_BLOCK
    nb = T // blk
    ctx = akx.shape[1]
    kv = lambda im: pl.BlockSpec((B, blk, ATTN_KV_W), im)
    prev = lambda n: (0, jnp.maximum(n - 1, 0), 0)
    cur = lambda n: (0, n, 0)
    nxt = lambda n: (0, jnp.minimum(n + 1, nb - 1), 0)
    cx = pl.BlockSpec((B, ctx, ATTN_KV_W), lambda n: (0, 0, 0))
    return pl.pallas_call(
        functools.partial(_attn_kernel, nb=nb),
        grid=(nb,),
        in_specs=[pl.BlockSpec((B, blk, ATTN_Q_W), cur),
                  kv(prev), kv(cur), kv(nxt), kv(prev), kv(cur), kv(nxt), cx, cx,
                  pl.BlockSpec((8, 128), lambda n: (0, 0))],
        out_specs=pl.BlockSpec((B, blk, ATTN_Q_W), cur),
        out_shape=jax.ShapeDtypeStruct((B, T, ATTN_Q_W), BF16),
        compiler_params=_cparams(("parallel",)),
        name="attn",
    )(aq, ak, ak, ak, av, av, av, akx, avx, sink8)


def _merge_kernel(x_ref, sh_ref, sc_ref, g_ref, g1_ref, hf_ref, hb_ref, a_ref,
                  wg_ref, ng_ref, bdm_ref, wbm_ref, wba_ref, wo_ref, o_ref):
    x = x_ref[0]
    h = _modulated(x, g_ref[...], sh_ref[0], sc_ref[0]).astype(BF16)
    pg = _dot(h, wg_ref[...])
    d = x.shape[1]
    hs = hf_ref[0] + hb_ref[0]
    ss = _segsum(hs * hs, bdm_ref[...])
    m = hs * lax.rsqrt(ss * (1.0 / MLSTM_DH) + EPS) * ng_ref[...] * _sigmoid(pg[:, 0:MLSTM_W])
    mm = _dot(m.astype(BF16), wbm_ref[...])
    aa = _dot(a_ref[0], wba_ref[...])
    y = _sigmoid(pg[:, MLSTM_W:MLSTM_W + d]) * mm + _sigmoid(pg[:, MLSTM_W + d:]) * aa
    o_ref[0] = x + g1_ref[0] * _dot(y.astype(BF16), wo_ref[...])


def _merge(x, shift, scale, g, g1, hf, hb, a, wg, ng, bdm, wbm, wba, wo, *, tm):
    B, T, D = x.shape
    full = lambda arr: pl.BlockSpec(arr.shape, lambda b, t: (0,) * arr.ndim)
    tok = lambda w: pl.BlockSpec((1, tm, w), lambda b, t: (b, t, 0))
    perb = pl.BlockSpec((1, 1, D), lambda b, t: (b, 0, 0))
    return pl.pallas_call(
        _merge_kernel,
        grid=(B, T // tm),
        in_specs=[tok(D), perb, perb, full(g), perb, tok(MLSTM_W), tok(MLSTM_W), tok(ATTN_Q_W),
                  full(wg), full(ng), full(bdm), full(wbm), full(wba), full(wo)],
        out_specs=tok(D),
        out_shape=jax.ShapeDtypeStruct((B, T, D), F32),
        compiler_params=_cparams(("parallel", "parallel")),
        name="merge",
    )(x, shift, scale, g, g1, hf, hb, a, wg, ng, bdm, wbm, wba, wo)


def _pquery_kernel(x_ref, sh_ref, sc_ref, g_ref, wq_ref, sk_ref, h_ref, s_ref):
    x = x_ref[0]
    h = _modulated(x, g_ref[...], sh_ref[0], sc_ref[0]).astype(BF16)
    h_ref[0] = h
    q = _dot(h, wq_ref[...]).astype(BF16)
    for j in range(2 * PEER_HEADS):
        s_ref[0, j] = _dot_nt(sk_ref[j], q[:, j * PEER_HALF:(j + 1) * PEER_HALF])


def _pquery(x, shift, scale, g, wq, sk, *, tm):
    B, T, D = x.shape
    nj = 2 * PEER_HEADS
    full = lambda arr: pl.BlockSpec(arr.shape, lambda b, t: (0,) * arr.ndim)
    perb = pl.BlockSpec((1, 1, D), lambda b, t: (b, 0, 0))
    return pl.pallas_call(
        _pquery_kernel,
        grid=(B, T // tm),
        in_specs=[pl.BlockSpec((1, tm, D), lambda b, t: (b, t, 0)), perb, perb,
                  full(g), full(wq), full(sk)],
        out_specs=[pl.BlockSpec((1, tm, D), lambda b, t: (b, t, 0)),
                   pl.BlockSpec((1, nj, PEER_N_KEYS, tm), lambda b, t: (b, 0, 0, t))],
        out_shape=[jax.ShapeDtypeStruct((B, T, D), BF16),
                   jax.ShapeDtypeStruct((B, nj, PEER_N_KEYS, T), F32)],
        compiler_params=_cparams(("parallel", "parallel")),
        name="pquery",
    )(x, shift, scale, g, wq, sk)


def _top_values(val, k, with_rank=False):
    tops = []
    rank = jnp.full(val.shape, float(k), F32) if with_rank else None
    for p in range(k):
        m = jnp.max(val, axis=0, keepdims=True)
        tops.append(m)
        hit = val == m
        if with_rank:
            rank = jnp.where(hit, float(p), rank)
        val = jnp.where(hit, NEG_INF, val)
    return (tops, rank) if with_rank else tops


_GELU_K0 = -2.0 * math.sqrt(2.0 / math.pi) * math.log2(math.e)
_GELU_K1 = 0.044715 * _GELU_K0


def _gelu_tanh(x, k0, k1):
    one = jnp.asarray(1.0, x.dtype)
    return x / (one + jnp.exp2(x * (x * x * k1 + k0)))


BF16_ROWS = 16


def _rows_bf16(row, tm):
    tile = jnp.broadcast_to(row, (BF16_ROWS, tm)).astype(BF16)
    return jnp.concatenate([tile] * (PEER_N_KEYS // BF16_ROWS), axis=0)


def _row_tile_bf16(ref, h, i, tm):
    return _rows_bf16(ref[h, pl.ds(i, 1), :], tm)


def _peer_router(s_ref, rank_ref, cw_ref, cnt_ref, e2_ref):
    k = PEER_TOPK
    for h in range(PEER_HEADS):
        s1 = s_ref[0, 2 * h]
        s2 = s_ref[0, 2 * h + 1]
        a, rank = _top_values(s1, k, with_rank=True)
        b = _top_values(s2, k)
        cand = [a[p] + b[q] for p in range(k) for q in range(k) if (p + 1) * (q + 1) <= k]
        cv = jnp.concatenate(cand, axis=0)
        thr = _top_values(cv, k)[k - 1]
        top = a[0] + b[0]
        z = jnp.sum(jnp.where(cv >= thr, jnp.exp(cv - top), 0.0), axis=0, keepdims=True)
        cnt = jnp.zeros(s2.shape, F32)
        for p in range(k):
            cnt = jnp.where(a[p] + s2 >= thr, float(p + 1), cnt)
        rank_ref[h] = rank
        cw_ref[h] = jnp.exp(s1 - a[0]) / z
        cnt_ref[h] = cnt.astype(BF16)
        e2_ref[h] = jnp.exp(s2 - b[0]).astype(BF16)


def _peer_kernel(h_ref, s_ref, u_ref, vt_ref, x_ref, g2_ref, gk_ref, o_ref,
                 acc_ref, rank_ref, cw_ref, cnt_ref, e2_ref, wt_ref, *, eb, sb):
    e = pl.program_id(2)
    ne = pl.num_programs(2)
    tm = h_ref.shape[1]

    @pl.when(e == 0)
    def _():
        acc_ref[...] = jnp.zeros_like(acc_ref)
        _peer_router(s_ref, rank_ref, cw_ref, cnt_ref, e2_ref)

    hq = h_ref[0]
    n_i = sb // PEER_N_KEYS
    n_sub = eb // sb
    scores = lambda j: _dot_nt(u_ref[j * sb:(j + 1) * sb, :], hq)
    zero = jnp.zeros((), BF16)
    gk0 = _rows_bf16(gk_ref[0:1, :], tm)
    gk1 = _rows_bf16(gk_ref[1:2, :], tm)
    def output_product(j):
        acc_ref[...] += _dot(vt_ref[:, j * sb:(j + 1) * sb], wt_ref[j % 2])

    acts = {j: scores(j) for j in range(min(2, n_sub))}
    for j in range(n_sub):
        act = acts.pop(j)
        for ii in range(n_i):
            if ii == n_i // 2 and j > 0:
                output_product(j - 1)
            i = e * (eb // PEER_N_KEYS) + j * n_i + ii
            w = None
            for h in range(PEER_HEADS):
                rank_i = _row_tile_bf16(rank_ref, h, i, tm)
                cw_i = _row_tile_bf16(cw_ref, h, i, tm)
                term = jnp.where(rank_i < cnt_ref[h], e2_ref[h], zero) * cw_i
                w = term if w is None else w + term
            act_i = act[ii * PEER_N_KEYS:(ii + 1) * PEER_N_KEYS].astype(BF16)
            wt_ref[j % 2, ii * PEER_N_KEYS:(ii + 1) * PEER_N_KEYS, :] = w * _gelu_tanh(act_i, gk0, gk1)
        if j + 2 < n_sub:
            acts[j + 2] = scores(j + 2)
    output_product(n_sub - 1)

    @pl.when(e == ne - 1)
    def _():
        o_ref[0] = x_ref[0] + g2_ref[0] * acc_ref[...].T


def _peer(h2, s_t, u, vt, x1, g2, *, tm, eb, sb):
    B, T, D = x1.shape
    ne = u.shape[0] // eb
    nj = 2 * PEER_HEADS
    gelu_k = jnp.broadcast_to(jnp.array([[_GELU_K0], [_GELU_K1]], F32), (2, tm))
    return pl.pallas_call(
        functools.partial(_peer_kernel, eb=eb, sb=sb),
        grid=(B, T // tm, ne),
        in_specs=[pl.BlockSpec((1, tm, D), lambda b, t, e: (b, t, 0)),
                  pl.BlockSpec((1, nj, PEER_N_KEYS, tm), lambda b, t, e: (b, 0, 0, t)),
                  pl.BlockSpec((eb, D), lambda b, t, e: (e, 0)),
                  pl.BlockSpec((D, eb), lambda b, t, e: (0, e)),
                  pl.BlockSpec((1, tm, D), lambda b, t, e: (b, t, 0)),
                  pl.BlockSpec((1, 1, D), lambda b, t, e: (b, 0, 0)),
                  pl.BlockSpec((2, tm), lambda b, t, e: (0, 0))],
        out_specs=pl.BlockSpec((1, tm, D), lambda b, t, e: (b, t, 0)),
        out_shape=jax.ShapeDtypeStruct((B, T, D), F32),
        scratch_shapes=[pltpu.VMEM((D, tm), F32),
                        pltpu.VMEM((PEER_HEADS, PEER_N_KEYS, tm), F32),
                        pltpu.VMEM((PEER_HEADS, PEER_N_KEYS, tm), F32),
                        pltpu.VMEM((PEER_HEADS, PEER_N_KEYS, tm), BF16),
                        pltpu.VMEM((PEER_HEADS, PEER_N_KEYS, tm), BF16),
                        pltpu.VMEM((2, sb, tm), BF16)],
        compiler_params=_cparams(("parallel", "parallel", "arbitrary")),
        name="peer",
    )(h2, s_t, u, vt, x1, g2, gelu_k)


_Q_HEAD_ORDER = (0, 4, 1, 5, 2, 6, 3, 7)


def _block_diag_ones(width, seg):
    i = jnp.arange(width)
    return (i[:, None] // seg == i[None, :] // seg).astype(BF16)


def _rope_tables(T):
    rows = T // GRID_W
    nf = ATTN_DH // 4
    row = jnp.broadcast_to(jnp.arange(rows)[:, None], (rows, GRID_W)).reshape(T)
    col = jnp.broadcast_to(jnp.arange(GRID_W)[None, :], (rows, GRID_W)).reshape(T)
    inv = ROPE_BASE ** (-jnp.arange(nf, dtype=F32) / nf)
    ang = jnp.concatenate([row[:, None].astype(F32) * inv, col[:, None].astype(F32) * inv], axis=-1)
    cos, sin = jnp.cos(ang), jnp.sin(ang)
    cos_t = jnp.concatenate([cos, cos] * ATTN_KV_HEADS, axis=-1)
    sin_t = jnp.concatenate([-sin, sin] * ATTN_KV_HEADS, axis=-1)
    return cos_t, sin_t


def _layer(x, ctx, mod, norm_mix_g, norm_ffn_g, w_in, b_mgates, mlstm_norm_g, attn_q_norm_g,
           attn_k_norm_g, attn_sink, w_branch_m, w_branch_a, w_out, peer_w_query, peer_sub_keys,
           peer_u, peer_v, cos_t, sin_t):
    B, T, D = x.shape
    n_ctx = ctx.shape[1]
    mods = jnp.split(mod, 6, axis=-1)
    lat = [m[0:B].reshape(B, 1, D) for m in mods]
    cxm = [jnp.broadcast_to(m[B:B + 1].reshape(1, 1, D), (B, 1, D)) for m in mods]
    sh1, sc1, g1, sh2, sc2, g2 = lat

    W = MLSTM_W
    o0 = 4 * W
    o1 = o0 + 2 * 2 * MLSTM_HEADS
    w_mq, w_mk, w_mv, w_mo = (w_in[:, i * W:(i + 1) * W] for i in range(4))
    w_g16 = w_in[:, o0:o1]
    w_aq = w_in[:, o1:o1 + ATTN_Q_W].reshape(D, ATTN_Q_HEADS, ATTN_DH)
    w_aq = w_aq[:, jnp.array(_Q_HEAD_ORDER), :].reshape(D, ATTN_Q_W)
    o2 = o1 + ATTN_Q_W
    w_ak = w_in[:, o2:o2 + ATTN_KV_W]
    w_av = w_in[:, o2 + ATTN_KV_W:o2 + 2 * ATTN_KV_W]
    w_gates = w_in[:, o2 + 2 * ATTN_KV_W:]
    w_g128 = jnp.pad(w_g16, ((0, 0), (0, GATE_LANES - w_g16.shape[1])))
    w_pack = jnp.concatenate([w_mq, w_mk, w_mv, w_aq, w_ak, w_av, w_g128], axis=1).astype(BF16)
    w_merge_in = jnp.concatenate([w_mo, w_gates], axis=1).astype(BF16)
    bg = jnp.pad(b_mgates, (0, GATE_LANES - b_mgates.shape[0])).reshape(1, GATE_LANES)
    qg = jnp.tile(attn_q_norm_g, ATTN_Q_HEADS).reshape(1, ATTN_Q_W)
    kg = jnp.tile(attn_k_norm_g, ATTN_KV_HEADS).reshape(1, ATTN_KV_W)
    bdq = _block_diag_ones(ATTN_Q_W, ATTN_DH)
    bdk = _block_diag_ones(ATTN_KV_W, ATTN_DH)
    bdm = _block_diag_ones(MLSTM_W, MLSTM_DH)
    gmix = norm_mix_g.reshape(1, D)

    tm_lat = min(512, T)
    mq, mk, mv, mg, aq, ak, av = _proj(x, sh1, sc1, gmix, w_pack, bg, qg, kg, bdq, bdk,
                                       cos_t, sin_t, rope=True, tm=tm_lat)
    _, mkc, mvc, mgc, _, akc, avc = _proj(ctx, cxm[0], cxm[1], gmix, w_pack, bg, qg, kg, bdq, bdk,
                                          cos_t, sin_t, rope=False, tm=n_ctx)

    hf, hb = _mlstm(mq, mkc, mvc, mgc, mk, mv, mg)

    sink8 = jnp.broadcast_to(attn_sink[jnp.array(_Q_HEAD_ORDER)].astype(F32)[:, None], (8, 128))
    a = _attn(aq, ak, av, akc, avc, sink8)

    wba = w_branch_a.reshape(ATTN_Q_HEADS, ATTN_DH, D)[jnp.array(_Q_HEAD_ORDER)]
    wba = wba.reshape(ATTN_Q_W, D).astype(BF16)
    x1 = _merge(x, sh1, sc1, gmix, g1, hf, hb, a, w_merge_in, mlstm_norm_g.reshape(1, MLSTM_W),
                bdm, w_branch_m.astype(BF16), wba, w_out.astype(BF16), tm=min(256, T))

    sk = peer_sub_keys.reshape(2 * PEER_HEADS, PEER_N_KEYS, PEER_HALF).astype(BF16)
    h2, s_t = _pquery(x1, sh2, sc2, norm_ffn_g.reshape(1, D), peer_w_query.astype(BF16), sk,
                      tm=min(512, T))
    u = peer_u.astype(BF16)
    vt = peer_v.T.astype(BF16)
    return _peer(h2, s_t, u, vt, x1, g2, tm=min(512, T), eb=2048, sb=512)


def kernel(x, c, ctx, c_ctx, w_ada, b_ada, norm_mix_g, norm_ffn_g, w_in, b_mgates, mlstm_norm_g,
           attn_q_norm_g, attn_k_norm_g, attn_sink, w_branch_m, w_branch_a, w_out, peer_w_query,
           peer_sub_keys, peer_u, peer_v):
    B, T, D = x.shape
    depth = w_ada.shape[0]
    assert depth == 1, "context-stream update for deeper stacks is not implemented"
    cos_t, sin_t = _rope_tables(T)
    cond8 = jnp.zeros((8, D), F32).at[0:B].set(c).at[B].set(c_ctx)
    for layer in range(depth):
        mod = _adaln(cond8, w_ada[layer], b_ada[layer])
        x = _layer(x, ctx, mod, norm_mix_g[layer], norm_ffn_g[layer], w_in[layer], b_mgates[layer],
                   mlstm_norm_g[layer], attn_q_norm_g[layer], attn_k_norm_g[layer], attn_sink[layer],
                   w_branch_m[layer], w_branch_a[layer], w_out[layer], peer_w_query[layer],
                   peer_sub_keys[layer], peer_u[layer], peer_v[layer], cos_t, sin_t)
    return x
```

```python
import functools
import math

import jax
import jax.numpy as jnp
from jax import lax
from jax.experimental import pallas as pl
from jax.experimental.pallas import tpu as pltpu

F32 = jnp.float32
BF16 = jnp.bfloat16

EPS = 1e-6
GRID_W = 64
ROPE_BASE = 10000.0

MLSTM_HEADS = 4
MLSTM_DH = 128
MLSTM_W = MLSTM_HEADS * MLSTM_DH
MLSTM_CHUNK = 128
GATE_LANES = 128

ATTN_Q_HEADS = 8
ATTN_KV_HEADS = 2
ATTN_DH = 64
ATTN_Q_W = ATTN_Q_HEADS * ATTN_DH
ATTN_KV_W = ATTN_KV_HEADS * ATTN_DH
ATTN_BLOCK = 128

PEER_HEADS = 8
PEER_N_KEYS = 128
PEER_HALF = 128
PEER_TOPK = 16

VMEM_LIMIT_BYTES = 56 * 1024 * 1024

NEG_INF = float("-inf")
POS_INF = float("inf")


def _cparams(sem):
    return pltpu.CompilerParams(dimension_semantics=sem, vmem_limit_bytes=VMEM_LIMIT_BYTES)


def _dot(a, b):
    return jnp.dot(a, b, preferred_element_type=F32)


def _dot_nt(a, b):
    return lax.dot_general(a, b, (((1,), (1,)), ((), ())), preferred_element_type=F32)


def _dot_tn(a, b):
    return lax.dot_general(a, b, (((0,), (0,)), ((), ())), preferred_element_type=F32)


def _sigmoid(x):
    return 1.0 / (1.0 + jnp.exp(-x))


def _segsum(x2, bd):
    hi = x2.astype(BF16)
    lo = (x2 - hi.astype(F32)).astype(BF16)
    return _dot(hi, bd) + _dot(lo, bd)


def _modulated(x, g, shift, scale):
    ms = jnp.mean(x * x, axis=-1, keepdims=True)
    return (x * lax.rsqrt(ms + EPS)) * g * (1.0 + scale) + shift


def _adaln_kernel(c_ref, w_ref, b_ref, o_ref):
    c = c_ref[...]
    s = c * _sigmoid(c)
    o_ref[...] = jnp.dot(s, w_ref[...], preferred_element_type=F32,
                         precision=lax.Precision.HIGHEST) + b_ref[...]


def _adaln(cond8, w, b):
    d = cond8.shape[1]
    n = w.shape[1]
    bn = n // 6
    return pl.pallas_call(
        _adaln_kernel,
        grid=(n // bn,),
        in_specs=[pl.BlockSpec((8, d), lambda j: (0, 0)),
                  pl.BlockSpec((d, bn), lambda j: (0, j)),
                  pl.BlockSpec((1, bn), lambda j: (0, j))],
        out_specs=pl.BlockSpec((8, bn), lambda j: (0, j)),
        out_shape=jax.ShapeDtypeStruct((8, n), F32),
        compiler_params=_cparams(("arbitrary",)),
        name="adaln",
    )(cond8, w, b.reshape(1, n))


_C_MQ, _C_MK, _C_MV = 0, 512, 1024
_C_AQ, _C_AK, _C_AV, _C_MG = 1536, 2048, 2176, 2304
_C_END = 2432


def _rope(t, cos_t, sin_t, width):
    lane = lax.broadcasted_iota(jnp.int32, t.shape, 1)
    lower = (lane % ATTN_DH) < (ATTN_DH // 2)
    swapped = jnp.where(lower, pltpu.roll(t, width - ATTN_DH // 2, 1),
                        pltpu.roll(t, ATTN_DH // 2, 1))
    return t * cos_t + swapped * sin_t


def _proj_kernel(x_ref, sh_ref, sc_ref, g_ref, w_ref, bg_ref, qg_ref, kg_ref,
                 bdq_ref, bdk_ref, cos_ref, sin_ref,
                 mq_ref, mk_ref, mv_ref, mg_ref, aq_ref, ak_ref, av_ref, *, rope):
    x = x_ref[0]
    h = _modulated(x, g_ref[...], sh_ref[0], sc_ref[0]).astype(BF16)

    p = _dot(h, w_ref[:, _C_MQ:_C_AQ])
    mq_ref[0] = p[:, 0:512].astype(BF16)
    mk_ref[0] = (p[:, 512:1024] * (MLSTM_DH ** -0.5)).astype(BF16)
    mv_ref[0] = p[:, 1024:1536].astype(BF16)

    pq = _dot(h, w_ref[:, _C_AQ:_C_AK])
    ssq = _segsum(pq * pq, bdq_ref[...])
    q = pq * lax.rsqrt(ssq * (1.0 / ATTN_DH) + EPS) * qg_ref[...]
    if rope:
        cos_t = cos_ref[...]
        sin_t = sin_ref[...]
        q = _rope(q, jnp.concatenate([cos_t] * 4, axis=1), jnp.concatenate([sin_t] * 4, axis=1),
                  ATTN_Q_W)
    aq_ref[0] = (q * (ATTN_DH ** -0.5)).astype(BF16)

    pk = _dot(h, w_ref[:, _C_AK:_C_AV])
    ssk = _segsum(pk * pk, bdk_ref[...])
    k = pk * lax.rsqrt(ssk * (1.0 / ATTN_DH) + EPS) * kg_ref[...]
    if rope:
        k = _rope(k, cos_ref[...], sin_ref[...], ATTN_KV_W)
    ak_ref[0] = k.astype(BF16)

    av_ref[0] = _dot(h, w_ref[:, _C_AV:_C_MG]).astype(BF16)

    z = _dot(h, w_ref[:, _C_MG:_C_END]) + bg_ref[...]
    lane = lax.broadcasted_iota(jnp.int32, z.shape, 1)
    is_forget = (lane % 8) >= MLSTM_HEADS
    logsig = jnp.minimum(z, 0.0) - jnp.log1p(jnp.exp(-jnp.abs(z)))
    mg_ref[0] = jnp.where(is_forget, logsig, z)


def _proj(x, shift, scale, g, w_pack, bg, qg, kg, bdq, bdk, cos_t, sin_t, *, rope, tm):
    B, T, D = x.shape
    nt = T // tm
    full = lambda shape: pl.BlockSpec(shape, lambda b, t: (0,) * len(shape))
    tok = lambda w: pl.BlockSpec((1, tm, w), lambda b, t: (b, t, 0))
    outs = [(MLSTM_W, BF16), (MLSTM_W, BF16), (MLSTM_W, BF16), (GATE_LANES, F32),
            (ATTN_Q_W, BF16), (ATTN_KV_W, BF16), (ATTN_KV_W, BF16)]
    return pl.pallas_call(
        functools.partial(_proj_kernel, rope=rope),
        grid=(B, nt),
        in_specs=[tok(D),
                  pl.BlockSpec((1, 1, D), lambda b, t: (b, 0, 0)),
                  pl.BlockSpec((1, 1, D), lambda b, t: (b, 0, 0)),
                  full((1, D)), full(w_pack.shape), full((1, GATE_LANES)),
                  full((1, ATTN_Q_W)), full((1, ATTN_KV_W)),
                  full((ATTN_Q_W, ATTN_Q_W)), full((ATTN_KV_W, ATTN_KV_W)),
                  pl.BlockSpec((tm, ATTN_KV_W), lambda b, t: (t, 0)),
                  pl.BlockSpec((tm, ATTN_KV_W), lambda b, t: (t, 0))],
        out_specs=[tok(w) for w, _ in outs],
        out_shape=[jax.ShapeDtypeStruct((B, T, w), dt) for w, dt in outs],
        compiler_params=_cparams(("parallel", "parallel")),
        name="proj_rope" if rope else "proj_ctx",
    )(x, shift, scale, g, w_pack, bg, qg, kg, bdq, bdk, cos_t, sin_t)


def _mlstm_kernel(qf_ref, qb_ref, kcf_ref, kcb_ref, klf_ref, klb_ref, vcf_ref, vcb_ref,
                  vlf_ref, vlb_ref, gcf_ref, gcb_ref, glf_ref, glb_ref,
                  hf_ref, hb_ref, c_ref, n_ref, m_ref, *, n_ctx_chunks):
    s = pl.program_id(0)
    L = MLSTM_CHUNK
    n_batch = qf_ref.shape[0]
    in_ctx = s < n_ctx_chunks

    @pl.when(s == 0)
    def _():
        c_ref[...] = jnp.zeros_like(c_ref)
        n_ref[...] = jnp.zeros_like(n_ref)
        m_ref[...] = jnp.zeros_like(m_ref)

    row = lax.broadcasted_iota(jnp.int32, (L, L), 0)
    col = lax.broadcasted_iota(jnp.int32, (L, L), 1)
    tri_f = (col <= row).astype(F32)
    pick = lambda c_ref_, l_ref_, bb: jnp.where(in_ctx, c_ref_[bb], l_ref_[bb])


    dirs = {}
    for bb in range(n_batch):
        for d in range(2):
            q_all = (qf_ref, qb_ref)[d][bb]
            k_all = pick((kcf_ref, kcb_ref)[d], (klf_ref, klb_ref)[d], bb)
            v_all = pick((vcf_ref, vcb_ref)[d], (vlf_ref, vlb_ref)[d], bb)
            g = pick((gcf_ref, gcb_ref)[d], (glf_ref, glb_ref)[d], bb)
            cum_f = jnp.dot(tri_f, g, preferred_element_type=F32, precision=lax.Precision.HIGHEST)
            if d == 0:
                b_all, seen, last = cum_f, col <= row, L - 1
            else:
                b_all, seen, last = cum_f[L - 1:L, :] - cum_f + g, col >= row, 0
            dirs[bb, d] = (q_all, k_all, v_all, g, b_all, g.T, b_all.T, seen, last)

    heads = [(bb, d, hh) for bb in range(n_batch) for d in range(2) for hh in range(MLSTM_HEADS)]
    hsl = lambda hh: slice(hh * MLSTM_DH, (hh + 1) * MLSTM_DH)
    sidx = lambda bb, d, hh: (bb * 2 + d) * MLSTM_HEADS + hh
    gate_cols = lambda d, hh: (d * 2 * MLSTM_HEADS + hh, d * 2 * MLSTM_HEADS + hh + MLSTM_HEADS)

    qk, qc, state = {}, {}, {}
    for key in heads:
        bb, d, hh = key
        q_all, k_all = dirs[bb, d][0], dirs[bb, d][1]
        r = sidx(*key)
        c_s = c_ref[r]
        state[key] = (c_s, n_ref[r], m_ref[r][:, 0:1])
        qk[key] = _dot_nt(q_all[:, hsl(hh)], k_all[:, hsl(hh)])
        qc[key] = _dot(q_all[:, hsl(hh)], c_s.astype(BF16))

    summ = {}
    for key in heads:
        bb, d, hh = key
        _, k_all, _, g, b_all, _, _, _, last = dirs[bb, d]
        ci, cf = gate_cols(d, hh)
        a = b_all[last:last + 1, cf:cf + 1]
        w_log = a - b_all[:, cf:cf + 1] + g[:, ci:ci + 1]
        m_loc = jnp.max(w_log, axis=0, keepdims=True)
        kw = k_all[:, hsl(hh)].astype(F32) * jnp.exp(w_log - m_loc)
        summ[key] = (a, m_loc, kw)
    for key in heads:
        bb, d, hh = key
        v_all = dirs[bb, d][2]
        r = sidx(*key)
        c_s, n_s, m_s = state[key]
        a, m_loc, kw = summ[key]
        c_loc = _dot_tn(kw.astype(BF16), v_all[:, hsl(hh)])
        n_loc = jnp.sum(kw, axis=0, keepdims=True)
        m_new = jnp.maximum(a + m_s, m_loc)
        d_old = jnp.exp(a + m_s - m_new)
        d_new = jnp.exp(m_loc - m_new)
        c_ref[r] = d_old * c_s + d_new * c_loc
        n_ref[r] = d_old * n_s + d_new * n_loc
        m_ref[r] = jnp.broadcast_to(m_new, (1, MLSTM_DH))

    rel, m_rel, sc, inter, den = {}, {}, {}, {}, {}
    for key in heads:
        bb, d, hh = key
        _, _, _, _, _, g_t, b_t, seen, _ = dirs[bb, d]
        ci, cf = gate_cols(d, hh)
        rel[key] = jnp.where(seen, g_t[ci:ci + 1, :] - b_t[cf:cf + 1, :], NEG_INF)
        m_rel[key] = jnp.maximum(state[key][2], jnp.max(rel[key], axis=1, keepdims=True))
    for key in heads:
        m_s = state[key][2]
        sc[key] = qk[key] * jnp.exp(rel[key] - m_rel[key])
        inter[key] = jnp.broadcast_to(jnp.exp(m_s - m_rel[key]), (L, MLSTM_DH))
    for key in heads:
        bb, d, hh = key
        qn = dirs[bb, d][0][:, hsl(hh)].astype(F32) * state[key][1]
        den[key] = jnp.sum(sc[key] + inter[key] * qn, axis=1, keepdims=True)
    for key in heads:
        bb, d, hh = key
        _, _, v_all, _, b_all, _, _, _, _ = dirs[bb, d]
        out_ref = (hf_ref, hb_ref)[d]
        cf = gate_cols(d, hh)[1]
        num = _dot(sc[key].astype(BF16), v_all[:, hsl(hh)]) + inter[key] * qc[key]
        floor = jnp.exp(-(b_all[:, cf:cf + 1] + m_rel[key]))
        out_ref[bb, :, hsl(hh)] = num * (1.0 / jnp.maximum(jnp.abs(den[key]), floor))


def _mlstm(q_lat, k_ctx, v_ctx, g_ctx, k_lat, v_lat, g_lat):
    B, T, _ = q_lat.shape
    L = MLSTM_CHUNK
    nl = T // L
    nc = k_ctx.shape[1] // L
    steps = nc + nl

    fwd_c = lambda s: (0, jnp.minimum(s, nc - 1), 0)
    bwd_c = lambda s: (0, jnp.maximum(nc - 1 - s, 0), 0)
    fwd_l = lambda s: (0, jnp.maximum(s - nc, 0), 0)
    bwd_l = lambda s: (0, jnp.minimum(steps - 1 - s, nl - 1), 0)

    blk = lambda w, im: pl.BlockSpec((B, L, w), im)
    pair = lambda w, f, g: [blk(w, f), blk(w, g)]
    nheads = B * 2 * MLSTM_HEADS
    return pl.pallas_call(
        functools.partial(_mlstm_kernel, n_ctx_chunks=nc),
        grid=(steps,),
        in_specs=(pair(MLSTM_W, fwd_l, bwd_l)
                  + pair(MLSTM_W, fwd_c, bwd_c) + pair(MLSTM_W, fwd_l, bwd_l)
                  + pair(MLSTM_W, fwd_c, bwd_c) + pair(MLSTM_W, fwd_l, bwd_l)
                  + pair(GATE_LANES, fwd_c, bwd_c) + pair(GATE_LANES, fwd_l, bwd_l)),
        out_specs=pair(MLSTM_W, fwd_l, bwd_l),
        out_shape=[jax.ShapeDtypeStruct((B, T, MLSTM_W), F32)] * 2,
        scratch_shapes=[pltpu.VMEM((nheads, MLSTM_DH, MLSTM_DH), F32),
                        pltpu.VMEM((nheads, 1, MLSTM_DH), F32),
                        pltpu.VMEM((nheads, 1, MLSTM_DH), F32)],
        compiler_params=_cparams(("arbitrary",)),
        name="mlstm",
    )(q_lat, q_lat, k_ctx, k_ctx, k_lat, k_lat, v_ctx, v_ctx, v_lat, v_lat,
      g_ctx, g_ctx, g_lat, g_lat)


def _attn_kernel(q_ref, kp_ref, kc_ref, kn_ref, vp_ref, vc_ref, vn_ref, kx_ref, vx_ref,
                 sink_ref, o_ref, *, nb):
    n = pl.program_id(0)
    blk = ATTN_BLOCK
    n_batch = q_ref.shape[0]

    qi = lax.broadcasted_iota(jnp.int32, (blk, blk), 0)
    kj = lax.broadcasted_iota(jnp.int32, (blk, blk), 1)
    ok_prev = (kj >= qi) & (n > 0)
    ok_next = (kj <= qi) & (n < nb - 1)
    lane = lax.broadcasted_iota(jnp.int32, (blk, 2 * ATTN_DH), 1)
    low = lane < ATTN_DH
    n_pairs = ATTN_Q_W // (2 * ATTN_DH)
    n_heads = 2 * n_pairs

    def fold(op, cols):
        groups = [c[:, i:i + blk] for c in cols for i in range(0, c.shape[1], blk)]
        return functools.reduce(op, groups)

    s_all, vcat = [], []
    for bb in range(n_batch):
        kcat = jnp.concatenate([kp_ref[bb], kc_ref[bb], kn_ref[bb], kx_ref[bb]], axis=0)
        vcat.append(jnp.concatenate([vp_ref[bb], vc_ref[bb], vn_ref[bb], vx_ref[bb]], axis=0))
        pieces = []
        for c in range(n_pairs):
            qc = q_ref[bb, :, c * 128:(c + 1) * 128]
            pieces += [jnp.where(low, qc, jnp.zeros_like(qc)), jnp.where(low, jnp.zeros_like(qc), qc)]
        s_all.append(_dot_nt(jnp.concatenate(pieces, axis=0), kcat))

    parts, maxes, probs, dens = {}, {}, {}, {}
    keys = [(bb, j) for bb in range(n_batch) for j in range(n_heads)]
    for bb, j in keys:
        sc = s_all[bb][j * blk:(j + 1) * blk]
        parts[bb, j] = [jnp.where(ok_prev, sc[:, 0:blk], NEG_INF), sc[:, blk:2 * blk],
                        jnp.where(ok_next, sc[:, 2 * blk:3 * blk], NEG_INF), sc[:, 3 * blk:]]
        row_max = jnp.max(fold(jnp.maximum, parts[bb, j]), axis=1, keepdims=True)
        maxes[bb, j] = jnp.maximum(row_max, sink_ref[j:j + 1, 0:1])
    for bb, j in keys:
        exps = [jnp.exp(part - maxes[bb, j]) for part in parts[bb, j]]
        row_sum = jnp.sum(fold(jnp.add, exps), axis=1, keepdims=True)
        dens[bb, j] = row_sum + jnp.exp(sink_ref[j:j + 1, 0:1] - maxes[bb, j])
        probs[bb, j] = jnp.concatenate([p.astype(BF16) for p in exps], axis=1)
    for bb in range(n_batch):
        o_all = _dot(jnp.concatenate([probs[bb, j] for j in range(n_heads)], axis=0), vcat[bb])
        for c in range(n_pairs):
            o_lo = o_all[(2 * c) * blk:(2 * c + 1) * blk] / dens[bb, 2 * c]
            o_hi = o_all[(2 * c + 1) * blk:(2 * c + 2) * blk] / dens[bb, 2 * c + 1]
            o_ref[bb, :, c * 128:(c + 1) * 128] = jnp.where(low, o_lo, o_hi).astype(BF16)


def _attn(aq, ak, av, akx, avx, sink8):
    B, T, _ = aq.shape
    blk = ATTN_BLOCK
    nb = T // blk
    ctx = akx.shape[1]
    kv = lambda im: pl.BlockSpec((B, blk, ATTN_KV_W), im)
    prev = lambda n: (0, jnp.maximum(n - 1, 0), 0)
    cur = lambda n: (0, n, 0)
    nxt = lambda n: (0, jnp.minimum(n + 1, nb - 1), 0)
    cx = pl.BlockSpec((B, ctx, ATTN_KV_W), lambda n: (0, 0, 0))
    return pl.pallas_call(
        functools.partial(_attn_kernel, nb=nb),
        grid=(nb,),
        in_specs=[pl.BlockSpec((B, blk, ATTN_Q_W), cur),
                  kv(prev), kv(cur), kv(nxt), kv(prev), kv(cur), kv(nxt), cx, cx,
                  pl.BlockSpec((8, 128), lambda n: (0, 0))],
        out_specs=pl.BlockSpec((B, blk, ATTN_Q_W), cur),
        out_shape=jax.ShapeDtypeStruct((B, T, ATTN_Q_W), BF16),
        compiler_params=_cparams(("parallel",)),
        name="attn",
    )(aq, ak, ak, ak, av, av, av, akx, avx, sink8)


def _merge_kernel(x_ref, sh_ref, sc_ref, g_ref, g1_ref, hf_ref, hb_ref, a_ref,
                  wg_ref, ng_ref, bdm_ref, wbm_ref, wba_ref, wo_ref, o_ref):
    x = x_ref[0]
    h = _modulated(x, g_ref[...], sh_ref[0], sc_ref[0]).astype(BF16)
    pg = _dot(h, wg_ref[...])
    d = x.shape[1]
    hs = hf_ref[0] + hb_ref[0]
    ss = _segsum(hs * hs, bdm_ref[...])
    m = hs * lax.rsqrt(ss * (1.0 / MLSTM_DH) + EPS) * ng_ref[...] * _sigmoid(pg[:, 0:MLSTM_W])
    mm = _dot(m.astype(BF16), wbm_ref[...])
    aa = _dot(a_ref[0], wba_ref[...])
    y = _sigmoid(pg[:, MLSTM_W:MLSTM_W + d]) * mm + _sigmoid(pg[:, MLSTM_W + d:]) * aa
    o_ref[0] = x + g1_ref[0] * _dot(y.astype(BF16), wo_ref[...])


def _merge(x, shift, scale, g, g1, hf, hb, a, wg, ng, bdm, wbm, wba, wo, *, tm):
    B, T, D = x.shape
    full = lambda arr: pl.BlockSpec(arr.shape, lambda b, t: (0,) * arr.ndim)
    tok = lambda w: pl.BlockSpec((1, tm, w), lambda b, t: (b, t, 0))
    perb = pl.BlockSpec((1, 1, D), lambda b, t: (b, 0, 0))
    return pl.pallas_call(
        _merge_kernel,
        grid=(B, T // tm),
        in_specs=[tok(D), perb, perb, full(g), perb, tok(MLSTM_W), tok(MLSTM_W), tok(ATTN_Q_W),
                  full(wg), full(ng), full(bdm), full(wbm), full(wba), full(wo)],
        out_specs=tok(D),
        out_shape=jax.ShapeDtypeStruct((B, T, D), F32),
        compiler_params=_cparams(("parallel", "parallel")),
        name="merge",
    )(x, shift, scale, g, g1, hf, hb, a, wg, ng, bdm, wbm, wba, wo)


def _pquery_kernel(x_ref, sh_ref, sc_ref, g_ref, wq_ref, sk_ref, h_ref, s_ref):
    x = x_ref[0]
    h = _modulated(x, g_ref[...], sh_ref[0], sc_ref[0]).astype(BF16)
    h_ref[0] = h
    q = _dot(h, wq_ref[...]).astype(BF16)
    for j in range(2 * PEER_HEADS):
        s_ref[0, j] = _dot_nt(sk_ref[j], q[:, j * PEER_HALF:(j + 1) * PEER_HALF])


def _pquery(x, shift, scale, g, wq, sk, *, tm):
    B, T, D = x.shape
    nj = 2 * PEER_HEADS
    full = lambda arr: pl.BlockSpec(arr.shape, lambda b, t: (0,) * arr.ndim)
    perb = pl.BlockSpec((1, 1, D), lambda b, t: (b, 0, 0))
    return pl.pallas_call(
        _pquery_kernel,
        grid=(B, T // tm),
        in_specs=[pl.BlockSpec((1, tm, D), lambda b, t: (b, t, 0)), perb, perb,
                  full(g), full(wq), full(sk)],
        out_specs=[pl.BlockSpec((1, tm, D), lambda b, t: (b, t, 0)),
                   pl.BlockSpec((1, nj, PEER_N_KEYS, tm), lambda b, t: (b, 0, 0, t))],
        out_shape=[jax.ShapeDtypeStruct((B, T, D), BF16),
                   jax.ShapeDtypeStruct((B, nj, PEER_N_KEYS, T), F32)],
        compiler_params=_cparams(("parallel", "parallel")),
        name="pquery",
    )(x, shift, scale, g, wq, sk)


def _top_values(val, k):
    tops = []
    for _ in range(k):
        m = jnp.max(val, axis=0, keepdims=True)
        tops.append(m)
        val = jnp.where(val == m, NEG_INF, val)
    return tops


_GELU_K0 = -2.0 * math.sqrt(2.0 / math.pi) * math.log2(math.e)
_GELU_K1 = 0.044715 * _GELU_K0


def _gelu_tanh(x, k0, k1):
    one = jnp.asarray(1.0, x.dtype)
    return x / (one + jnp.exp2(x * (x * x * k1 + k0)))


BF16_ROWS = 16


def _rows_bf16(row, tm):
    tile = jnp.broadcast_to(row, (BF16_ROWS, tm)).astype(BF16)
    return jnp.concatenate([tile] * (PEER_N_KEYS // BF16_ROWS), axis=0)


def _row_tile_bf16(ref, h, i, tm):
    return _rows_bf16(ref[h, pl.ds(i, 1), :], tm)


def _batcher_pairs(n):
    pairs = []
    p = 1
    while p < n:
        k = p
        while k >= 1:
            for j in range(k % p, n - k, 2 * k):
                for i in range(min(k, n - j - k)):
                    if (i + j) // (2 * p) == (i + j + k) // (2 * p):
                        pairs.append((i + j, i + j + k))
            k //= 2
        p *= 2
    return pairs


def _compare_exchange(rows, i, j):
    rows[i], rows[j] = jnp.maximum(rows[i], rows[j]), jnp.minimum(rows[i], rows[j])


def _sorted_top(x, k):
    sub = 8
    assert x.shape[0] == sub * k and k & (k - 1) == 0
    rows = [x[sub * r:sub * (r + 1), :] for r in range(k)]
    for i, j in _batcher_pairs(k):
        _compare_exchange(rows, i, j)
    shift = sub // 2
    while shift >= 1:
        other = [pltpu.roll(r, shift, 0) for r in rows]
        rows = [jnp.maximum(rows[r], other[k - 1 - r]) for r in range(k)]
        d = k // 2
        while d >= 1:
            for i in range(k):
                if i & d == 0:
                    _compare_exchange(rows, i, i + d)
            d //= 2
        shift //= 2
    return rows


def _peer_router(s_ref, rank_ref, cw_ref, cnt_ref, e2_ref):
    k = PEER_TOPK
    for h in range(PEER_HEADS):
        s1 = s_ref[0, 2 * h]
        s2 = s_ref[0, 2 * h + 1]
        a = [r[0:1, :] for r in _sorted_top(s1, k)]
        b = [r[0:1, :] for r in _sorted_top(s2, k)]
        cand = [a[p] + b[q] for p in range(k) for q in range(k) if (p + 1) * (q + 1) <= k]
        cv = jnp.concatenate(cand, axis=0)
        thr = _top_values(cv, k)[k - 1]
        top = a[0] + b[0]
        z = jnp.sum(jnp.where(cv >= thr, jnp.exp(cv - top), 0.0), axis=0, keepdims=True)
        rank = jnp.zeros(s1.shape, F32)
        cnt = jnp.zeros(s2.shape, F32)
        for p in range(k):
            rank = jnp.where(a[p] > s1, float(p + 1), rank)
            cnt = jnp.where(a[p] + s2 >= thr, float(p + 1), cnt)
        rank_ref[h] = rank
        cw_ref[h] = jnp.exp(s1 - a[0]) / z
        cnt_ref[h] = cnt.astype(BF16)
        e2_ref[h] = jnp.exp(s2 - b[0]).astype(BF16)


def _peer_kernel(h_ref, s_ref, u_ref, vt_ref, x_ref, g2_ref, gk_ref, o_ref,
                 acc_ref, rank_ref, cw_ref, cnt_ref, e2_ref, wt_ref, *, eb, sb):
    e = pl.program_id(2)
    ne = pl.num_programs(2)
    tm = h_ref.shape[1]

    @pl.when(e == 0)
    def _():
        acc_ref[...] = jnp.zeros_like(acc_ref)
        _peer_router(s_ref, rank_ref, cw_ref, cnt_ref, e2_ref)

    hq = h_ref[0]
    n_i = sb // PEER_N_KEYS
    n_sub = eb // sb
    scores = lambda j: _dot_nt(u_ref[j * sb:(j + 1) * sb, :], hq)
    zero = jnp.zeros((), BF16)
    gk0 = _rows_bf16(gk_ref[0:1, :], tm)
    gk1 = _rows_bf16(gk_ref[1:2, :], tm)
    def output_product(j):
        acc_ref[...] += _dot(vt_ref[:, j * sb:(j + 1) * sb], wt_ref[j % 2])

    acts = {j: scores(j) for j in range(min(2, n_sub))}
    for j in range(n_sub):
        act = acts.pop(j)
        for ii in range(n_i):
            if ii == n_i // 2 and j > 0:
                output_product(j - 1)
            i = e * (eb // PEER_N_KEYS) + j * n_i + ii
            w = None
            for h in range(PEER_HEADS):
                rank_i = _row_tile_bf16(rank_ref, h, i, tm)
                cw_i = _row_tile_bf16(cw_ref, h, i, tm)
                term = jnp.where(rank_i < cnt_ref[h], e2_ref[h], zero) * cw_i
                w = term if w is None else w + term
            act_i = act[ii * PEER_N_KEYS:(ii + 1) * PEER_N_KEYS].astype(BF16)
            wt_ref[j % 2, ii * PEER_N_KEYS:(ii + 1) * PEER_N_KEYS, :] = w * _gelu_tanh(act_i, gk0, gk1)
        if j + 2 < n_sub:
            acts[j + 2] = scores(j + 2)
    output_product(n_sub - 1)

    @pl.when(e == ne - 1)
    def _():
        o_ref[0] = x_ref[0] + g2_ref[0] * acc_ref[...].T


def _peer(h2, s_t, u, vt, x1, g2, *, tm, eb, sb):
    B, T, D = x1.shape
    ne = u.shape[0] // eb
    nj = 2 * PEER_HEADS
    gelu_k = jnp.broadcast_to(jnp.array([[_GELU_K0], [_GELU_K1]], F32), (2, tm))
    return pl.pallas_call(
        functools.partial(_peer_kernel, eb=eb, sb=sb),
        grid=(B, T // tm, ne),
        in_specs=[pl.BlockSpec((1, tm, D), lambda b, t, e: (b, t, 0)),
                  pl.BlockSpec((1, nj, PEER_N_KEYS, tm), lambda b, t, e: (b, 0, 0, t)),
                  pl.BlockSpec((eb, D), lambda b, t, e: (e, 0)),
                  pl.BlockSpec((D, eb), lambda b, t, e: (0, e)),
                  pl.BlockSpec((1, tm, D), lambda b, t, e: (b, t, 0)),
                  pl.BlockSpec((1, 1, D), lambda b, t, e: (b, 0, 0)),
                  pl.BlockSpec((2, tm), lambda b, t, e: (0, 0))],
        out_specs=pl.BlockSpec((1, tm, D), lambda b, t, e: (b, t, 0)),
        out_shape=jax.ShapeDtypeStruct((B, T, D), F32),
        scratch_shapes=[pltpu.VMEM((D, tm), F32),
                        pltpu.VMEM((PEER_HEADS, PEER_N_KEYS, tm), F32),
                        pltpu.VMEM((PEER_HEADS, PEER_N_KEYS, tm), F32),
                        pltpu.VMEM((PEER_HEADS, PEER_N_KEYS, tm), BF16),
                        pltpu.VMEM((PEER_HEADS, PEER_N_KEYS, tm), BF16),
                        pltpu.VMEM((2, sb, tm), BF16)],
        compiler_params=_cparams(("parallel", "parallel", "arbitrary")),
        name="peer",
    )(h2, s_t, u, vt, x1, g2, gelu_k)


_Q_HEAD_ORDER = (0, 4, 1, 5, 2, 6, 3, 7)


def _block_diag_ones(width, seg):
    i = jnp.arange(width)
    return (i[:, None] // seg == i[None, :] // seg).astype(BF16)


def _rope_tables(T):
    rows = T // GRID_W
    nf = ATTN_DH // 4
    row = jnp.broadcast_to(jnp.arange(rows)[:, None], (rows, GRID_W)).reshape(T)
    col = jnp.broadcast_to(jnp.arange(GRID_W)[None, :], (rows, GRID_W)).reshape(T)
    inv = ROPE_BASE ** (-jnp.arange(nf, dtype=F32) / nf)
    ang = jnp.concatenate([row[:, None].astype(F32) * inv, col[:, None].astype(F32) * inv], axis=-1)
    cos, sin = jnp.cos(ang), jnp.sin(ang)
    cos_t = jnp.concatenate([cos, cos] * ATTN_KV_HEADS, axis=-1)
    sin_t = jnp.concatenate([-sin, sin] * ATTN_KV_HEADS, axis=-1)
    return cos_t, sin_t


def _layer(x, ctx, mod, norm_mix_g, norm_ffn_g, w_in, b_mgates, mlstm_norm_g, attn_q_norm_g,
           attn_k_norm_g, attn_sink, w_branch_m, w_branch_a, w_out, peer_w_query, peer_sub_keys,
           peer_u, peer_v, cos_t, sin_t):
    B, T, D = x.shape
    n_ctx = ctx.shape[1]
    mods = jnp.split(mod, 6, axis=-1)
    lat = [m[0:B].reshape(B, 1, D) for m in mods]
    cxm = [jnp.broadcast_to(m[B:B + 1].reshape(1, 1, D), (B, 1, D)) for m in mods]
    sh1, sc1, g1, sh2, sc2, g2 = lat

    W = MLSTM_W
    o0 = 4 * W
    o1 = o0 + 2 * 2 * MLSTM_HEADS
    w_mq, w_mk, w_mv, w_mo = (w_in[:, i * W:(i + 1) * W] for i in range(4))
    w_g16 = w_in[:, o0:o1]
    w_aq = w_in[:, o1:o1 + ATTN_Q_W].reshape(D, ATTN_Q_HEADS, ATTN_DH)
    w_aq = w_aq[:, jnp.array(_Q_HEAD_ORDER), :].reshape(D, ATTN_Q_W)
    o2 = o1 + ATTN_Q_W
    w_ak = w_in[:, o2:o2 + ATTN_KV_W]
    w_av = w_in[:, o2 + ATTN_KV_W:o2 + 2 * ATTN_KV_W]
    w_gates = w_in[:, o2 + 2 * ATTN_KV_W:]
    w_g128 = jnp.pad(w_g16, ((0, 0), (0, GATE_LANES - w_g16.shape[1])))
    w_pack = jnp.concatenate([w_mq, w_mk, w_mv, w_aq, w_ak, w_av, w_g128], axis=1).astype(BF16)
    w_merge_in = jnp.concatenate([w_mo, w_gates], axis=1).astype(BF16)
    bg = jnp.pad(b_mgates, (0, GATE_LANES - b_mgates.shape[0])).reshape(1, GATE_LANES)
    qg = jnp.tile(attn_q_norm_g, ATTN_Q_HEADS).reshape(1, ATTN_Q_W)
    kg = jnp.tile(attn_k_norm_g, ATTN_KV_HEADS).reshape(1, ATTN_KV_W)
    bdq = _block_diag_ones(ATTN_Q_W, ATTN_DH)
    bdk = _block_diag_ones(ATTN_KV_W, ATTN_DH)
    bdm = _block_diag_ones(MLSTM_W, MLSTM_DH)
    gmix = norm_mix_g.reshape(1, D)

    tm_lat = min(512, T)
    mq, mk, mv, mg, aq, ak, av = _proj(x, sh1, sc1, gmix, w_pack, bg, qg, kg, bdq, bdk,
                                       cos_t, sin_t, rope=True, tm=tm_lat)
    _, mkc, mvc, mgc, _, akc, avc = _proj(ctx, cxm[0], cxm[1], gmix, w_pack, bg, qg, kg, bdq, bdk,
                                          cos_t, sin_t, rope=False, tm=n_ctx)

    hf, hb = _mlstm(mq, mkc, mvc, mgc, mk, mv, mg)

    sink8 = jnp.broadcast_to(attn_sink[jnp.array(_Q_HEAD_ORDER)].astype(F32)[:, None], (8, 128))
    a = _attn(aq, ak, av, akc, avc, sink8)

    wba = w_branch_a.reshape(ATTN_Q_HEADS, ATTN_DH, D)[jnp.array(_Q_HEAD_ORDER)]
    wba = wba.reshape(ATTN_Q_W, D).astype(BF16)
    x1 = _merge(x, sh1, sc1, gmix, g1, hf, hb, a, w_merge_in, mlstm_norm_g.reshape(1, MLSTM_W),
                bdm, w_branch_m.astype(BF16), wba, w_out.astype(BF16), tm=min(256, T))

    sk = peer_sub_keys.reshape(2 * PEER_HEADS, PEER_N_KEYS, PEER_HALF).astype(BF16)
    h2, s_t = _pquery(x1, sh2, sc2, norm_ffn_g.reshape(1, D), peer_w_query.astype(BF16), sk,
                      tm=min(512, T))
    u = peer_u.astype(BF16)
    vt = peer_v.T.astype(BF16)
    return _peer(h2, s_t, u, vt, x1, g2, tm=min(512, T), eb=2048, sb=512)


def kernel(x, c, ctx, c_ctx, w_ada, b_ada, norm_mix_g, norm_ffn_g, w_in, b_mgates, mlstm_norm_g,
           attn_q_norm_g, attn_k_norm_g, attn_sink, w_branch_m, w_branch_a, w_out, peer_w_query,
           peer_sub_keys, peer_u, peer_v):
    B, T, D = x.shape
    depth = w_ada.shape[0]
    assert depth == 1, "context-stream update for deeper stacks is not implemented"
    cos_t, sin_t = _rope_tables(T)
    cond8 = jnp.zeros((8, D), F32).at[0:B].set(c).at[B].set(c_ctx)
    for layer in range(depth):
        mod = _adaln(cond8, w_ada[layer], b_ada[layer])
        x = _layer(x, ctx, mod, norm_mix_g[layer], norm_ffn_g[layer], w_in[layer], b_mgates[layer],
                   mlstm_norm_g[layer], attn_q_norm_g[layer], attn_k_norm_g[layer], attn_sink[layer],
                   w_branch_m[layer], w_branch_a[layer], w_out[layer], peer_w_query[layer],
                   peer_sub_keys[layer], peer_u[layer], peer_v[layer], cos_t, sin_t)
    return x
```

```python
import functools
import math

import jax
import jax.numpy as jnp
from jax import lax
from jax.experimental import pallas as pl
from jax.experimental.pallas import tpu as pltpu

F32 = jnp.float32
BF16 = jnp.bfloat16

EPS = 1e-6
GRID_W = 64
ROPE_BASE = 10000.0

MLSTM_HEADS = 4
MLSTM_DH = 128
MLSTM_W = MLSTM_HEADS * MLSTM_DH
MLSTM_CHUNK = 128
GATE_LANES = 128

ATTN_Q_HEADS = 8
ATTN_KV_HEADS = 2
ATTN_DH = 64
ATTN_Q_W = ATTN_Q_HEADS * ATTN_DH
ATTN_KV_W = ATTN_KV_HEADS * ATTN_DH
ATTN_BLOCK = 128

PEER_HEADS = 8
PEER_N_KEYS = 128
PEER_HALF = 128
PEER_TOPK = 16

VMEM_LIMIT_BYTES = 56 * 1024 * 1024

NEG_INF = float("-inf")
POS_INF = float("inf")


def _cparams(sem):
    return pltpu.CompilerParams(dimension_semantics=sem, vmem_limit_bytes=VMEM_LIMIT_BYTES)


def _dot(a, b):
    return jnp.dot(a, b, preferred_element_type=F32)


def _dot_nt(a, b):
    return lax.dot_general(a, b, (((1,), (1,)), ((), ())), preferred_element_type=F32)


def _dot_tn(a, b):
    return lax.dot_general(a, b, (((0,), (0,)), ((), ())), preferred_element_type=F32)


def _sigmoid(x):
    return 1.0 / (1.0 + jnp.exp(-x))


def _segsum(x2, bd):
    hi = x2.astype(BF16)
    lo = (x2 - hi.astype(F32)).astype(BF16)
    return _dot(hi, bd) + _dot(lo, bd)


def _modulated(x, g, shift, scale):
    ms = jnp.mean(x * x, axis=-1, keepdims=True)
    return (x * lax.rsqrt(ms + EPS)) * g * (1.0 + scale) + shift


def _adaln_kernel(c_ref, w_ref, b_ref, o_ref):
    c = c_ref[...]
    s = c * _sigmoid(c)
    o_ref[...] = jnp.dot(s, w_ref[...], preferred_element_type=F32,
                         precision=lax.Precision.HIGHEST) + b_ref[...]


def _adaln(cond8, w, b):
    d = cond8.shape[1]
    n = w.shape[1]
    bn = n // 6
    return pl.pallas_call(
        _adaln_kernel,
        grid=(n // bn,),
        in_specs=[pl.BlockSpec((8, d), lambda j: (0, 0)),
                  pl.BlockSpec((d, bn), lambda j: (0, j)),
                  pl.BlockSpec((1, bn), lambda j: (0, j))],
        out_specs=pl.BlockSpec((8, bn), lambda j: (0, j)),
        out_shape=jax.ShapeDtypeStruct((8, n), F32),
        compiler_params=_cparams(("arbitrary",)),
        name="adaln",
    )(cond8, w, b.reshape(1, n))


_C_MQ, _C_MK, _C_MV = 0, 512, 1024
_C_AQ, _C_AK, _C_AV, _C_MG = 1536, 2048, 2176, 2304
_C_END = 2432


def _rope(t, cos_t, sin_t, width):
    lane = lax.broadcasted_iota(jnp.int32, t.shape, 1)
    lower = (lane % ATTN_DH) < (ATTN_DH // 2)
    swapped = jnp.where(lower, pltpu.roll(t, width - ATTN_DH // 2, 1),
                        pltpu.roll(t, ATTN_DH // 2, 1))
    return t * cos_t + swapped * sin_t


def _proj_kernel(x_ref, sh_ref, sc_ref, g_ref, w_ref, bg_ref, qg_ref, kg_ref,
                 bdq_ref, bdk_ref, cos_ref, sin_ref,
                 mq_ref, mk_ref, mv_ref, mg_ref, aq_ref, ak_ref, av_ref, *, rope):
    x = x_ref[0]
    h = _modulated(x, g_ref[...], sh_ref[0], sc_ref[0]).astype(BF16)

    p = _dot(h, w_ref[:, _C_MQ:_C_AQ])
    mq_ref[0] = p[:, 0:512].astype(BF16)
    mk_ref[0] = (p[:, 512:1024] * (MLSTM_DH ** -0.5)).astype(BF16)
    mv_ref[0] = p[:, 1024:1536].astype(BF16)

    pq = _dot(h, w_ref[:, _C_AQ:_C_AK])
    ssq = _segsum(pq * pq, bdq_ref[...])
    q = pq * lax.rsqrt(ssq * (1.0 / ATTN_DH) + EPS) * qg_ref[...]
    if rope:
        cos_t = cos_ref[...]
        sin_t = sin_ref[...]
        q = _rope(q, jnp.concatenate([cos_t] * 4, axis=1), jnp.concatenate([sin_t] * 4, axis=1),
                  ATTN_Q_W)
    aq_ref[0] = (q * (ATTN_DH ** -0.5)).astype(BF16)

    pk = _dot(h, w_ref[:, _C_AK:_C_AV])
    ssk = _segsum(pk * pk, bdk_ref[...])
    k = pk * lax.rsqrt(ssk * (1.0 / ATTN_DH) + EPS) * kg_ref[...]
    if rope:
        k = _rope(k, cos_ref[...], sin_ref[...], ATTN_KV_W)
    ak_ref[0] = k.astype(BF16)

    av_ref[0] = _dot(h, w_ref[:, _C_AV:_C_MG]).astype(BF16)

    z = _dot(h, w_ref[:, _C_MG:_C_END]) + bg_ref[...]
    lane = lax.broadcasted_iota(jnp.int32, z.shape, 1)
    is_forget = (lane % 8) >= MLSTM_HEADS
    logsig = jnp.minimum(z, 0.0) - jnp.log1p(jnp.exp(-jnp.abs(z)))
    mg_ref[0] = jnp.where(is_forget, logsig, z)


def _proj(x, shift, scale, g, w_pack, bg, qg, kg, bdq, bdk, cos_t, sin_t, *, rope, tm):
    B, T, D = x.shape
    nt = T // tm
    full = lambda shape: pl.BlockSpec(shape, lambda b, t: (0,) * len(shape))
    tok = lambda w: pl.BlockSpec((1, tm, w), lambda b, t: (b, t, 0))
    outs = [(MLSTM_W, BF16), (MLSTM_W, BF16), (MLSTM_W, BF16), (GATE_LANES, F32),
            (ATTN_Q_W, BF16), (ATTN_KV_W, BF16), (ATTN_KV_W, BF16)]
    return pl.pallas_call(
        functools.partial(_proj_kernel, rope=rope),
        grid=(B, nt),
        in_specs=[tok(D),
                  pl.BlockSpec((1, 1, D), lambda b, t: (b, 0, 0)),
                  pl.BlockSpec((1, 1, D), lambda b, t: (b, 0, 0)),
                  full((1, D)), full(w_pack.shape), full((1, GATE_LANES)),
                  full((1, ATTN_Q_W)), full((1, ATTN_KV_W)),
                  full((ATTN_Q_W, ATTN_Q_W)), full((ATTN_KV_W, ATTN_KV_W)),
                  pl.BlockSpec((tm, ATTN_KV_W), lambda b, t: (t, 0)),
                  pl.BlockSpec((tm, ATTN_KV_W), lambda b, t: (t, 0))],
        out_specs=[tok(w) for w, _ in outs],
        out_shape=[jax.ShapeDtypeStruct((B, T, w), dt) for w, dt in outs],
        compiler_params=_cparams(("parallel", "parallel")),
        name="proj_rope" if rope else "proj_ctx",
    )(x, shift, scale, g, w_pack, bg, qg, kg, bdq, bdk, cos_t, sin_t)


def _mlstm_kernel(qf_ref, qb_ref, kcf_ref, kcb_ref, klf_ref, klb_ref, vcf_ref, vcb_ref,
                  vlf_ref, vlb_ref, gcf_ref, gcb_ref, glf_ref, glb_ref,
                  hf_ref, hb_ref, c_ref, n_ref, m_ref, *, n_ctx_chunks):
    s = pl.program_id(0)
    L = MLSTM_CHUNK
    n_batch = qf_ref.shape[0]
    in_ctx = s < n_ctx_chunks

    @pl.when(s == 0)
    def _():
        c_ref[...] = jnp.zeros_like(c_ref)
        n_ref[...] = jnp.zeros_like(n_ref)
        m_ref[...] = jnp.zeros_like(m_ref)

    row = lax.broadcasted_iota(jnp.int32, (L, L), 0)
    col = lax.broadcasted_iota(jnp.int32, (L, L), 1)
    tri_f = (col <= row).astype(F32)
    pick = lambda c_ref_, l_ref_, bb: jnp.where(in_ctx, c_ref_[bb], l_ref_[bb])


    dirs = {}
    for bb in range(n_batch):
        for d in range(2):
            q_all = (qf_ref, qb_ref)[d][bb]
            k_all = pick((kcf_ref, kcb_ref)[d], (klf_ref, klb_ref)[d], bb)
            v_all = pick((vcf_ref, vcb_ref)[d], (vlf_ref, vlb_ref)[d], bb)
            g = pick((gcf_ref, gcb_ref)[d], (glf_ref, glb_ref)[d], bb)
            cum_f = jnp.dot(tri_f, g, preferred_element_type=F32, precision=lax.Precision.HIGHEST)
            if d == 0:
                b_all, seen, last = cum_f, col <= row, L - 1
            else:
                b_all, seen, last = cum_f[L - 1:L, :] - cum_f + g, col >= row, 0
            dirs[bb, d] = (q_all, k_all, v_all, g, b_all, g.T, b_all.T, seen, last)

    heads = [(bb, d, hh) for bb in range(n_batch) for d in range(2) for hh in range(MLSTM_HEADS)]
    hsl = lambda hh: slice(hh * MLSTM_DH, (hh + 1) * MLSTM_DH)
    sidx = lambda bb, d, hh: (bb * 2 + d) * MLSTM_HEADS + hh
    gate_cols = lambda d, hh: (d * 2 * MLSTM_HEADS + hh, d * 2 * MLSTM_HEADS + hh + MLSTM_HEADS)

    qk, qc, state = {}, {}, {}
    for key in heads:
        bb, d, hh = key
        q_all, k_all = dirs[bb, d][0], dirs[bb, d][1]
        r = sidx(*key)
        c_s = c_ref[r]
        state[key] = (c_s, n_ref[r], m_ref[r][:, 0:1])
        qk[key] = _dot_nt(q_all[:, hsl(hh)], k_all[:, hsl(hh)])
        qc[key] = _dot(q_all[:, hsl(hh)], c_s.astype(BF16))

    summ = {}
    for key in heads:
        bb, d, hh = key
        _, k_all, _, g, b_all, _, _, _, last = dirs[bb, d]
        ci, cf = gate_cols(d, hh)
        a = b_all[last:last + 1, cf:cf + 1]
        w_log = a - b_all[:, cf:cf + 1] + g[:, ci:ci + 1]
        m_loc = jnp.max(w_log, axis=0, keepdims=True)
        kw = k_all[:, hsl(hh)].astype(F32) * jnp.exp(w_log - m_loc)
        summ[key] = (a, m_loc, kw)
    for key in heads:
        bb, d, hh = key
        v_all = dirs[bb, d][2]
        r = sidx(*key)
        c_s, n_s, m_s = state[key]
        a, m_loc, kw = summ[key]
        c_loc = _dot_tn(kw.astype(BF16), v_all[:, hsl(hh)])
        n_loc = jnp.sum(kw, axis=0, keepdims=True)
        m_new = jnp.maximum(a + m_s, m_loc)
        d_old = jnp.exp(a + m_s - m_new)
        d_new = jnp.exp(m_loc - m_new)
        c_ref[r] = d_old * c_s + d_new * c_loc
        n_ref[r] = d_old * n_s + d_new * n_loc
        m_ref[r] = jnp.broadcast_to(m_new, (1, MLSTM_DH))

    rel, m_rel, sc, inter, den = {}, {}, {}, {}, {}
    for key in heads:
        bb, d, hh = key
        _, _, _, _, _, g_t, b_t, seen, _ = dirs[bb, d]
        ci, cf = gate_cols(d, hh)
        rel[key] = jnp.where(seen, g_t[ci:ci + 1, :] - b_t[cf:cf + 1, :], NEG_INF)
        m_rel[key] = jnp.maximum(state[key][2], jnp.max(rel[key], axis=1, keepdims=True))
    for key in heads:
        m_s = state[key][2]
        m_rel_b = jnp.broadcast_to(m_rel[key], (L, MLSTM_DH))
        sc[key] = qk[key] * jnp.exp(rel[key] - m_rel_b)
        inter[key] = jnp.exp(m_s - m_rel_b)
    for key in heads:
        bb, d, hh = key
        qn = dirs[bb, d][0][:, hsl(hh)].astype(F32) * state[key][1]
        den[key] = jnp.sum(sc[key] + inter[key] * qn, axis=1, keepdims=True)
    for key in heads:
        bb, d, hh = key
        _, _, v_all, _, b_all, _, _, _, _ = dirs[bb, d]
        out_ref = (hf_ref, hb_ref)[d]
        cf = gate_cols(d, hh)[1]
        num = _dot(sc[key].astype(BF16), v_all[:, hsl(hh)]) + inter[key] * qc[key]
        floor = jnp.exp(-(b_all[:, cf:cf + 1] + m_rel[key]))
        out_ref[bb, :, hsl(hh)] = num * (1.0 / jnp.maximum(jnp.abs(den[key]), floor))


def _mlstm(q_lat, k_ctx, v_ctx, g_ctx, k_lat, v_lat, g_lat):
    B, T, _ = q_lat.shape
    L = MLSTM_CHUNK
    nl = T // L
    nc = k_ctx.shape[1] // L
    steps = nc + nl

    fwd_c = lambda s: (0, jnp.minimum(s, nc - 1), 0)
    bwd_c = lambda s: (0, jnp.maximum(nc - 1 - s, 0), 0)
    fwd_l = lambda s: (0, jnp.maximum(s - nc, 0), 0)
    bwd_l = lambda s: (0, jnp.minimum(steps - 1 - s, nl - 1), 0)

    blk = lambda w, im: pl.BlockSpec((B, L, w), im)
    pair = lambda w, f, g: [blk(w, f), blk(w, g)]
    nheads = B * 2 * MLSTM_HEADS
    return pl.pallas_call(
        functools.partial(_mlstm_kernel, n_ctx_chunks=nc),
        grid=(steps,),
        in_specs=(pair(MLSTM_W, fwd_l, bwd_l)
                  + pair(MLSTM_W, fwd_c, bwd_c) + pair(MLSTM_W, fwd_l, bwd_l)
                  + pair(MLSTM_W, fwd_c, bwd_c) + pair(MLSTM_W, fwd_l, bwd_l)
                  + pair(GATE_LANES, fwd_c, bwd_c) + pair(GATE_LANES, fwd_l, bwd_l)),
        out_specs=pair(MLSTM_W, fwd_l, bwd_l),
        out_shape=[jax.ShapeDtypeStruct((B, T, MLSTM_W), F32)] * 2,
        scratch_shapes=[pltpu.VMEM((nheads, MLSTM_DH, MLSTM_DH), F32),
                        pltpu.VMEM((nheads, 1, MLSTM_DH), F32),
                        pltpu.VMEM((nheads, 1, MLSTM_DH), F32)],
        compiler_params=_cparams(("arbitrary",)),
        name="mlstm",
    )(q_lat, q_lat, k_ctx, k_ctx, k_lat, k_lat, v_ctx, v_ctx, v_lat, v_lat,
      g_ctx, g_ctx, g_lat, g_lat)


def _attn_kernel(q_ref, kp_ref, kc_ref, kn_ref, vp_ref, vc_ref, vn_ref, kx_ref, vx_ref,
                 sink_ref, o_ref, *, nb):
    n = pl.program_id(0)
    blk = ATTN_BLOCK
    n_batch = q_ref.shape[0]

    qi = lax.broadcasted_iota(jnp.int32, (blk, blk), 0)
    kj = lax.broadcasted_iota(jnp.int32, (blk, blk), 1)
    ok_prev = (kj >= qi) & (n > 0)
    ok_next = (kj <= qi) & (n < nb - 1)
    lane = lax.broadcasted_iota(jnp.int32, (blk, 2 * ATTN_DH), 1)
    low = lane < ATTN_DH
    n_pairs = ATTN_Q_W // (2 * ATTN_DH)
    n_heads = 2 * n_pairs

    def fold(op, cols):
        groups = [c[:, i:i + blk] for c in cols for i in range(0, c.shape[1], blk)]
        return functools.reduce(op, groups)

    s_all, vcat = [], []
    for bb in range(n_batch):
        kcat = jnp.concatenate([kp_ref[bb], kc_ref[bb], kn_ref[bb], kx_ref[bb]], axis=0)
        vcat.append(jnp.concatenate([vp_ref[bb], vc_ref[bb], vn_ref[bb], vx_ref[bb]], axis=0))
        pieces = []
        for c in range(n_pairs):
            qc = q_ref[bb, :, c * 128:(c + 1) * 128]
            pieces += [jnp.where(low, qc, jnp.zeros_like(qc)), jnp.where(low, jnp.zeros_like(qc), qc)]
        s_all.append(_dot_nt(jnp.concatenate(pieces, axis=0), kcat))

    parts, maxes, probs, dens = {}, {}, {}, {}
    keys = [(bb, j) for bb in range(n_batch) for j in range(n_heads)]
    for bb, j in keys:
        sc = s_all[bb][j * blk:(j + 1) * blk]
        parts[bb, j] = [jnp.where(ok_prev, sc[:, 0:blk], NEG_INF), sc[:, blk:2 * blk],
                        jnp.where(ok_next, sc[:, 2 * blk:3 * blk], NEG_INF), sc[:, 3 * blk:]]
        row_max = jnp.max(fold(jnp.maximum, parts[bb, j]), axis=1, keepdims=True)
        maxes[bb, j] = jnp.maximum(row_max, sink_ref[j:j + 1, 0:1])
    for bb, j in keys:
        exps = [jnp.exp(part - maxes[bb, j]) for part in parts[bb, j]]
        row_sum = jnp.sum(fold(jnp.add, exps), axis=1, keepdims=True)
        dens[bb, j] = row_sum + jnp.exp(sink_ref[j:j + 1, 0:1] - maxes[bb, j])
        probs[bb, j] = jnp.concatenate([p.astype(BF16) for p in exps], axis=1)
    for bb in range(n_batch):
        o_all = _dot(jnp.concatenate([probs[bb, j] for j in range(n_heads)], axis=0), vcat[bb])
        for c in range(n_pairs):
            o_lo = o_all[(2 * c) * blk:(2 * c + 1) * blk] / dens[bb, 2 * c]
            o_hi = o_all[(2 * c + 1) * blk:(2 * c + 2) * blk] / dens[bb, 2 * c + 1]
            o_ref[bb, :, c * 128:(c + 1) * 128] = jnp.where(low, o_lo, o_hi).astype(BF16)


def _attn(aq, ak, av, akx, avx, sink8):
    B, T, _ = aq.shape
    blk = ATTN_BLOCK
    nb = T // blk
    ctx = akx.shape[1]
    kv = lambda im: pl.BlockSpec((B, blk, ATTN_KV_W), im)
    prev = lambda n: (0, jnp.maximum(n - 1, 0), 0)
    cur = lambda n: (0, n, 0)
    nxt = lambda n: (0, jnp.minimum(n + 1, nb - 1), 0)
    cx = pl.BlockSpec((B, ctx, ATTN_KV_W), lambda n: (0, 0, 0))
    return pl.pallas_call(
        functools.partial(_attn_kernel, nb=nb),
        grid=(nb,),
        in_specs=[pl.BlockSpec((B, blk, ATTN_Q_W), cur),
                  kv(prev), kv(cur), kv(nxt), kv(prev), kv(cur), kv(nxt), cx, cx,
                  pl.BlockSpec((8, 128), lambda n: (0, 0))],
        out_specs=pl.BlockSpec((B, blk, ATTN_Q_W), cur),
        out_shape=jax.ShapeDtypeStruct((B, T, ATTN_Q_W), BF16),
        compiler_params=_cparams(("parallel",)),
        name="attn",
    )(aq, ak, ak, ak, av, av, av, akx, avx, sink8)


def _merge_kernel(x_ref, sh_ref, sc_ref, g_ref, g1_ref, hf_ref, hb_ref, a_ref,
                  wg_ref, ng_ref, wbm_ref, wba_ref, wo_ref, o_ref):
    x = x_ref[0]
    h = _modulated(x, g_ref[...], sh_ref[0], sc_ref[0]).astype(BF16)
    pg = _dot(h, wg_ref[...])
    d = x.shape[1]
    hs = hf_ref[0] + hb_ref[0]
    heads = [hs[:, i:i + MLSTM_DH] for i in range(0, MLSTM_W, MLSTM_DH)]
    hn = jnp.concatenate(
        [hh * lax.rsqrt(jnp.mean(hh * hh, axis=1, keepdims=True) + EPS) for hh in heads], axis=1)
    m = hn * ng_ref[...] * _sigmoid(pg[:, 0:MLSTM_W])
    mm = _dot(m.astype(BF16), wbm_ref[...])
    aa = _dot(a_ref[0], wba_ref[...])
    y = _sigmoid(pg[:, MLSTM_W:MLSTM_W + d]) * mm + _sigmoid(pg[:, MLSTM_W + d:]) * aa
    o_ref[0] = x + g1_ref[0] * _dot(y.astype(BF16), wo_ref[...])


def _merge(x, shift, scale, g, g1, hf, hb, a, wg, ng, wbm, wba, wo, *, tm):
    B, T, D = x.shape
    full = lambda arr: pl.BlockSpec(arr.shape, lambda b, t: (0,) * arr.ndim)
    tok = lambda w: pl.BlockSpec((1, tm, w), lambda b, t: (b, t, 0))
    perb = pl.BlockSpec((1, 1, D), lambda b, t: (b, 0, 0))
    return pl.pallas_call(
        _merge_kernel,
        grid=(B, T // tm),
        in_specs=[tok(D), perb, perb, full(g), perb, tok(MLSTM_W), tok(MLSTM_W), tok(ATTN_Q_W),
                  full(wg), full(ng), full(wbm), full(wba), full(wo)],
        out_specs=tok(D),
        out_shape=jax.ShapeDtypeStruct((B, T, D), F32),
        compiler_params=_cparams(("parallel", "parallel")),
        name="merge",
    )(x, shift, scale, g, g1, hf, hb, a, wg, ng, wbm, wba, wo)


def _pquery_kernel(x_ref, sh_ref, sc_ref, g_ref, wq_ref, sk_ref, h_ref, s_ref):
    x = x_ref[0]
    h = _modulated(x, g_ref[...], sh_ref[0], sc_ref[0]).astype(BF16)
    h_ref[0] = h
    q = _dot(h, wq_ref[...]).astype(BF16)
    for j in range(2 * PEER_HEADS):
        s_ref[0, j] = _dot_nt(sk_ref[j], q[:, j * PEER_HALF:(j + 1) * PEER_HALF])


def _pquery(x, shift, scale, g, wq, sk, *, tm):
    B, T, D = x.shape
    nj = 2 * PEER_HEADS
    full = lambda arr: pl.BlockSpec(arr.shape, lambda b, t: (0,) * arr.ndim)
    perb = pl.BlockSpec((1, 1, D), lambda b, t: (b, 0, 0))
    return pl.pallas_call(
        _pquery_kernel,
        grid=(B, T // tm),
        in_specs=[pl.BlockSpec((1, tm, D), lambda b, t: (b, t, 0)), perb, perb,
                  full(g), full(wq), full(sk)],
        out_specs=[pl.BlockSpec((1, tm, D), lambda b, t: (b, t, 0)),
                   pl.BlockSpec((1, nj, PEER_N_KEYS, tm), lambda b, t: (b, 0, 0, t))],
        out_shape=[jax.ShapeDtypeStruct((B, T, D), BF16),
                   jax.ShapeDtypeStruct((B, nj, PEER_N_KEYS, T), F32)],
        compiler_params=_cparams(("parallel", "parallel")),
        name="pquery",
    )(x, shift, scale, g, wq, sk)


def _top_values(val, k):
    tops = []
    for _ in range(k):
        m = jnp.max(val, axis=0, keepdims=True)
        tops.append(m)
        val = jnp.where(val == m, NEG_INF, val)
    return tops


_GELU_K0 = -2.0 * math.sqrt(2.0 / math.pi) * math.log2(math.e)
_GELU_K1 = 0.044715 * _GELU_K0


def _gelu_tanh(x, k0, k1):
    one = jnp.asarray(1.0, x.dtype)
    return x / (one + jnp.exp2(x * (x * x * k1 + k0)))


BF16_ROWS = 16


def _rows_bf16(row, tm):
    tile = jnp.broadcast_to(row, (BF16_ROWS, tm)).astype(BF16)
    return jnp.concatenate([tile] * (PEER_N_KEYS // BF16_ROWS), axis=0)


def _row_tile_bf16(ref, h, i, tm):
    return _rows_bf16(ref[h, pl.ds(i, 1), :], tm)


def _batcher_pairs(n):
    pairs = []
    p = 1
    while p < n:
        k = p
        while k >= 1:
            for j in range(k % p, n - k, 2 * k):
                for i in range(min(k, n - j - k)):
                    if (i + j) // (2 * p) == (i + j + k) // (2 * p):
                        pairs.append((i + j, i + j + k))
            k //= 2
        p *= 2
    return pairs


def _compare_exchange(rows, i, j):
    rows[i], rows[j] = jnp.maximum(rows[i], rows[j]), jnp.minimum(rows[i], rows[j])


def _sorted_top(x, k):
    sub = 8
    assert x.shape[0] == sub * k and k & (k - 1) == 0
    rows = [x[sub * r:sub * (r + 1), :] for r in range(k)]
    for i, j in _batcher_pairs(k):
        _compare_exchange(rows, i, j)
    shift = sub // 2
    while shift >= 1:
        other = [pltpu.roll(r, shift, 0) for r in rows]
        rows = [jnp.maximum(rows[r], other[k - 1 - r]) for r in range(k)]
        d = k // 2
        while d >= 1:
            for i in range(k):
                if i & d == 0:
                    _compare_exchange(rows, i, i + d)
            d //= 2
        shift //= 2
    return rows


def _prefix_count(pred, vals):
    k = len(vals)
    bits, steps = [], []
    s = k // 2
    while s >= 1:
        def pivot(i, base):
            if i == len(bits):
                return vals[base + s - 1]
            return jnp.where(bits[i], pivot(i + 1, base + steps[i]), pivot(i + 1, base))
        bits.append(pred(pivot(0, 0)))
        steps.append(s)
        s //= 2
    n = functools.reduce(jnp.add, [jnp.where(b, float(st), 0.0) for b, st in zip(bits, steps)])
    return jnp.where(pred(vals[k - 1]), float(k), n)


def _peer_router(s_ref, rank_ref, cw_ref, cnt_ref, e2_ref):
    k = PEER_TOPK
    for h in range(PEER_HEADS):
        s1 = s_ref[0, 2 * h]
        s2 = s_ref[0, 2 * h + 1]
        a = [r[0:1, :] for r in _sorted_top(s1, k)]
        b = [r[0:1, :] for r in _sorted_top(s2, k)]
        cand = [a[p] + b[q] for p in range(k) for q in range(k) if (p + 1) * (q + 1) <= k]
        cv = jnp.concatenate(cand, axis=0)
        thr = _top_values(cv, k)[k - 1]
        top = a[0] + b[0]
        z = jnp.sum(jnp.where(cv >= thr, jnp.exp(cv - top), 0.0), axis=0, keepdims=True)
        rank = _prefix_count(lambda v: v > s1, a)
        cnt = _prefix_count(lambda v: v + s2 >= thr, a)
        rank_ref[h] = rank
        cw_ref[h] = jnp.exp(s1 - a[0]) / z
        cnt_ref[h] = cnt.astype(BF16)
        e2_ref[h] = jnp.exp(s2 - b[0]).astype(BF16)


def _peer_kernel(h_ref, s_ref, u_ref, vt_ref, x_ref, g2_ref, gk_ref, o_ref,
                 acc_ref, rank_ref, cw_ref, cnt_ref, e2_ref, wt_ref, *, eb, sb):
    e = pl.program_id(2)
    ne = pl.num_programs(2)
    tm = h_ref.shape[1]

    @pl.when(e == 0)
    def _():
        acc_ref[...] = jnp.zeros_like(acc_ref)
        _peer_router(s_ref, rank_ref, cw_ref, cnt_ref, e2_ref)

    hq = h_ref[0]
    n_i = sb // PEER_N_KEYS
    n_sub = eb // sb
    scores = lambda j: _dot_nt(u_ref[j * sb:(j + 1) * sb, :], hq)
    zero = jnp.zeros((), BF16)
    gk0 = _rows_bf16(gk_ref[0:1, :], tm)
    gk1 = _rows_bf16(gk_ref[1:2, :], tm)
    def output_product(j):
        acc_ref[...] += _dot(vt_ref[:, j * sb:(j + 1) * sb], wt_ref[j % 2])

    acts = {j: scores(j) for j in range(min(2, n_sub))}
    for j in range(n_sub):
        act = acts.pop(j)
        for ii in range(n_i):
            if ii == n_i // 2 and j > 0:
                output_product(j - 1)
            i = e * (eb // PEER_N_KEYS) + j * n_i + ii
            w = None
            for h in range(PEER_HEADS):
                rank_i = _row_tile_bf16(rank_ref, h, i, tm)
                cw_i = _row_tile_bf16(cw_ref, h, i, tm)
                term = jnp.where(rank_i < cnt_ref[h], e2_ref[h], zero) * cw_i
                w = term if w is None else w + term
            act_i = act[ii * PEER_N_KEYS:(ii + 1) * PEER_N_KEYS].astype(BF16)
            wt_ref[j % 2, ii * PEER_N_KEYS:(ii + 1) * PEER_N_KEYS, :] = w * _gelu_tanh(act_i, gk0, gk1)
        if j + 2 < n_sub:
            acts[j + 2] = scores(j + 2)
    output_product(n_sub - 1)

    @pl.when(e == ne - 1)
    def _():
        o_ref[0] = x_ref[0] + g2_ref[0] * acc_ref[...].T


def _peer(h2, s_t, u, vt, x1, g2, *, tm, eb, sb):
    B, T, D = x1.shape
    ne = u.shape[0] // eb
    nj = 2 * PEER_HEADS
    gelu_k = jnp.broadcast_to(jnp.array([[_GELU_K0], [_GELU_K1]], F32), (2, tm))
    return pl.pallas_call(
        functools.partial(_peer_kernel, eb=eb, sb=sb),
        grid=(B, T // tm, ne),
        in_specs=[pl.BlockSpec((1, tm, D), lambda b, t, e: (b, t, 0)),
                  pl.BlockSpec((1, nj, PEER_N_KEYS, tm), lambda b, t, e: (b, 0, 0, t)),
                  pl.BlockSpec((eb, D), lambda b, t, e: (e, 0)),
                  pl.BlockSpec((D, eb), lambda b, t, e: (0, e)),
                  pl.BlockSpec((1, tm, D), lambda b, t, e: (b, t, 0)),
                  pl.BlockSpec((1, 1, D), lambda b, t, e: (b, 0, 0)),
                  pl.BlockSpec((2, tm), lambda b, t, e: (0, 0))],
        out_specs=pl.BlockSpec((1, tm, D), lambda b, t, e: (b, t, 0)),
        out_shape=jax.ShapeDtypeStruct((B, T, D), F32),
        scratch_shapes=[pltpu.VMEM((D, tm), F32),
                        pltpu.VMEM((PEER_HEADS, PEER_N_KEYS, tm), F32),
                        pltpu.VMEM((PEER_HEADS, PEER_N_KEYS, tm), F32),
                        pltpu.VMEM((PEER_HEADS, PEER_N_KEYS, tm), BF16),
                        pltpu.VMEM((PEER_HEADS, PEER_N_KEYS, tm), BF16),
                        pltpu.VMEM((2, sb, tm), BF16)],
        compiler_params=_cparams(("parallel", "parallel", "arbitrary")),
        name="peer",
    )(h2, s_t, u, vt, x1, g2, gelu_k)


_Q_HEAD_ORDER = (0, 4, 1, 5, 2, 6, 3, 7)


def _block_diag_ones(width, seg):
    i = jnp.arange(width)
    return (i[:, None] // seg == i[None, :] // seg).astype(BF16)


def _rope_tables(T):
    rows = T // GRID_W
    nf = ATTN_DH // 4
    row = jnp.broadcast_to(jnp.arange(rows)[:, None], (rows, GRID_W)).reshape(T)
    col = jnp.broadcast_to(jnp.arange(GRID_W)[None, :], (rows, GRID_W)).reshape(T)
    inv = ROPE_BASE ** (-jnp.arange(nf, dtype=F32) / nf)
    ang = jnp.concatenate([row[:, None].astype(F32) * inv, col[:, None].astype(F32) * inv], axis=-1)
    cos, sin = jnp.cos(ang), jnp.sin(ang)
    cos_t = jnp.concatenate([cos, cos] * ATTN_KV_HEADS, axis=-1)
    sin_t = jnp.concatenate([-sin, sin] * ATTN_KV_HEADS, axis=-1)
    return cos_t, sin_t


def _layer(x, ctx, mod, norm_mix_g, norm_ffn_g, w_in, b_mgates, mlstm_norm_g, attn_q_norm_g,
           attn_k_norm_g, attn_sink, w_branch_m, w_branch_a, w_out, peer_w_query, peer_sub_keys,
           peer_u, peer_v, cos_t, sin_t):
    B, T, D = x.shape
    n_ctx = ctx.shape[1]
    mods = jnp.split(mod, 6, axis=-1)
    lat = [m[0:B].reshape(B, 1, D) for m in mods]
    cxm = [jnp.broadcast_to(m[B:B + 1].reshape(1, 1, D), (B, 1, D)) for m in mods]
    sh1, sc1, g1, sh2, sc2, g2 = lat

    W = MLSTM_W
    o0 = 4 * W
    o1 = o0 + 2 * 2 * MLSTM_HEADS
    w_mq, w_mk, w_mv, w_mo = (w_in[:, i * W:(i + 1) * W] for i in range(4))
    w_g16 = w_in[:, o0:o1]
    w_aq = w_in[:, o1:o1 + ATTN_Q_W].reshape(D, ATTN_Q_HEADS, ATTN_DH)
    w_aq = w_aq[:, jnp.array(_Q_HEAD_ORDER), :].reshape(D, ATTN_Q_W)
    o2 = o1 + ATTN_Q_W
    w_ak = w_in[:, o2:o2 + ATTN_KV_W]
    w_av = w_in[:, o2 + ATTN_KV_W:o2 + 2 * ATTN_KV_W]
    w_gates = w_in[:, o2 + 2 * ATTN_KV_W:]
    w_g128 = jnp.pad(w_g16, ((0, 0), (0, GATE_LANES - w_g16.shape[1])))
    w_pack = jnp.concatenate([w_mq, w_mk, w_mv, w_aq, w_ak, w_av, w_g128], axis=1).astype(BF16)
    w_merge_in = jnp.concatenate([w_mo, w_gates], axis=1).astype(BF16)
    bg = jnp.pad(b_mgates, (0, GATE_LANES - b_mgates.shape[0])).reshape(1, GATE_LANES)
    qg = jnp.tile(attn_q_norm_g, ATTN_Q_HEADS).reshape(1, ATTN_Q_W)
    kg = jnp.tile(attn_k_norm_g, ATTN_KV_HEADS).reshape(1, ATTN_KV_W)
    bdq = _block_diag_ones(ATTN_Q_W, ATTN_DH)
    bdk = _block_diag_ones(ATTN_KV_W, ATTN_DH)
    gmix = norm_mix_g.reshape(1, D)

    tm_lat = min(512, T)
    mq, mk, mv, mg, aq, ak, av = _proj(x, sh1, sc1, gmix, w_pack, bg, qg, kg, bdq, bdk,
                                       cos_t, sin_t, rope=True, tm=tm_lat)
    _, mkc, mvc, mgc, _, akc, avc = _proj(ctx, cxm[0], cxm[1], gmix, w_pack, bg, qg, kg, bdq, bdk,
                                          cos_t, sin_t, rope=False, tm=n_ctx)

    hf, hb = _mlstm(mq, mkc, mvc, mgc, mk, mv, mg)

    sink8 = jnp.broadcast_to(attn_sink[jnp.array(_Q_HEAD_ORDER)].astype(F32)[:, None], (8, 128))
    a = _attn(aq, ak, av, akc, avc, sink8)

    wba = w_branch_a.reshape(ATTN_Q_HEADS, ATTN_DH, D)[jnp.array(_Q_HEAD_ORDER)]
    wba = wba.reshape(ATTN_Q_W, D).astype(BF16)
    x1 = _merge(x, sh1, sc1, gmix, g1, hf, hb, a, w_merge_in, mlstm_norm_g.reshape(1, MLSTM_W),
                w_branch_m.astype(BF16), wba, w_out.astype(BF16), tm=min(256, T))

    sk = peer_sub_keys.reshape(2 * PEER_HEADS, PEER_N_KEYS, PEER_HALF).astype(BF16)
    h2, s_t = _pquery(x1, sh2, sc2, norm_ffn_g.reshape(1, D), peer_w_query.astype(BF16), sk,
                      tm=min(512, T))
    u = peer_u.astype(BF16)
    vt = peer_v.T.astype(BF16)
    return _peer(h2, s_t, u, vt, x1, g2, tm=min(512, T), eb=2048, sb=1024)


def kernel(x, c, ctx, c_ctx, w_ada, b_ada, norm_mix_g, norm_ffn_g, w_in, b_mgates, mlstm_norm_g,
           attn_q_norm_g, attn_k_norm_g, attn_sink, w_branch_m, w_branch_a, w_out, peer_w_query,
           peer_sub_keys, peer_u, peer_v):
    B, T, D = x.shape
    depth = w_ada.shape[0]
    assert depth == 1, "context-stream update for deeper stacks is not implemented"
    cos_t, sin_t = _rope_tables(T)
    cond8 = jnp.zeros((8, D), F32).at[0:B].set(c).at[B].set(c_ctx)
    for layer in range(depth):
        mod = _adaln(cond8, w_ada[layer], b_ada[layer])
        x = _layer(x, ctx, mod, norm_mix_g[layer], norm_ffn_g[layer], w_in[layer], b_mgates[layer],
                   mlstm_norm_g[layer], attn_q_norm_g[layer], attn_k_norm_g[layer], attn_sink[layer],
                   w_branch_m[layer], w_branch_a[layer], w_out[layer], peer_w_query[layer],
                   peer_sub_keys[layer], peer_u[layer], peer_v[layer], cos_t, sin_t)
    return x
```

```python
import functools
import math

import jax
import jax.numpy as jnp
from jax import lax
from jax.experimental import pallas as pl
from jax.experimental.pallas import tpu as pltpu

F32 = jnp.float32
BF16 = jnp.bfloat16

EPS = 1e-6
GRID_W = 64
ROPE_BASE = 10000.0

MLSTM_HEADS = 4
MLSTM_DH = 128
MLSTM_W = MLSTM_HEADS * MLSTM_DH
MLSTM_CHUNK = 128
GATE_LANES = 128

ATTN_Q_HEADS = 8
ATTN_KV_HEADS = 2
ATTN_DH = 64
ATTN_Q_W = ATTN_Q_HEADS * ATTN_DH
ATTN_KV_W = ATTN_KV_HEADS * ATTN_DH
ATTN_BLOCK = 128

PEER_HEADS = 8
PEER_N_KEYS = 128
PEER_HALF = 128
PEER_TOPK = 16

VMEM_LIMIT_BYTES = 56 * 1024 * 1024

TOKENS_PROJ = 1024
TOKENS_MERGE = 512
TOKENS_PQUERY = 1024
TOKENS_PEER = 512
PEER_EXPERT_BLOCK = 2048
PEER_SUB_BLOCK = 1024

NEG_INF = float("-inf")


def _cparams(sem):
    return pltpu.CompilerParams(dimension_semantics=sem, vmem_limit_bytes=VMEM_LIMIT_BYTES)


def _dot(a, b):
    return jnp.dot(a, b, preferred_element_type=F32)


def _dot_nt(a, b):
    return lax.dot_general(a, b, (((1,), (1,)), ((), ())), preferred_element_type=F32)


def _dot_tn(a, b):
    return lax.dot_general(a, b, (((0,), (0,)), ((), ())), preferred_element_type=F32)


def _sigmoid(x):
    return 1.0 / (1.0 + jnp.exp(-x))


def _segsum(x2, bd):
    hi = x2.astype(BF16)
    lo = (x2 - hi.astype(F32)).astype(BF16)
    return _dot(hi, bd) + _dot(lo, bd)


def _modulated(x, g, shift, scale):
    ms = jnp.mean(x * x, axis=-1, keepdims=True)
    return (x * lax.rsqrt(ms + EPS)) * g * (1.0 + scale) + shift


def _adaln_kernel(c_ref, w_ref, b_ref, o_ref):
    c = c_ref[...]
    s = c * _sigmoid(c)
    o_ref[...] = jnp.dot(s, w_ref[...], preferred_element_type=F32,
                         precision=lax.Precision.HIGHEST) + b_ref[...]


def _adaln(cond8, w, b):
    d = cond8.shape[1]
    n = w.shape[1]
    bn = n // 6
    return pl.pallas_call(
        _adaln_kernel,
        grid=(n // bn,),
        in_specs=[pl.BlockSpec((8, d), lambda j: (0, 0)),
                  pl.BlockSpec((d, bn), lambda j: (0, j)),
                  pl.BlockSpec((1, bn), lambda j: (0, j))],
        out_specs=pl.BlockSpec((8, bn), lambda j: (0, j)),
        out_shape=jax.ShapeDtypeStruct((8, n), F32),
        compiler_params=_cparams(("arbitrary",)),
        name="adaln",
    )(cond8, w, b.reshape(1, n))


_C_MQ, _C_MK, _C_MV = 0, 512, 1024
_C_AQ, _C_AK, _C_AV, _C_MG = 1536, 2048, 2176, 2304
_C_END = 2432


def _rope(t, cos_t, sin_t, width):
    lane = lax.broadcasted_iota(jnp.int32, t.shape, 1)
    lower = (lane % ATTN_DH) < (ATTN_DH // 2)
    swapped = jnp.where(lower, pltpu.roll(t, width - ATTN_DH // 2, 1),
                        pltpu.roll(t, ATTN_DH // 2, 1))
    return t * cos_t + swapped * sin_t


def _proj_kernel(x_ref, sh_ref, sc_ref, g_ref, w_ref, bg_ref, qg_ref, kg_ref,
                 bdq_ref, bdk_ref, cos_ref, sin_ref,
                 mq_ref, mk_ref, mv_ref, mg_ref, aq_ref, ak_ref, av_ref, *, rope):
    x = x_ref[0]
    h = _modulated(x, g_ref[...], sh_ref[0], sc_ref[0]).astype(BF16)

    p = _dot(h, w_ref[:, _C_MQ:_C_AQ])
    mq_ref[0] = p[:, 0:512].astype(BF16)
    mk_ref[0] = (p[:, 512:1024] * (MLSTM_DH ** -0.5)).astype(BF16)
    mv_ref[0] = p[:, 1024:1536].astype(BF16)

    pq = _dot(h, w_ref[:, _C_AQ:_C_AK])
    ssq = _segsum(pq * pq, bdq_ref[...])
    q = pq * lax.rsqrt(ssq * (1.0 / ATTN_DH) + EPS) * qg_ref[...]
    if rope:
        cos_t = cos_ref[...]
        sin_t = sin_ref[...]
        q = _rope(q, jnp.concatenate([cos_t] * 4, axis=1), jnp.concatenate([sin_t] * 4, axis=1),
                  ATTN_Q_W)
    aq_ref[0] = (q * (ATTN_DH ** -0.5)).astype(BF16)

    pk = _dot(h, w_ref[:, _C_AK:_C_AV])
    ssk = _segsum(pk * pk, bdk_ref[...])
    k = pk * lax.rsqrt(ssk * (1.0 / ATTN_DH) + EPS) * kg_ref[...]
    if rope:
        k = _rope(k, cos_ref[...], sin_ref[...], ATTN_KV_W)
    ak_ref[0] = k.astype(BF16)

    av_ref[0] = _dot(h, w_ref[:, _C_AV:_C_MG]).astype(BF16)

    z = _dot(h, w_ref[:, _C_MG:_C_END]) + bg_ref[...]
    lane = lax.broadcasted_iota(jnp.int32, z.shape, 1)
    is_forget = (lane % 8) >= MLSTM_HEADS
    logsig = jnp.minimum(z, 0.0) - jnp.log1p(jnp.exp(-jnp.abs(z)))
    mg_ref[0] = jnp.where(is_forget, logsig, z)


def _proj(x, shift, scale, g, w_pack, bg, qg, kg, bdq, bdk, cos_t, sin_t, *, rope, tm):
    B, T, D = x.shape
    nt = T // tm
    full = lambda shape: pl.BlockSpec(shape, lambda b, t: (0,) * len(shape))
    tok = lambda w: pl.BlockSpec((1, tm, w), lambda b, t: (b, t, 0))
    outs = [(MLSTM_W, BF16), (MLSTM_W, BF16), (MLSTM_W, BF16), (GATE_LANES, F32),
            (ATTN_Q_W, BF16), (ATTN_KV_W, BF16), (ATTN_KV_W, BF16)]
    return pl.pallas_call(
        functools.partial(_proj_kernel, rope=rope),
        grid=(B, nt),
        in_specs=[tok(D),
                  pl.BlockSpec((1, 1, D), lambda b, t: (b, 0, 0)),
                  pl.BlockSpec((1, 1, D), lambda b, t: (b, 0, 0)),
                  full((1, D)), full(w_pack.shape), full((1, GATE_LANES)),
                  full((1, ATTN_Q_W)), full((1, ATTN_KV_W)),
                  full((ATTN_Q_W, ATTN_Q_W)), full((ATTN_KV_W, ATTN_KV_W)),
                  pl.BlockSpec((tm, ATTN_KV_W), lambda b, t: (t, 0)),
                  pl.BlockSpec((tm, ATTN_KV_W), lambda b, t: (t, 0))],
        out_specs=[tok(w) for w, _ in outs],
        out_shape=[jax.ShapeDtypeStruct((B, T, w), dt) for w, dt in outs],
        compiler_params=_cparams(("parallel", "parallel")),
        name="proj_rope" if rope else "proj_ctx",
    )(x, shift, scale, g, w_pack, bg, qg, kg, bdq, bdk, cos_t, sin_t)


def _mlstm_kernel(qf_ref, qb_ref, kcf_ref, kcb_ref, klf_ref, klb_ref, vcf_ref, vcb_ref,
                  vlf_ref, vlb_ref, gcf_ref, gcb_ref, glf_ref, glb_ref,
                  hf_ref, hb_ref, c_ref, n_ref, m_ref, *, n_ctx_chunks):
    s = pl.program_id(0)
    L = MLSTM_CHUNK
    n_batch = qf_ref.shape[0]
    in_ctx = s < n_ctx_chunks

    @pl.when(s == 0)
    def _():
        c_ref[...] = jnp.zeros_like(c_ref)
        n_ref[...] = jnp.zeros_like(n_ref)
        m_ref[...] = jnp.zeros_like(m_ref)

    row = lax.broadcasted_iota(jnp.int32, (L, L), 0)
    col = lax.broadcasted_iota(jnp.int32, (L, L), 1)
    tri_f = (col <= row).astype(F32)
    pick = lambda c_ref_, l_ref_, bb: jnp.where(in_ctx, c_ref_[bb], l_ref_[bb])


    dirs = {}
    for bb in range(n_batch):
        for d in range(2):
            q_all = (qf_ref, qb_ref)[d][bb]
            k_all = pick((kcf_ref, kcb_ref)[d], (klf_ref, klb_ref)[d], bb)
            v_all = pick((vcf_ref, vcb_ref)[d], (vlf_ref, vlb_ref)[d], bb)
            g = pick((gcf_ref, gcb_ref)[d], (glf_ref, glb_ref)[d], bb)
            cum_f = jnp.dot(tri_f, g, preferred_element_type=F32, precision=lax.Precision.HIGHEST)
            if d == 0:
                b_all, seen, last = cum_f, col <= row, L - 1
            else:
                b_all, seen, last = cum_f[L - 1:L, :] - cum_f + g, col >= row, 0
            dirs[bb, d] = (q_all, k_all, v_all, g, b_all, g.T, b_all.T, seen, last)

    heads = [(bb, d, hh) for bb in range(n_batch) for d in range(2) for hh in range(MLSTM_HEADS)]
    hsl = lambda hh: slice(hh * MLSTM_DH, (hh + 1) * MLSTM_DH)
    sidx = lambda bb, d, hh: (bb * 2 + d) * MLSTM_HEADS + hh
    gate_cols = lambda d, hh: (d * 2 * MLSTM_HEADS + hh, d * 2 * MLSTM_HEADS + hh + MLSTM_HEADS)

    qk, qc, state = {}, {}, {}
    for key in heads:
        bb, d, hh = key
        q_all, k_all = dirs[bb, d][0], dirs[bb, d][1]
        r = sidx(*key)
        c_s = c_ref[r]
        state[key] = (c_s, n_ref[r], m_ref[r][:, 0:1])
        qk[key] = _dot_nt(q_all[:, hsl(hh)], k_all[:, hsl(hh)])
        qc[key] = _dot(q_all[:, hsl(hh)], c_s.astype(BF16))

    summ = {}
    for key in heads:
        bb, d, hh = key
        _, k_all, _, g, b_all, _, _, _, last = dirs[bb, d]
        ci, cf = gate_cols(d, hh)
        a = b_all[last:last + 1, cf:cf + 1]
        w_log = a - b_all[:, cf:cf + 1] + g[:, ci:ci + 1]
        m_loc = jnp.max(w_log, axis=0, keepdims=True)
        kw = k_all[:, hsl(hh)].astype(F32) * jnp.exp(w_log - m_loc)
        summ[key] = (a, m_loc, kw)
    for key in heads:
        bb, d, hh = key
        v_all = dirs[bb, d][2]
        r = sidx(*key)
        c_s, n_s, m_s = state[key]
        a, m_loc, kw = summ[key]
        c_loc = _dot_tn(kw.astype(BF16), v_all[:, hsl(hh)])
        n_loc = jnp.sum(kw, axis=0, keepdims=True)
        m_new = jnp.maximum(a + m_s, m_loc)
        d_old = jnp.exp(a + m_s - m_new)
        d_new = jnp.exp(m_loc - m_new)
        c_ref[r] = d_old * c_s + d_new * c_loc
        n_ref[r] = d_old * n_s + d_new * n_loc
        m_ref[r] = jnp.broadcast_to(m_new, (1, MLSTM_DH))

    rel, m_rel, sc, inter, den = {}, {}, {}, {}, {}
    for key in heads:
        bb, d, hh = key
        _, _, _, _, _, g_t, b_t, seen, _ = dirs[bb, d]
        ci, cf = gate_cols(d, hh)
        rel[key] = jnp.where(seen, g_t[ci:ci + 1, :] - b_t[cf:cf + 1, :], NEG_INF)
        m_rel[key] = jnp.maximum(state[key][2], jnp.max(rel[key], axis=1, keepdims=True))
    for key in heads:
        m_s = state[key][2]
        m_rel_b = jnp.broadcast_to(m_rel[key], (L, MLSTM_DH))
        sc[key] = qk[key] * jnp.exp(rel[key] - m_rel_b)
        inter[key] = jnp.exp(m_s - m_rel_b)
    for key in heads:
        bb, d, hh = key
        qn = dirs[bb, d][0][:, hsl(hh)].astype(F32) * state[key][1]
        den[key] = jnp.sum(sc[key] + inter[key] * qn, axis=1, keepdims=True)
    for key in heads:
        bb, d, hh = key
        _, _, v_all, _, b_all, _, _, _, _ = dirs[bb, d]
        out_ref = (hf_ref, hb_ref)[d]
        cf = gate_cols(d, hh)[1]
        num = _dot(sc[key].astype(BF16), v_all[:, hsl(hh)]) + inter[key] * qc[key]
        floor = jnp.exp(-(b_all[:, cf:cf + 1] + m_rel[key]))
        out_ref[bb, :, hsl(hh)] = num * (1.0 / jnp.maximum(jnp.abs(den[key]), floor))


def _mlstm(q_lat, k_ctx, v_ctx, g_ctx, k_lat, v_lat, g_lat):
    B, T, _ = q_lat.shape
    L = MLSTM_CHUNK
    nl = T // L
    nc = k_ctx.shape[1] // L
    steps = nc + nl

    fwd_c = lambda s: (0, jnp.minimum(s, nc - 1), 0)
    bwd_c = lambda s: (0, jnp.maximum(nc - 1 - s, 0), 0)
    fwd_l = lambda s: (0, jnp.maximum(s - nc, 0), 0)
    bwd_l = lambda s: (0, jnp.minimum(steps - 1 - s, nl - 1), 0)

    blk = lambda w, im: pl.BlockSpec((B, L, w), im)
    pair = lambda w, f, g: [blk(w, f), blk(w, g)]
    nheads = B * 2 * MLSTM_HEADS
    return pl.pallas_call(
        functools.partial(_mlstm_kernel, n_ctx_chunks=nc),
        grid=(steps,),
        in_specs=(pair(MLSTM_W, fwd_l, bwd_l)
                  + pair(MLSTM_W, fwd_c, bwd_c) + pair(MLSTM_W, fwd_l, bwd_l)
                  + pair(MLSTM_W, fwd_c, bwd_c) + pair(MLSTM_W, fwd_l, bwd_l)
                  + pair(GATE_LANES, fwd_c, bwd_c) + pair(GATE_LANES, fwd_l, bwd_l)),
        out_specs=pair(MLSTM_W, fwd_l, bwd_l),
        out_shape=[jax.ShapeDtypeStruct((B, T, MLSTM_W), F32)] * 2,
        scratch_shapes=[pltpu.VMEM((nheads, MLSTM_DH, MLSTM_DH), F32),
                        pltpu.VMEM((nheads, 1, MLSTM_DH), F32),
                        pltpu.VMEM((nheads, 1, MLSTM_DH), F32)],
        compiler_params=_cparams(("arbitrary",)),
        name="mlstm",
    )(q_lat, q_lat, k_ctx, k_ctx, k_lat, k_lat, v_ctx, v_ctx, v_lat, v_lat,
      g_ctx, g_ctx, g_lat, g_lat)


def _attn_kernel(q_ref, kp_ref, kc_ref, kn_ref, vp_ref, vc_ref, vn_ref, kx_ref, vx_ref,
                 sink_ref, o_ref, *, nb):
    n = pl.program_id(0)
    blk = ATTN_BLOCK
    n_batch = q_ref.shape[0]

    qi = lax.broadcasted_iota(jnp.int32, (blk, blk), 0)
    kj = lax.broadcasted_iota(jnp.int32, (blk, blk), 1)
    ok_prev = (kj >= qi) & (n > 0)
    ok_next = (kj <= qi) & (n < nb - 1)
    lane = lax.broadcasted_iota(jnp.int32, (blk, 2 * ATTN_DH), 1)
    low = lane < ATTN_DH
    n_pairs = ATTN_Q_W // (2 * ATTN_DH)
    n_heads = 2 * n_pairs

    def fold(op, cols):
        groups = [c[:, i:i + blk] for c in cols for i in range(0, c.shape[1], blk)]
        return functools.reduce(op, groups)

    s_all, vcat = [], []
    for bb in range(n_batch):
        kcat = jnp.concatenate([kp_ref[bb], kc_ref[bb], kn_ref[bb], kx_ref[bb]], axis=0)
        vcat.append(jnp.concatenate([vp_ref[bb], vc_ref[bb], vn_ref[bb], vx_ref[bb]], axis=0))
        pieces = []
        for c in range(n_pairs):
            qc = q_ref[bb, :, c * 128:(c + 1) * 128]
            pieces += [jnp.where(low, qc, jnp.zeros_like(qc)), jnp.where(low, jnp.zeros_like(qc), qc)]
        s_all.append(_dot_nt(jnp.concatenate(pieces, axis=0), kcat))

    parts, maxes, probs, dens = {}, {}, {}, {}
    keys = [(bb, j) for bb in range(n_batch) for j in range(n_heads)]
    for bb, j in keys:
        sc = s_all[bb][j * blk:(j + 1) * blk]
        parts[bb, j] = [jnp.where(ok_prev, sc[:, 0:blk], NEG_INF), sc[:, blk:2 * blk],
                        jnp.where(ok_next, sc[:, 2 * blk:3 * blk], NEG_INF), sc[:, 3 * blk:]]
        row_max = jnp.max(fold(jnp.maximum, parts[bb, j]), axis=1, keepdims=True)
        maxes[bb, j] = jnp.maximum(row_max, sink_ref[j:j + 1, 0:1])
    for bb, j in keys:
        exps = [jnp.exp(part - maxes[bb, j]) for part in parts[bb, j]]
        row_sum = jnp.sum(fold(jnp.add, exps), axis=1, keepdims=True)
        dens[bb, j] = row_sum + jnp.exp(sink_ref[j:j + 1, 0:1] - maxes[bb, j])
        probs[bb, j] = jnp.concatenate([p.astype(BF16) for p in exps], axis=1)
    for bb in range(n_batch):
        o_all = _dot(jnp.concatenate([probs[bb, j] for j in range(n_heads)], axis=0), vcat[bb])
        for c in range(n_pairs):
            o_lo = o_all[(2 * c) * blk:(2 * c + 1) * blk] / dens[bb, 2 * c]
            o_hi = o_all[(2 * c + 1) * blk:(2 * c + 2) * blk] / dens[bb, 2 * c + 1]
            o_ref[bb, :, c * 128:(c + 1) * 128] = jnp.where(low, o_lo, o_hi).astype(BF16)


def _attn(aq, ak, av, akx, avx, sink8):
    B, T, _ = aq.shape
    blk = ATTN_BLOCK
    nb = T // blk
    ctx = akx.shape[1]
    kv = lambda im: pl.BlockSpec((B, blk, ATTN_KV_W), im)
    prev = lambda n: (0, jnp.maximum(n - 1, 0), 0)
    cur = lambda n: (0, n, 0)
    nxt = lambda n: (0, jnp.minimum(n + 1, nb - 1), 0)
    cx = pl.BlockSpec((B, ctx, ATTN_KV_W), lambda n: (0, 0, 0))
    return pl.pallas_call(
        functools.partial(_attn_kernel, nb=nb),
        grid=(nb,),
        in_specs=[pl.BlockSpec((B, blk, ATTN_Q_W), cur),
                  kv(prev), kv(cur), kv(nxt), kv(prev), kv(cur), kv(nxt), cx, cx,
                  pl.BlockSpec((8, 128), lambda n: (0, 0))],
        out_specs=pl.BlockSpec((B, blk, ATTN_Q_W), cur),
        out_shape=jax.ShapeDtypeStruct((B, T, ATTN_Q_W), BF16),
        compiler_params=_cparams(("parallel",)),
        name="attn",
    )(aq, ak, ak, ak, av, av, av, akx, avx, sink8)


def _merge_kernel(x_ref, sh_ref, sc_ref, g_ref, g1_ref, hf_ref, hb_ref, a_ref,
                  wg_ref, ng_ref, wbm_ref, wba_ref, wo_ref, o_ref):
    x = x_ref[0]
    h = _modulated(x, g_ref[...], sh_ref[0], sc_ref[0]).astype(BF16)
    pg = _dot(h, wg_ref[...])
    d = x.shape[1]
    hs = hf_ref[0] + hb_ref[0]
    heads = [hs[:, i:i + MLSTM_DH] for i in range(0, MLSTM_W, MLSTM_DH)]
    hn = jnp.concatenate(
        [hh * lax.rsqrt(jnp.mean(hh * hh, axis=1, keepdims=True) + EPS) for hh in heads], axis=1)
    m = hn * ng_ref[...] * _sigmoid(pg[:, 0:MLSTM_W])
    mm = _dot(m.astype(BF16), wbm_ref[...])
    aa = _dot(a_ref[0], wba_ref[...])
    y = _sigmoid(pg[:, MLSTM_W:MLSTM_W + d]) * mm + _sigmoid(pg[:, MLSTM_W + d:]) * aa
    o_ref[0] = x + g1_ref[0] * _dot(y.astype(BF16), wo_ref[...])


def _merge(x, shift, scale, g, g1, hf, hb, a, wg, ng, wbm, wba, wo, *, tm):
    B, T, D = x.shape
    full = lambda arr: pl.BlockSpec(arr.shape, lambda b, t: (0,) * arr.ndim)
    tok = lambda w: pl.BlockSpec((1, tm, w), lambda b, t: (b, t, 0))
    perb = pl.BlockSpec((1, 1, D), lambda b, t: (b, 0, 0))
    return pl.pallas_call(
        _merge_kernel,
        grid=(B, T // tm),
        in_specs=[tok(D), perb, perb, full(g), perb, tok(MLSTM_W), tok(MLSTM_W), tok(ATTN_Q_W),
                  full(wg), full(ng), full(wbm), full(wba), full(wo)],
        out_specs=tok(D),
        out_shape=jax.ShapeDtypeStruct((B, T, D), F32),
        compiler_params=_cparams(("parallel", "parallel")),
        name="merge",
    )(x, shift, scale, g, g1, hf, hb, a, wg, ng, wbm, wba, wo)


def _pquery_kernel(x_ref, sh_ref, sc_ref, g_ref, wq_ref, sk_ref, h_ref, s_ref):
    x = x_ref[0]
    h = _modulated(x, g_ref[...], sh_ref[0], sc_ref[0]).astype(BF16)
    h_ref[0] = h
    q = _dot(h, wq_ref[...]).astype(BF16)
    for j in range(2 * PEER_HEADS):
        s_ref[0, j] = _dot_nt(sk_ref[j], q[:, j * PEER_HALF:(j + 1) * PEER_HALF])


def _pquery(x, shift, scale, g, wq, sk, *, tm):
    B, T, D = x.shape
    nj = 2 * PEER_HEADS
    full = lambda arr: pl.BlockSpec(arr.shape, lambda b, t: (0,) * arr.ndim)
    perb = pl.BlockSpec((1, 1, D), lambda b, t: (b, 0, 0))
    return pl.pallas_call(
        _pquery_kernel,
        grid=(B, T // tm),
        in_specs=[pl.BlockSpec((1, tm, D), lambda b, t: (b, t, 0)), perb, perb,
                  full(g), full(wq), full(sk)],
        out_specs=[pl.BlockSpec((1, tm, D), lambda b, t: (b, t, 0)),
                   pl.BlockSpec((1, nj, PEER_N_KEYS, tm), lambda b, t: (b, 0, 0, t))],
        out_shape=[jax.ShapeDtypeStruct((B, T, D), BF16),
                   jax.ShapeDtypeStruct((B, nj, PEER_N_KEYS, T), F32)],
        compiler_params=_cparams(("parallel", "parallel")),
        name="pquery",
    )(x, shift, scale, g, wq, sk)


def _top_values(val, k):
    tops = []
    for _ in range(k):
        m = jnp.max(val, axis=0, keepdims=True)
        tops.append(m)
        val = jnp.where(val == m, NEG_INF, val)
    return tops


_GELU_K0 = -2.0 * math.sqrt(2.0 / math.pi) * math.log2(math.e)
_GELU_K1 = 0.044715 * _GELU_K0


def _gelu_tanh(x, k0, k1):
    one = jnp.asarray(1.0, x.dtype)
    return x / (one + jnp.exp2(x * (x * x * k1 + k0)))


BF16_ROWS = 16


def _rows_bf16(row, tm):
    tile = jnp.broadcast_to(row, (BF16_ROWS, tm)).astype(BF16)
    return jnp.concatenate([tile] * (PEER_N_KEYS // BF16_ROWS), axis=0)


def _row_tile_bf16(ref, h, i, tm):
    return _rows_bf16(ref[h, pl.ds(i, 1), :], tm)


def _batcher_pairs(n):
    pairs = []
    p = 1
    while p < n:
        k = p
        while k >= 1:
            for j in range(k % p, n - k, 2 * k):
                for i in range(min(k, n - j - k)):
                    if (i + j) // (2 * p) == (i + j + k) // (2 * p):
                        pairs.append((i + j, i + j + k))
            k //= 2
        p *= 2
    return pairs


def _compare_exchange(rows, i, j):
    rows[i], rows[j] = jnp.maximum(rows[i], rows[j]), jnp.minimum(rows[i], rows[j])


def _sorted_top(x, k):
    sub = 8
    assert x.shape[0] == sub * k and k & (k - 1) == 0
    rows = [x[sub * r:sub * (r + 1), :] for r in range(k)]
    for i, j in _batcher_pairs(k):
        _compare_exchange(rows, i, j)
    shift = sub // 2
    while shift >= 1:
        other = [pltpu.roll(r, shift, 0) for r in rows]
        rows = [jnp.maximum(rows[r], other[k - 1 - r]) for r in range(k)]
        d = k // 2
        while d >= 1:
            for i in range(k):
                if i & d == 0:
                    _compare_exchange(rows, i, i + d)
            d //= 2
        shift //= 2
    return rows


def _prefix_count(pred, vals):
    k = len(vals)
    bits, steps = [], []
    s = k // 2
    while s >= 1:
        def pivot(i, base):
            if i == len(bits):
                return vals[base + s - 1]
            return jnp.where(bits[i], pivot(i + 1, base + steps[i]), pivot(i + 1, base))
        bits.append(pred(pivot(0, 0)))
        steps.append(s)
        s //= 2
    n = functools.reduce(jnp.add, [jnp.where(b, float(st), 0.0) for b, st in zip(bits, steps)])
    return jnp.where(pred(vals[k - 1]), float(k), n)


def _peer_router(s_ref, rank_ref, cw_ref, cnt_ref, e2_ref):
    k = PEER_TOPK
    for h in range(PEER_HEADS):
        s1 = s_ref[0, 2 * h]
        s2 = s_ref[0, 2 * h + 1]
        a = [r[0:1, :] for r in _sorted_top(s1, k)]
        b = [r[0:1, :] for r in _sorted_top(s2, k)]
        cand = [a[p] + b[q] for p in range(k) for q in range(k) if (p + 1) * (q + 1) <= k]
        cv = jnp.concatenate(cand, axis=0)
        thr = _top_values(cv, k)[k - 1]
        top = a[0] + b[0]
        z = jnp.sum(jnp.where(cv >= thr, jnp.exp(cv - top), 0.0), axis=0, keepdims=True)
        rank = _prefix_count(lambda v: v > s1, a)
        cnt = _prefix_count(lambda v: v + s2 >= thr, a)
        rank_ref[h] = rank
        cw_ref[h] = jnp.exp(s1 - a[0]) / z
        cnt_ref[h] = cnt.astype(BF16)
        e2_ref[h] = jnp.exp(s2 - b[0]).astype(BF16)


def _peer_kernel(h_ref, s_ref, u_ref, vt_ref, x_ref, g2_ref, gk_ref, o_ref,
                 acc_ref, rank_ref, cw_ref, cnt_ref, e2_ref, wt_ref, *, eb, sb):
    e = pl.program_id(2)
    ne = pl.num_programs(2)
    tm = h_ref.shape[1]

    @pl.when(e == 0)
    def _():
        acc_ref[...] = jnp.zeros_like(acc_ref)
        _peer_router(s_ref, rank_ref, cw_ref, cnt_ref, e2_ref)

    hq = h_ref[0]
    n_i = sb // PEER_N_KEYS
    n_sub = eb // sb
    scores = lambda j: _dot_nt(u_ref[j * sb:(j + 1) * sb, :], hq)
    zero = jnp.zeros((), BF16)
    gk0 = _rows_bf16(gk_ref[0:1, :], tm)
    gk1 = _rows_bf16(gk_ref[1:2, :], tm)
    def output_product(j):
        acc_ref[...] += _dot(vt_ref[:, j * sb:(j + 1) * sb], wt_ref[j % 2])

    acts = {j: scores(j) for j in range(min(2, n_sub))}
    for j in range(n_sub):
        act = acts.pop(j)
        for ii in range(n_i):
            if ii == n_i // 2 and j > 0:
                output_product(j - 1)
            i = e * (eb // PEER_N_KEYS) + j * n_i + ii
            w = None
            for h in range(PEER_HEADS):
                rank_i = _row_tile_bf16(rank_ref, h, i, tm)
                cw_i = _row_tile_bf16(cw_ref, h, i, tm)
                term = jnp.where(rank_i < cnt_ref[h], e2_ref[h], zero) * cw_i
                w = term if w is None else w + term
            act_i = act[ii * PEER_N_KEYS:(ii + 1) * PEER_N_KEYS].astype(BF16)
            wt_ref[j % 2, ii * PEER_N_KEYS:(ii + 1) * PEER_N_KEYS, :] = w * _gelu_tanh(act_i, gk0, gk1)
        if j + 2 < n_sub:
            acts[j + 2] = scores(j + 2)
    output_product(n_sub - 1)

    @pl.when(e == ne - 1)
    def _():
        o_ref[0] = x_ref[0] + g2_ref[0] * acc_ref[...].T


def _peer(h2, s_t, u, vt, x1, g2, *, tm, eb, sb):
    B, T, D = x1.shape
    ne = u.shape[0] // eb
    nj = 2 * PEER_HEADS
    gelu_k = jnp.broadcast_to(jnp.array([[_GELU_K0], [_GELU_K1]], F32), (2, tm))
    return pl.pallas_call(
        functools.partial(_peer_kernel, eb=eb, sb=sb),
        grid=(B, T // tm, ne),
        in_specs=[pl.BlockSpec((1, tm, D), lambda b, t, e: (b, t, 0)),
                  pl.BlockSpec((1, nj, PEER_N_KEYS, tm), lambda b, t, e: (b, 0, 0, t)),
                  pl.BlockSpec((eb, D), lambda b, t, e: (e, 0)),
                  pl.BlockSpec((D, eb), lambda b, t, e: (0, e)),
                  pl.BlockSpec((1, tm, D), lambda b, t, e: (b, t, 0)),
                  pl.BlockSpec((1, 1, D), lambda b, t, e: (b, 0, 0)),
                  pl.BlockSpec((2, tm), lambda b, t, e: (0, 0))],
        out_specs=pl.BlockSpec((1, tm, D), lambda b, t, e: (b, t, 0)),
        out_shape=jax.ShapeDtypeStruct((B, T, D), F32),
        scratch_shapes=[pltpu.VMEM((D, tm), F32),
                        pltpu.VMEM((PEER_HEADS, PEER_N_KEYS, tm), F32),
                        pltpu.VMEM((PEER_HEADS, PEER_N_KEYS, tm), F32),
                        pltpu.VMEM((PEER_HEADS, PEER_N_KEYS, tm), BF16),
                        pltpu.VMEM((PEER_HEADS, PEER_N_KEYS, tm), BF16),
                        pltpu.VMEM((2, sb, tm), BF16)],
        compiler_params=_cparams(("parallel", "parallel", "arbitrary")),
        name="peer",
    )(h2, s_t, u, vt, x1, g2, gelu_k)


_Q_HEAD_ORDER = (0, 4, 1, 5, 2, 6, 3, 7)


def _block_diag_ones(width, seg):
    i = jnp.arange(width)
    return (i[:, None] // seg == i[None, :] // seg).astype(BF16)


def _rope_tables(T):
    rows = T // GRID_W
    nf = ATTN_DH // 4
    row = jnp.broadcast_to(jnp.arange(rows)[:, None], (rows, GRID_W)).reshape(T)
    col = jnp.broadcast_to(jnp.arange(GRID_W)[None, :], (rows, GRID_W)).reshape(T)
    inv = ROPE_BASE ** (-jnp.arange(nf, dtype=F32) / nf)
    ang = jnp.concatenate([row[:, None].astype(F32) * inv, col[:, None].astype(F32) * inv], axis=-1)
    cos, sin = jnp.cos(ang), jnp.sin(ang)
    cos_t = jnp.concatenate([cos, cos] * ATTN_KV_HEADS, axis=-1)
    sin_t = jnp.concatenate([-sin, sin] * ATTN_KV_HEADS, axis=-1)
    return cos_t, sin_t


def _layer(x, ctx, mod, norm_mix_g, norm_ffn_g, w_in, b_mgates, mlstm_norm_g, attn_q_norm_g,
           attn_k_norm_g, attn_sink, w_branch_m, w_branch_a, w_out, peer_w_query, peer_sub_keys,
           peer_u, peer_v, cos_t, sin_t):
    B, T, D = x.shape
    n_ctx = ctx.shape[1]
    mods = jnp.split(mod, 6, axis=-1)
    lat = [m[0:B].reshape(B, 1, D) for m in mods]
    cxm = [jnp.broadcast_to(m[B:B + 1].reshape(1, 1, D), (B, 1, D)) for m in mods]
    sh1, sc1, g1, sh2, sc2, g2 = lat

    W = MLSTM_W
    o0 = 4 * W
    o1 = o0 + 2 * 2 * MLSTM_HEADS
    w_mq, w_mk, w_mv, w_mo = (w_in[:, i * W:(i + 1) * W] for i in range(4))
    w_g16 = w_in[:, o0:o1]
    w_aq = w_in[:, o1:o1 + ATTN_Q_W].reshape(D, ATTN_Q_HEADS, ATTN_DH)
    w_aq = w_aq[:, jnp.array(_Q_HEAD_ORDER), :].reshape(D, ATTN_Q_W)
    o2 = o1 + ATTN_Q_W
    w_ak = w_in[:, o2:o2 + ATTN_KV_W]
    w_av = w_in[:, o2 + ATTN_KV_W:o2 + 2 * ATTN_KV_W]
    w_gates = w_in[:, o2 + 2 * ATTN_KV_W:]
    w_g128 = jnp.pad(w_g16, ((0, 0), (0, GATE_LANES - w_g16.shape[1])))
    w_pack = jnp.concatenate([w_mq, w_mk, w_mv, w_aq, w_ak, w_av, w_g128], axis=1).astype(BF16)
    w_merge_in = jnp.concatenate([w_mo, w_gates], axis=1).astype(BF16)
    bg = jnp.pad(b_mgates, (0, GATE_LANES - b_mgates.shape[0])).reshape(1, GATE_LANES)
    qg = jnp.tile(attn_q_norm_g, ATTN_Q_HEADS).reshape(1, ATTN_Q_W)
    kg = jnp.tile(attn_k_norm_g, ATTN_KV_HEADS).reshape(1, ATTN_KV_W)
    bdq = _block_diag_ones(ATTN_Q_W, ATTN_DH)
    bdk = _block_diag_ones(ATTN_KV_W, ATTN_DH)
    gmix = norm_mix_g.reshape(1, D)

    mq, mk, mv, mg, aq, ak, av = _proj(x, sh1, sc1, gmix, w_pack, bg, qg, kg, bdq, bdk,
                                       cos_t, sin_t, rope=True, tm=min(TOKENS_PROJ, T))
    _, mkc, mvc, mgc, _, akc, avc = _proj(ctx, cxm[0], cxm[1], gmix, w_pack, bg, qg, kg, bdq, bdk,
                                          cos_t, sin_t, rope=False, tm=n_ctx)

    hf, hb = _mlstm(mq, mkc, mvc, mgc, mk, mv, mg)

    sink8 = jnp.broadcast_to(attn_sink[jnp.array(_Q_HEAD_ORDER)].astype(F32)[:, None], (8, 128))
    a = _attn(aq, ak, av, akc, avc, sink8)

    wba = w_branch_a.reshape(ATTN_Q_HEADS, ATTN_DH, D)[jnp.array(_Q_HEAD_ORDER)]
    wba = wba.reshape(ATTN_Q_W, D).astype(BF16)
    x1 = _merge(x, sh1, sc1, gmix, g1, hf, hb, a, w_merge_in, mlstm_norm_g.reshape(1, MLSTM_W),
                w_branch_m.astype(BF16), wba, w_out.astype(BF16), tm=min(TOKENS_MERGE, T))

    sk = peer_sub_keys.reshape(2 * PEER_HEADS, PEER_N_KEYS, PEER_HALF).astype(BF16)
    h2, s_t = _pquery(x1, sh2, sc2, norm_ffn_g.reshape(1, D), peer_w_query.astype(BF16), sk,
                      tm=min(TOKENS_PQUERY, T))
    u = peer_u.astype(BF16)
    vt = peer_v.T.astype(BF16)
    return _peer(h2, s_t, u, vt, x1, g2, tm=min(TOKENS_PEER, T), eb=PEER_EXPERT_BLOCK,
                 sb=PEER_SUB_BLOCK)


def kernel(x, c, ctx, c_ctx, w_ada, b_ada, norm_mix_g, norm_ffn_g, w_in, b_mgates, mlstm_norm_g,
           attn_q_norm_g, attn_k_norm_g, attn_sink, w_branch_m, w_branch_a, w_out, peer_w_query,
           peer_sub_keys, peer_u, peer_v):
    B, T, D = x.shape
    depth = w_ada.shape[0]
    assert depth == 1, "context-stream update for deeper stacks is not implemented"
    cos_t, sin_t = _rope_tables(T)
    cond8 = jnp.zeros((8, D), F32).at[0:B].set(c).at[B].set(c_ctx)
    for layer in range(depth):
        mod = _adaln(cond8, w_ada[layer], b_ada[layer])
        x = _layer(x, ctx, mod, norm_mix_g[layer], norm_ffn_g[layer], w_in[layer], b_mgates[layer],
                   mlstm_norm_g[layer], attn_q_norm_g[layer], attn_k_norm_g[layer], attn_sink[layer],
                   w_branch_m[layer], w_branch_a[layer], w_out[layer], peer_w_query[layer],
                   peer_sub_keys[layer], peer_u[layer], peer_v[layer], cos_t, sin_t)
    return x
```

```python
import functools
import math

import jax
import jax.numpy as jnp
from jax import lax
from jax.experimental import pallas as pl
from jax.experimental.pallas import tpu as pltpu

F32 = jnp.float32
BF16 = jnp.bfloat16

EPS = 1e-6
GRID_W = 64
ROPE_BASE = 10000.0

MLSTM_HEADS = 4
MLSTM_DH = 128
MLSTM_W = MLSTM_HEADS * MLSTM_DH
MLSTM_CHUNK = 128
GATE_LANES = 128

ATTN_Q_HEADS = 8
ATTN_KV_HEADS = 2
ATTN_DH = 64
ATTN_Q_W = ATTN_Q_HEADS * ATTN_DH
ATTN_KV_W = ATTN_KV_HEADS * ATTN_DH
ATTN_BLOCK = 128

PEER_HEADS = 8
PEER_N_KEYS = 128
PEER_HALF = 128
PEER_TOPK = 16

VMEM_LIMIT_BYTES = 56 * 1024 * 1024

TOKENS_PROJ = 1024
TOKENS_MERGE = 512
TOKENS_PQUERY = 1024
TOKENS_PEER = 512
PEER_EXPERT_BLOCK = 2048
PEER_SUB_BLOCK = 1024

NEG_INF = float("-inf")


def _cparams(sem):
    return pltpu.CompilerParams(dimension_semantics=sem, vmem_limit_bytes=VMEM_LIMIT_BYTES)


def _dot(a, b):
    return jnp.dot(a, b, preferred_element_type=F32)


def _dot_nt(a, b):
    return lax.dot_general(a, b, (((1,), (1,)), ((), ())), preferred_element_type=F32)


def _dot_tn(a, b):
    return lax.dot_general(a, b, (((0,), (0,)), ((), ())), preferred_element_type=F32)


def _sigmoid(x):
    return 1.0 / (1.0 + jnp.exp(-x))


def _segsum(x2, bd):
    hi = x2.astype(BF16)
    lo = (x2 - hi.astype(F32)).astype(BF16)
    return _dot(hi, bd) + _dot(lo, bd)


def _modulated(x, g, shift, scale):
    ms = jnp.mean(x * x, axis=-1, keepdims=True)
    return (x * lax.rsqrt(ms + EPS)) * g * (1.0 + scale) + shift


def _adaln_kernel(c_ref, w_ref, b_ref, o_ref):
    c = c_ref[...]
    s = c * _sigmoid(c)
    o_ref[...] = jnp.dot(s, w_ref[...], preferred_element_type=F32,
                         precision=lax.Precision.HIGHEST) + b_ref[...]


def _adaln(cond8, w, b):
    d = cond8.shape[1]
    n = w.shape[1]
    bn = n // 6
    return pl.pallas_call(
        _adaln_kernel,
        grid=(n // bn,),
        in_specs=[pl.BlockSpec((8, d), lambda j: (0, 0)),
                  pl.BlockSpec((d, bn), lambda j: (0, j)),
                  pl.BlockSpec((1, bn), lambda j: (0, j))],
        out_specs=pl.BlockSpec((8, bn), lambda j: (0, j)),
        out_shape=jax.ShapeDtypeStruct((8, n), F32),
        compiler_params=_cparams(("arbitrary",)),
        name="adaln",
    )(cond8, w, b.reshape(1, n))


_C_MQ, _C_MK, _C_MV = 0, 512, 1024
_C_AQ, _C_AK, _C_AV, _C_MG = 1536, 2048, 2176, 2304
_C_END = 2432


def _rope(t, cos_t, sin_t, width):
    lane = lax.broadcasted_iota(jnp.int32, t.shape, 1)
    lower = (lane % ATTN_DH) < (ATTN_DH // 2)
    swapped = jnp.where(lower, pltpu.roll(t, width - ATTN_DH // 2, 1),
                        pltpu.roll(t, ATTN_DH // 2, 1))
    return t * cos_t + swapped * sin_t


def _proj_kernel(x_ref, sh_ref, sc_ref, g_ref, w_ref, bg_ref, qg_ref, kg_ref,
                 bdq_ref, bdk_ref, cos_ref, sin_ref,
                 mq_ref, mk_ref, mv_ref, mg_ref, aq_ref, ak_ref, av_ref, *, rope):
    x = x_ref[0]
    h = _modulated(x, g_ref[...], sh_ref[0], sc_ref[0]).astype(BF16)

    p = _dot(h, w_ref[:, _C_MQ:_C_AQ])
    mq_ref[0] = p[:, 0:512].astype(BF16)
    mk_ref[0] = (p[:, 512:1024] * (MLSTM_DH ** -0.5)).astype(BF16)
    mv_ref[0] = p[:, 1024:1536].astype(BF16)

    pq = _dot(h, w_ref[:, _C_AQ:_C_AK])
    ssq = _segsum(pq * pq, bdq_ref[...])
    q = pq * lax.rsqrt(ssq * (1.0 / ATTN_DH) + EPS) * qg_ref[...]
    if rope:
        cos_t = cos_ref[...]
        sin_t = sin_ref[...]
        q = _rope(q, jnp.concatenate([cos_t] * 4, axis=1), jnp.concatenate([sin_t] * 4, axis=1),
                  ATTN_Q_W)
    aq_ref[0] = (q * (ATTN_DH ** -0.5)).astype(BF16)

    pk = _dot(h, w_ref[:, _C_AK:_C_AV])
    ssk = _segsum(pk * pk, bdk_ref[...])
    k = pk * lax.rsqrt(ssk * (1.0 / ATTN_DH) + EPS) * kg_ref[...]
    if rope:
        k = _rope(k, cos_ref[...], sin_ref[...], ATTN_KV_W)
    ak_ref[0] = k.astype(BF16)

    av_ref[0] = _dot(h, w_ref[:, _C_AV:_C_MG]).astype(BF16)

    z = _dot(h, w_ref[:, _C_MG:_C_END]) + bg_ref[...]
    lane = lax.broadcasted_iota(jnp.int32, z.shape, 1)
    is_forget = (lane % 8) >= MLSTM_HEADS
    logsig = jnp.minimum(z, 0.0) - jnp.log1p(jnp.exp(-jnp.abs(z)))
    mg_ref[0] = jnp.where(is_forget, logsig, z)


def _proj(x, shift, scale, g, w_pack, bg, qg, kg, bdq, bdk, cos_t, sin_t, *, rope, tm):
    B, T, D = x.shape
    nt = T // tm
    full = lambda shape: pl.BlockSpec(shape, lambda b, t: (0,) * len(shape))
    tok = lambda w: pl.BlockSpec((1, tm, w), lambda b, t: (b, t, 0))
    outs = [(MLSTM_W, BF16), (MLSTM_W, BF16), (MLSTM_W, BF16), (GATE_LANES, F32),
            (ATTN_Q_W, BF16), (ATTN_KV_W, BF16), (ATTN_KV_W, BF16)]
    return pl.pallas_call(
        functools.partial(_proj_kernel, rope=rope),
        grid=(B, nt),
        in_specs=[tok(D),
                  pl.BlockSpec((1, 1, D), lambda b, t: (b, 0, 0)),
                  pl.BlockSpec((1, 1, D), lambda b, t: (b, 0, 0)),
                  full((1, D)), full(w_pack.shape), full((1, GATE_LANES)),
                  full((1, ATTN_Q_W)), full((1, ATTN_KV_W)),
                  full((ATTN_Q_W, ATTN_Q_W)), full((ATTN_KV_W, ATTN_KV_W)),
                  pl.BlockSpec((tm, ATTN_KV_W), lambda b, t: (t, 0)),
                  pl.BlockSpec((tm, ATTN_KV_W), lambda b, t: (t, 0))],
        out_specs=[tok(w) for w, _ in outs],
        out_shape=[jax.ShapeDtypeStruct((B, T, w), dt) for w, dt in outs],
        compiler_params=_cparams(("parallel", "parallel")),
        name="proj_rope" if rope else "proj_ctx",
    )(x, shift, scale, g, w_pack, bg, qg, kg, bdq, bdk, cos_t, sin_t)


def _mlstm_kernel(qf_ref, qb_ref, kcf_ref, kcb_ref, klf_ref, klb_ref, vcf_ref, vcb_ref,
                  vlf_ref, vlb_ref, gcf_ref, gcb_ref, glf_ref, glb_ref,
                  hf_ref, hb_ref, c_ref, n_ref, m_ref, *, n_ctx_chunks):
    s = pl.program_id(0)
    L = MLSTM_CHUNK
    n_batch = qf_ref.shape[0]
    in_ctx = s < n_ctx_chunks

    @pl.when(s == 0)
    def _():
        c_ref[...] = jnp.zeros_like(c_ref)
        n_ref[...] = jnp.zeros_like(n_ref)
        m_ref[...] = jnp.zeros_like(m_ref)

    row = lax.broadcasted_iota(jnp.int32, (L, L), 0)
    col = lax.broadcasted_iota(jnp.int32, (L, L), 1)
    tri_f = (col <= row).astype(F32)
    pick = lambda c_ref_, l_ref_, bb: jnp.where(in_ctx, c_ref_[bb], l_ref_[bb])


    dirs = {}
    for bb in range(n_batch):
        for d in range(2):
            q_all = (qf_ref, qb_ref)[d][bb]
            k_all = pick((kcf_ref, kcb_ref)[d], (klf_ref, klb_ref)[d], bb)
            v_all = pick((vcf_ref, vcb_ref)[d], (vlf_ref, vlb_ref)[d], bb)
            g = pick((gcf_ref, gcb_ref)[d], (glf_ref, glb_ref)[d], bb)
            cum_f = jnp.dot(tri_f, g, preferred_element_type=F32, precision=lax.Precision.HIGHEST)
            if d == 0:
                b_all, seen, last = cum_f, col <= row, L - 1
            else:
                b_all, seen, last = cum_f[L - 1:L, :] - cum_f + g, col >= row, 0
            dirs[bb, d] = (q_all, k_all, v_all, g, b_all, g.T, b_all.T, seen, last)

    heads = [(bb, d, hh) for bb in range(n_batch) for d in range(2) for hh in range(MLSTM_HEADS)]
    hsl = lambda hh: slice(hh * MLSTM_DH, (hh + 1) * MLSTM_DH)
    sidx = lambda bb, d, hh: (bb * 2 + d) * MLSTM_HEADS + hh
    gate_cols = lambda d, hh: (d * 2 * MLSTM_HEADS + hh, d * 2 * MLSTM_HEADS + hh + MLSTM_HEADS)

    qk, qc, state = {}, {}, {}
    for key in heads:
        bb, d, hh = key
        q_all, k_all = dirs[bb, d][0], dirs[bb, d][1]
        r = sidx(*key)
        c_s = c_ref[r]
        state[key] = (c_s, n_ref[r], m_ref[r][:, 0:1])
        qk[key] = _dot_nt(q_all[:, hsl(hh)], k_all[:, hsl(hh)])
        qc[key] = _dot(q_all[:, hsl(hh)], c_s.astype(BF16))

    summ = {}
    for key in heads:
        bb, d, hh = key
        _, k_all, _, g, b_all, _, _, _, last = dirs[bb, d]
        ci, cf = gate_cols(d, hh)
        a = b_all[last:last + 1, cf:cf + 1]
        w_log = a - b_all[:, cf:cf + 1] + g[:, ci:ci + 1]
        m_loc = jnp.max(w_log, axis=0, keepdims=True)
        kw = k_all[:, hsl(hh)].astype(F32) * jnp.exp(w_log - m_loc)
        summ[key] = (a, m_loc, kw)
    for key in heads:
        bb, d, hh = key
        v_all = dirs[bb, d][2]
        r = sidx(*key)
        c_s, n_s, m_s = state[key]
        a, m_loc, kw = summ[key]
        c_loc = _dot_tn(kw.astype(BF16), v_all[:, hsl(hh)])
        n_loc = jnp.sum(kw, axis=0, keepdims=True)
        m_new = jnp.maximum(a + m_s, m_loc)
        d_old = jnp.exp(a + m_s - m_new)
        d_new = jnp.exp(m_loc - m_new)
        c_ref[r] = d_old * c_s + d_new * c_loc
        n_ref[r] = d_old * n_s + d_new * n_loc
        m_ref[r] = jnp.broadcast_to(m_new, (1, MLSTM_DH))

    rel, m_rel, sc, inter, den = {}, {}, {}, {}, {}
    for key in heads:
        bb, d, hh = key
        _, _, _, _, _, g_t, b_t, seen, _ = dirs[bb, d]
        ci, cf = gate_cols(d, hh)
        rel[key] = jnp.where(seen, g_t[ci:ci + 1, :] - b_t[cf:cf + 1, :], NEG_INF)
        m_rel[key] = jnp.maximum(state[key][2], jnp.max(rel[key], axis=1, keepdims=True))
    for key in heads:
        m_s = state[key][2]
        m_rel_b = jnp.broadcast_to(m_rel[key], (L, MLSTM_DH))
        sc[key] = qk[key] * jnp.exp(rel[key] - m_rel_b)
        inter[key] = jnp.exp(m_s - m_rel_b)
    for key in heads:
        bb, d, hh = key
        qn = dirs[bb, d][0][:, hsl(hh)].astype(F32) * state[key][1]
        den[key] = jnp.sum(sc[key] + inter[key] * qn, axis=1, keepdims=True)
    for key in heads:
        bb, d, hh = key
        _, _, v_all, _, b_all, _, _, _, _ = dirs[bb, d]
        out_ref = (hf_ref, hb_ref)[d]
        cf = gate_cols(d, hh)[1]
        num = _dot(sc[key].astype(BF16), v_all[:, hsl(hh)]) + inter[key] * qc[key]
        floor = jnp.exp(-(b_all[:, cf:cf + 1] + m_rel[key]))
        out_ref[bb, :, hsl(hh)] = num * (1.0 / jnp.maximum(jnp.abs(den[key]), floor))


def _mlstm(q_lat, k_ctx, v_ctx, g_ctx, k_lat, v_lat, g_lat):
    B, T, _ = q_lat.shape
    L = MLSTM_CHUNK
    nl = T // L
    nc = k_ctx.shape[1] // L
    steps = nc + nl

    fwd_c = lambda s: (0, jnp.minimum(s, nc - 1), 0)
    bwd_c = lambda s: (0, jnp.maximum(nc - 1 - s, 0), 0)
    fwd_l = lambda s: (0, jnp.maximum(s - nc, 0), 0)
    bwd_l = lambda s: (0, jnp.minimum(steps - 1 - s, nl - 1), 0)

    blk = lambda w, im: pl.BlockSpec((B, L, w), im)
    pair = lambda w, f, g: [blk(w, f), blk(w, g)]
    nheads = B * 2 * MLSTM_HEADS
    return pl.pallas_call(
        functools.partial(_mlstm_kernel, n_ctx_chunks=nc),
        grid=(steps,),
        in_specs=(pair(MLSTM_W, fwd_l, bwd_l)
                  + pair(MLSTM_W, fwd_c, bwd_c) + pair(MLSTM_W, fwd_l, bwd_l)
                  + pair(MLSTM_W, fwd_c, bwd_c) + pair(MLSTM_W, fwd_l, bwd_l)
                  + pair(GATE_LANES, fwd_c, bwd_c) + pair(GATE_LANES, fwd_l, bwd_l)),
        out_specs=pair(MLSTM_W, fwd_l, bwd_l),
        out_shape=[jax.ShapeDtypeStruct((B, T, MLSTM_W), F32)] * 2,
        scratch_shapes=[pltpu.VMEM((nheads, MLSTM_DH, MLSTM_DH), F32),
                        pltpu.VMEM((nheads, 1, MLSTM_DH), F32),
                        pltpu.VMEM((nheads, 1, MLSTM_DH), F32)],
        compiler_params=_cparams(("arbitrary",)),
        name="mlstm",
    )(q_lat, q_lat, k_ctx, k_ctx, k_lat, k_lat, v_ctx, v_ctx, v_lat, v_lat,
      g_ctx, g_ctx, g_lat, g_lat)


def _attn_kernel(q_ref, kp_ref, kc_ref, kn_ref, vp_ref, vc_ref, vn_ref, kx_ref, vx_ref,
                 sink_ref, o_ref, *, nb):
    n = pl.program_id(0)
    blk = ATTN_BLOCK
    n_batch = q_ref.shape[0]

    qi = lax.broadcasted_iota(jnp.int32, (blk, blk), 0)
    kj = lax.broadcasted_iota(jnp.int32, (blk, blk), 1)
    ok_prev = (kj >= qi) & (n > 0)
    ok_next = (kj <= qi) & (n < nb - 1)
    lane = lax.broadcasted_iota(jnp.int32, (blk, 2 * ATTN_DH), 1)
    low = lane < ATTN_DH
    n_pairs = ATTN_Q_W // (2 * ATTN_DH)
    n_heads = 2 * n_pairs

    def fold(op, cols):
        groups = [c[:, i:i + blk] for c in cols for i in range(0, c.shape[1], blk)]
        return functools.reduce(op, groups)

    s_all, vcat = [], []
    for bb in range(n_batch):
        kcat = jnp.concatenate([kp_ref[bb], kc_ref[bb], kn_ref[bb], kx_ref[bb]], axis=0)
        vcat.append(jnp.concatenate([vp_ref[bb], vc_ref[bb], vn_ref[bb], vx_ref[bb]], axis=0))
        pieces = []
        for c in range(n_pairs):
            qc = q_ref[bb, :, c * 128:(c + 1) * 128]
            pieces += [jnp.where(low, qc, jnp.zeros_like(qc)), jnp.where(low, jnp.zeros_like(qc), qc)]
        s_all.append(_dot_nt(jnp.concatenate(pieces, axis=0), kcat))

    parts, maxes, probs, dens = {}, {}, {}, {}
    keys = [(bb, j) for bb in range(n_batch) for j in range(n_heads)]
    for bb, j in keys:
        sc = s_all[bb][j * blk:(j + 1) * blk]
        parts[bb, j] = [jnp.where(ok_prev, sc[:, 0:blk], NEG_INF), sc[:, blk:2 * blk],
                        jnp.where(ok_next, sc[:, 2 * blk:3 * blk], NEG_INF), sc[:, 3 * blk:]]
        row_max = jnp.max(fold(jnp.maximum, parts[bb, j]), axis=1, keepdims=True)
        maxes[bb, j] = jnp.maximum(row_max, sink_ref[j:j + 1, 0:1])
    for bb, j in keys:
        exps = [jnp.exp(part - maxes[bb, j]) for part in parts[bb, j]]
        row_sum = jnp.sum(fold(jnp.add, exps), axis=1, keepdims=True)
        dens[bb, j] = row_sum + jnp.exp(sink_ref[j:j + 1, 0:1] - maxes[bb, j])
        probs[bb, j] = jnp.concatenate([p.astype(BF16) for p in exps], axis=1)
    for bb in range(n_batch):
        o_all = _dot(jnp.concatenate([probs[bb, j] for j in range(n_heads)], axis=0), vcat[bb])
        for c in range(n_pairs):
            o_lo = o_all[(2 * c) * blk:(2 * c + 1) * blk] / dens[bb, 2 * c]
            o_hi = o_all[(2 * c + 1) * blk:(2 * c + 2) * blk] / dens[bb, 2 * c + 1]
            o_ref[bb, :, c * 128:(c + 1) * 128] = jnp.where(low, o_lo, o_hi).astype(BF16)


def _attn(aq, ak, av, akx, avx, sink8):
    B, T, _ = aq.shape
    blk = ATTN_BLOCK
    nb = T // blk
    ctx = akx.shape[1]
    kv = lambda im: pl.BlockSpec((B, blk, ATTN_KV_W), im)
    prev = lambda n: (0, jnp.maximum(n - 1, 0), 0)
    cur = lambda n: (0, n, 0)
    nxt = lambda n: (0, jnp.minimum(n + 1, nb - 1), 0)
    cx = pl.BlockSpec((B, ctx, ATTN_KV_W), lambda n: (0, 0, 0))
    return pl.pallas_call(
        functools.partial(_attn_kernel, nb=nb),
        grid=(nb,),
        in_specs=[pl.BlockSpec((B, blk, ATTN_Q_W), cur),
                  kv(prev), kv(cur), kv(nxt), kv(prev), kv(cur), kv(nxt), cx, cx,
                  pl.BlockSpec((8, 128), lambda n: (0, 0))],
        out_specs=pl.BlockSpec((B, blk, ATTN_Q_W), cur),
        out_shape=jax.ShapeDtypeStruct((B, T, ATTN_Q_W), BF16),
        compiler_params=_cparams(("parallel",)),
        name="attn",
    )(aq, ak, ak, ak, av, av, av, akx, avx, sink8)


def _merge_kernel(x_ref, sh_ref, sc_ref, g_ref, g1_ref, hf_ref, hb_ref, a_ref,
                  wg_ref, ng_ref, wbm_ref, wba_ref, wo_ref, o_ref):
    x = x_ref[0]
    h = _modulated(x, g_ref[...], sh_ref[0], sc_ref[0]).astype(BF16)
    pg = _dot(h, wg_ref[...])
    d = x.shape[1]
    hs = hf_ref[0] + hb_ref[0]
    heads = [hs[:, i:i + MLSTM_DH] for i in range(0, MLSTM_W, MLSTM_DH)]
    hn = jnp.concatenate(
        [hh * lax.rsqrt(jnp.mean(hh * hh, axis=1, keepdims=True) + EPS) for hh in heads], axis=1)
    m = hn * ng_ref[...] * _sigmoid(pg[:, 0:MLSTM_W])
    mm = _dot(m.astype(BF16), wbm_ref[...])
    aa = _dot(a_ref[0], wba_ref[...])
    y = _sigmoid(pg[:, MLSTM_W:MLSTM_W + d]) * mm + _sigmoid(pg[:, MLSTM_W + d:]) * aa
    o_ref[0] = x + g1_ref[0] * _dot(y.astype(BF16), wo_ref[...])


def _merge(x, shift, scale, g, g1, hf, hb, a, wg, ng, wbm, wba, wo, *, tm):
    B, T, D = x.shape
    full = lambda arr: pl.BlockSpec(arr.shape, lambda b, t: (0,) * arr.ndim)
    tok = lambda w: pl.BlockSpec((1, tm, w), lambda b, t: (b, t, 0))
    perb = pl.BlockSpec((1, 1, D), lambda b, t: (b, 0, 0))
    return pl.pallas_call(
        _merge_kernel,
        grid=(B, T // tm),
        in_specs=[tok(D), perb, perb, full(g), perb, tok(MLSTM_W), tok(MLSTM_W), tok(ATTN_Q_W),
                  full(wg), full(ng), full(wbm), full(wba), full(wo)],
        out_specs=tok(D),
        out_shape=jax.ShapeDtypeStruct((B, T, D), F32),
        compiler_params=_cparams(("parallel", "parallel")),
        name="merge",
    )(x, shift, scale, g, g1, hf, hb, a, wg, ng, wbm, wba, wo)


def _pquery_kernel(x_ref, sh_ref, sc_ref, g_ref, wq_ref, sk_ref, h_ref, s_ref):
    x = x_ref[0]
    y = _modulated(x, g_ref[...], sh_ref[0], sc_ref[0])
    h = y.astype(BF16)
    h_ref[0] = y.T.astype(BF16)
    q = _dot(h, wq_ref[...]).astype(BF16)
    for j in range(2 * PEER_HEADS):
        s_ref[0, j] = _dot_nt(sk_ref[j], q[:, j * PEER_HALF:(j + 1) * PEER_HALF])


def _pquery(x, shift, scale, g, wq, sk, *, tm):
    B, T, D = x.shape
    nj = 2 * PEER_HEADS
    full = lambda arr: pl.BlockSpec(arr.shape, lambda b, t: (0,) * arr.ndim)
    perb = pl.BlockSpec((1, 1, D), lambda b, t: (b, 0, 0))
    return pl.pallas_call(
        _pquery_kernel,
        grid=(B, T // tm),
        in_specs=[pl.BlockSpec((1, tm, D), lambda b, t: (b, t, 0)), perb, perb,
                  full(g), full(wq), full(sk)],
        out_specs=[pl.BlockSpec((1, D, tm), lambda b, t: (b, 0, t)),
                   pl.BlockSpec((1, nj, PEER_N_KEYS, tm), lambda b, t: (b, 0, 0, t))],
        out_shape=[jax.ShapeDtypeStruct((B, D, T), BF16),
                   jax.ShapeDtypeStruct((B, nj, PEER_N_KEYS, T), F32)],
        compiler_params=_cparams(("parallel", "parallel")),
        name="pquery",
    )(x, shift, scale, g, wq, sk)


def _top_values(val, k):
    tops = []
    for _ in range(k):
        m = jnp.max(val, axis=0, keepdims=True)
        tops.append(m)
        val = jnp.where(val == m, NEG_INF, val)
    return tops


_GELU_K0 = -2.0 * math.sqrt(2.0 / math.pi) * math.log2(math.e)
_GELU_K1 = 0.044715 * _GELU_K0


def _gelu_tanh(x, k0, k1):
    one = jnp.asarray(1.0, x.dtype)
    return x / (one + jnp.exp2(x * (x * x * k1 + k0)))


BF16_ROWS = 16


def _rows_bf16(row, tm):
    tile = jnp.broadcast_to(row, (BF16_ROWS, tm)).astype(BF16)
    return jnp.concatenate([tile] * (PEER_N_KEYS // BF16_ROWS), axis=0)


def _row_tile_bf16(ref, h, i, tm):
    return _rows_bf16(ref[h, pl.ds(i, 1), :], tm)


def _batcher_pairs(n):
    pairs = []
    p = 1
    while p < n:
        k = p
        while k >= 1:
            for j in range(k % p, n - k, 2 * k):
                for i in range(min(k, n - j - k)):
                    if (i + j) // (2 * p) == (i + j + k) // (2 * p):
                        pairs.append((i + j, i + j + k))
            k //= 2
        p *= 2
    return pairs


def _compare_exchange(rows, i, j):
    rows[i], rows[j] = jnp.maximum(rows[i], rows[j]), jnp.minimum(rows[i], rows[j])


def _sorted_top(x, k):
    sub = 8
    assert x.shape[0] == sub * k and k & (k - 1) == 0
    rows = [x[sub * r:sub * (r + 1), :] for r in range(k)]
    for i, j in _batcher_pairs(k):
        _compare_exchange(rows, i, j)
    shift = sub // 2
    while shift >= 1:
        other = [pltpu.roll(r, shift, 0) for r in rows]
        rows = [jnp.maximum(rows[r], other[k - 1 - r]) for r in range(k)]
        d = k // 2
        while d >= 1:
            for i in range(k):
                if i & d == 0:
                    _compare_exchange(rows, i, i + d)
            d //= 2
        shift //= 2
    return rows


def _prefix_count(pred, vals):
    k = len(vals)
    bits, steps = [], []
    s = k // 2
    while s >= 1:
        def pivot(i, base):
            if i == len(bits):
                return vals[base + s - 1]
            return jnp.where(bits[i], pivot(i + 1, base + steps[i]), pivot(i + 1, base))
        bits.append(pred(pivot(0, 0)))
        steps.append(s)
        s //= 2
    n = functools.reduce(jnp.add, [jnp.where(b, float(st), 0.0) for b, st in zip(bits, steps)])
    return jnp.where(pred(vals[k - 1]), float(k), n)


def _peer_router(s_ref, rank_ref, cw_ref, cnt_ref, e2_ref):
    k = PEER_TOPK
    for h in range(PEER_HEADS):
        s1 = s_ref[0, 2 * h]
        s2 = s_ref[0, 2 * h + 1]
        a = [r[0:1, :] for r in _sorted_top(s1, k)]
        b = [r[0:1, :] for r in _sorted_top(s2, k)]
        cand = [a[p] + b[q] for p in range(k) for q in range(k) if (p + 1) * (q + 1) <= k]
        cv = jnp.concatenate(cand, axis=0)
        thr = _top_values(cv, k)[k - 1]
        top = a[0] + b[0]
        z = jnp.sum(jnp.where(cv >= thr, jnp.exp(cv - top), 0.0), axis=0, keepdims=True)
        rank = _prefix_count(lambda v: v > s1, a)
        cnt = _prefix_count(lambda v: v + s2 >= thr, a)
        rank_ref[h] = rank
        cw_ref[h] = jnp.exp(s1 - a[0]) / z
        cnt_ref[h] = cnt.astype(BF16)
        e2_ref[h] = jnp.exp(s2 - b[0]).astype(BF16)


def _peer_kernel(h_ref, s_ref, u_ref, vt_ref, x_ref, g2_ref, gk_ref, o_ref,
                 acc_ref, rank_ref, cw_ref, cnt_ref, e2_ref, wt_ref, *, eb, sb):
    e = pl.program_id(2)
    ne = pl.num_programs(2)
    tm = h_ref.shape[2]

    @pl.when(e == 0)
    def _():
        acc_ref[...] = jnp.zeros_like(acc_ref)
        _peer_router(s_ref, rank_ref, cw_ref, cnt_ref, e2_ref)

    hq = h_ref[0]
    n_i = sb // PEER_N_KEYS
    n_sub = eb // sb
    scores = lambda j: _dot(u_ref[j * sb:(j + 1) * sb, :], hq)
    zero = jnp.zeros((), BF16)
    gk0 = _rows_bf16(gk_ref[0:1, :], tm)
    gk1 = _rows_bf16(gk_ref[1:2, :], tm)
    def output_product(j):
        acc_ref[...] += _dot(vt_ref[:, j * sb:(j + 1) * sb], wt_ref[j % 2])

    acts = {j: scores(j) for j in range(min(2, n_sub))}
    for j in range(n_sub):
        act = acts.pop(j)
        for ii in range(n_i):
            if ii == n_i // 2 and j > 0:
                output_product(j - 1)
            i = e * (eb // PEER_N_KEYS) + j * n_i + ii
            w = None
            for h in range(PEER_HEADS):
                rank_i = _row_tile_bf16(rank_ref, h, i, tm)
                cw_i = _row_tile_bf16(cw_ref, h, i, tm)
                term = jnp.where(rank_i < cnt_ref[h], e2_ref[h], zero) * cw_i
                w = term if w is None else w + term
            act_i = act[ii * PEER_N_KEYS:(ii + 1) * PEER_N_KEYS].astype(BF16)
            wt_ref[j % 2, ii * PEER_N_KEYS:(ii + 1) * PEER_N_KEYS, :] = w * _gelu_tanh(act_i, gk0, gk1)
        if j + 2 < n_sub:
            acts[j + 2] = scores(j + 2)
    output_product(n_sub - 1)

    @pl.when(e == ne - 1)
    def _():
        o_ref[0] = x_ref[0] + g2_ref[0] * acc_ref[...].T


def _peer(h2, s_t, u, vt, x1, g2, *, tm, eb, sb):
    B, T, D = x1.shape
    ne = u.shape[0] // eb
    nj = 2 * PEER_HEADS
    gelu_k = jnp.broadcast_to(jnp.array([[_GELU_K0], [_GELU_K1]], F32), (2, tm))
    return pl.pallas_call(
        functools.partial(_peer_kernel, eb=eb, sb=sb),
        grid=(B, T // tm, ne),
        in_specs=[pl.BlockSpec((1, D, tm), lambda b, t, e: (b, 0, t)),
                  pl.BlockSpec((1, nj, PEER_N_KEYS, tm), lambda b, t, e: (b, 0, 0, t)),
                  pl.BlockSpec((eb, D), lambda b, t, e: (e, 0)),
                  pl.BlockSpec((D, eb), lambda b, t, e: (0, e)),
                  pl.BlockSpec((1, tm, D), lambda b, t, e: (b, t, 0)),
                  pl.BlockSpec((1, 1, D), lambda b, t, e: (b, 0, 0)),
                  pl.BlockSpec((2, tm), lambda b, t, e: (0, 0))],
        out_specs=pl.BlockSpec((1, tm, D), lambda b, t, e: (b, t, 0)),
        out_shape=jax.ShapeDtypeStruct((B, T, D), F32),
        scratch_shapes=[pltpu.VMEM((D, tm), F32),
                        pltpu.VMEM((PEER_HEADS, PEER_N_KEYS, tm), F32),
                        pltpu.VMEM((PEER_HEADS, PEER_N_KEYS, tm), F32),
                        pltpu.VMEM((PEER_HEADS, PEER_N_KEYS, tm), BF16),
                        pltpu.VMEM((PEER_HEADS, PEER_N_KEYS, tm), BF16),
                        pltpu.VMEM((2, sb, tm), BF16)],
        compiler_params=_cparams(("parallel", "parallel", "arbitrary")),
        name="peer",
    )(h2, s_t, u, vt, x1, g2, gelu_k)


_Q_HEAD_ORDER = (0, 4, 1, 5, 2, 6, 3, 7)


def _block_diag_ones(width, seg):
    i = jnp.arange(width)
    return (i[:, None] // seg == i[None, :] // seg).astype(BF16)


def _rope_tables(T):
    rows = T // GRID_W
    nf = ATTN_DH // 4
    row = jnp.broadcast_to(jnp.arange(rows)[:, None], (rows, GRID_W)).reshape(T)
    col = jnp.broadcast_to(jnp.arange(GRID_W)[None, :], (rows, GRID_W)).reshape(T)
    inv = ROPE_BASE ** (-jnp.arange(nf, dtype=F32) / nf)
    ang = jnp.concatenate([row[:, None].astype(F32) * inv, col[:, None].astype(F32) * inv], axis=-1)
    cos, sin = jnp.cos(ang), jnp.sin(ang)
    cos_t = jnp.concatenate([cos, cos] * ATTN_KV_HEADS, axis=-1)
    sin_t = jnp.concatenate([-sin, sin] * ATTN_KV_HEADS, axis=-1)
    return cos_t, sin_t


def _layer(x, ctx, mod, norm_mix_g, norm_ffn_g, w_in, b_mgates, mlstm_norm_g, attn_q_norm_g,
           attn_k_norm_g, attn_sink, w_branch_m, w_branch_a, w_out, peer_w_query, peer_sub_keys,
           peer_u, peer_v, cos_t, sin_t):
    B, T, D = x.shape
    n_ctx = ctx.shape[1]
    mods = jnp.split(mod, 6, axis=-1)
    lat = [m[0:B].reshape(B, 1, D) for m in mods]
    cxm = [jnp.broadcast_to(m[B:B + 1].reshape(1, 1, D), (B, 1, D)) for m in mods]
    sh1, sc1, g1, sh2, sc2, g2 = lat

    W = MLSTM_W
    o0 = 4 * W
    o1 = o0 + 2 * 2 * MLSTM_HEADS
    w_mq, w_mk, w_mv, w_mo = (w_in[:, i * W:(i + 1) * W] for i in range(4))
    w_g16 = w_in[:, o0:o1]
    w_aq = w_in[:, o1:o1 + ATTN_Q_W].reshape(D, ATTN_Q_HEADS, ATTN_DH)
    w_aq = w_aq[:, jnp.array(_Q_HEAD_ORDER), :].reshape(D, ATTN_Q_W)
    o2 = o1 + ATTN_Q_W
    w_ak = w_in[:, o2:o2 + ATTN_KV_W]
    w_av = w_in[:, o2 + ATTN_KV_W:o2 + 2 * ATTN_KV_W]
    w_gates = w_in[:, o2 + 2 * ATTN_KV_W:]
    w_g128 = jnp.pad(w_g16, ((0, 0), (0, GATE_LANES - w_g16.shape[1])))
    w_pack = jnp.concatenate([w_mq, w_mk, w_mv, w_aq, w_ak, w_av, w_g128], axis=1).astype(BF16)
    w_merge_in = jnp.concatenate([w_mo, w_gates], axis=1).astype(BF16)
    bg = jnp.pad(b_mgates, (0, GATE_LANES - b_mgates.shape[0])).reshape(1, GATE_LANES)
    qg = jnp.tile(attn_q_norm_g, ATTN_Q_HEADS).reshape(1, ATTN_Q_W)
    kg = jnp.tile(attn_k_norm_g, ATTN_KV_HEADS).reshape(1, ATTN_KV_W)
    bdq = _block_diag_ones(ATTN_Q_W, ATTN_DH)
    bdk = _block_diag_ones(ATTN_KV_W, ATTN_DH)
    gmix = norm_mix_g.reshape(1, D)

    mq, mk, mv, mg, aq, ak, av = _proj(x, sh1, sc1, gmix, w_pack, bg, qg, kg, bdq, bdk,
                                       cos_t, sin_t, rope=True, tm=min(TOKENS_PROJ, T))
    _, mkc, mvc, mgc, _, akc, avc = _proj(ctx, cxm[0], cxm[1], gmix, w_pack, bg, qg, kg, bdq, bdk,
                                          cos_t, sin_t, rope=False, tm=n_ctx)

    hf, hb = _mlstm(mq, mkc, mvc, mgc, mk, mv, mg)

    sink8 = jnp.broadcast_to(attn_sink[jnp.array(_Q_HEAD_ORDER)].astype(F32)[:, None], (8, 128))
    a = _attn(aq, ak, av, akc, avc, sink8)

    wba = w_branch_a.reshape(ATTN_Q_HEADS, ATTN_DH, D)[jnp.array(_Q_HEAD_ORDER)]
    wba = wba.reshape(ATTN_Q_W, D).astype(BF16)
    x1 = _merge(x, sh1, sc1, gmix, g1, hf, hb, a, w_merge_in, mlstm_norm_g.reshape(1, MLSTM_W),
                w_branch_m.astype(BF16), wba, w_out.astype(BF16), tm=min(TOKENS_MERGE, T))

    sk = peer_sub_keys.reshape(2 * PEER_HEADS, PEER_N_KEYS, PEER_HALF).astype(BF16)
    h2, s_t = _pquery(x1, sh2, sc2, norm_ffn_g.reshape(1, D), peer_w_query.astype(BF16), sk,
                      tm=min(TOKENS_PQUERY, T))
    u = peer_u.astype(BF16)
    vt = peer_v.T.astype(BF16)
    return _peer(h2, s_t, u, vt, x1, g2, tm=min(TOKENS_PEER, T), eb=PEER_EXPERT_BLOCK,
                 sb=PEER_SUB_BLOCK)


def kernel(x, c, ctx, c_ctx, w_ada, b_ada, norm_mix_g, norm_ffn_g, w_in, b_mgates, mlstm_norm_g,
           attn_q_norm_g, attn_k_norm_g, attn_sink, w_branch_m, w_branch_a, w_out, peer_w_query,
           peer_sub_keys, peer_u, peer_v):
    B, T, D = x.shape
    depth = w_ada.shape[0]
    assert depth == 1, "context-stream update for deeper stacks is not implemented"
    cos_t, sin_t = _rope_tables(T)
    cond8 = jnp.zeros((8, D), F32).at[0:B].set(c).at[B].set(c_ctx)
    for layer in range(depth):
        mod = _adaln(cond8, w_ada[layer], b_ada[layer])
        x = _layer(x, ctx, mod, norm_mix_g[layer], norm_ffn_g[layer], w_in[layer], b_mgates[layer],
                   mlstm_norm_g[layer], attn_q_norm_g[layer], attn_k_norm_g[layer], attn_sink[layer],
                   w_branch_m[layer], w_branch_a[layer], w_out[layer], peer_w_query[layer],
                   peer_sub_keys[layer], peer_u[layer], peer_v[layer], cos_t, sin_t)
    return x
```

```python
import functools
import math

import jax
import jax.numpy as jnp
from jax import lax
from jax.experimental import pallas as pl
from jax.experimental.pallas import tpu as pltpu

F32 = jnp.float32
BF16 = jnp.bfloat16

EPS = 1e-6
GRID_W = 64
ROPE_BASE = 10000.0

MLSTM_HEADS = 4
MLSTM_DH = 128
MLSTM_W = MLSTM_HEADS * MLSTM_DH
MLSTM_CHUNK = 128
GATE_LANES = 128

ATTN_Q_HEADS = 8
ATTN_KV_HEADS = 2
ATTN_DH = 64
ATTN_Q_W = ATTN_Q_HEADS * ATTN_DH
ATTN_KV_W = ATTN_KV_HEADS * ATTN_DH
ATTN_BLOCK = 128

PEER_HEADS = 8
PEER_N_KEYS = 128
PEER_HALF = 128
PEER_TOPK = 16

VMEM_LIMIT_BYTES = 56 * 1024 * 1024

TOKENS_PROJ = 1024
TOKENS_MERGE = 512
TOKENS_PQUERY = 1024
TOKENS_PEER = 512
PEER_EXPERT_BLOCK = 2048
PEER_SUB_BLOCK = 1024

NEG_INF = float("-inf")


def _cparams(sem):
    return pltpu.CompilerParams(dimension_semantics=sem, vmem_limit_bytes=VMEM_LIMIT_BYTES)


def _dot(a, b):
    return jnp.dot(a, b, preferred_element_type=F32)


def _dot_nt(a, b):
    return lax.dot_general(a, b, (((1,), (1,)), ((), ())), preferred_element_type=F32)


def _dot_tn(a, b):
    return lax.dot_general(a, b, (((0,), (0,)), ((), ())), preferred_element_type=F32)


def _sigmoid(x):
    return 1.0 / (1.0 + jnp.exp(-x))


def _segsum(x2, bd):
    hi = x2.astype(BF16)
    lo = (x2 - hi.astype(F32)).astype(BF16)
    return _dot(hi, bd) + _dot(lo, bd)


def _modulated(x, g, shift, scale):
    ms = jnp.mean(x * x, axis=-1, keepdims=True)
    return (x * lax.rsqrt(ms + EPS)) * g * (1.0 + scale) + shift


def _adaln_kernel(c_ref, w_ref, b_ref, o_ref):
    c = c_ref[...]
    s = c * _sigmoid(c)
    o_ref[...] = jnp.dot(s, w_ref[...], preferred_element_type=F32,
                         precision=lax.Precision.HIGHEST) + b_ref[...]


def _adaln(cond8, w, b):
    d = cond8.shape[1]
    n = w.shape[1]
    bn = n // 6
    return pl.pallas_call(
        _adaln_kernel,
        grid=(n // bn,),
        in_specs=[pl.BlockSpec((8, d), lambda j: (0, 0)),
                  pl.BlockSpec((d, bn), lambda j: (0, j)),
                  pl.BlockSpec((1, bn), lambda j: (0, j))],
        out_specs=pl.BlockSpec((8, bn), lambda j: (0, j)),
        out_shape=jax.ShapeDtypeStruct((8, n), F32),
        compiler_params=_cparams(("arbitrary",)),
        name="adaln",
    )(cond8, w, b.reshape(1, n))


_C_MQ, _C_MK, _C_MV = 0, 512, 1024
_C_AQ, _C_AK, _C_AV, _C_MG = 1536, 2048, 2176, 2304
_C_END = 2432


def _rope(t, cos_t, sin_t, width):
    lane = lax.broadcasted_iota(jnp.int32, t.shape, 1)
    lower = (lane % ATTN_DH) < (ATTN_DH // 2)
    swapped = jnp.where(lower, pltpu.roll(t, width - ATTN_DH // 2, 1),
                        pltpu.roll(t, ATTN_DH // 2, 1))
    return t * cos_t + swapped * sin_t


def _proj_kernel(x_ref, sh_ref, sc_ref, g_ref, w_ref, bg_ref, qg_ref, kg_ref,
                 bdq_ref, bdk_ref, cos_ref, sin_ref,
                 mq_ref, mk_ref, mv_ref, mg_ref, aq_ref, ak_ref, av_ref, *, rope):
    x = x_ref[0]
    h = _modulated(x, g_ref[...], sh_ref[0], sc_ref[0]).astype(BF16)

    p = _dot(h, w_ref[:, _C_MQ:_C_AQ])
    mq_ref[0] = p[:, 0:512].astype(BF16)
    mk_ref[0] = (p[:, 512:1024] * (MLSTM_DH ** -0.5)).astype(BF16)
    mv_ref[0] = p[:, 1024:1536].astype(BF16)

    pq = _dot(h, w_ref[:, _C_AQ:_C_AK])
    ssq = _segsum(pq * pq, bdq_ref[...])
    q = pq * lax.rsqrt(ssq * (1.0 / ATTN_DH) + EPS) * qg_ref[...]
    if rope:
        cos_t = cos_ref[...]
        sin_t = sin_ref[...]
        q = _rope(q, jnp.concatenate([cos_t] * 4, axis=1), jnp.concatenate([sin_t] * 4, axis=1),
                  ATTN_Q_W)
    aq_ref[0] = (q * (ATTN_DH ** -0.5)).astype(BF16)

    pk = _dot(h, w_ref[:, _C_AK:_C_AV])
    ssk = _segsum(pk * pk, bdk_ref[...])
    k = pk * lax.rsqrt(ssk * (1.0 / ATTN_DH) + EPS) * kg_ref[...]
    if rope:
        k = _rope(k, cos_ref[...], sin_ref[...], ATTN_KV_W)
    ak_ref[0] = k.astype(BF16)

    av_ref[0] = _dot(h, w_ref[:, _C_AV:_C_MG]).astype(BF16)

    z = _dot(h, w_ref[:, _C_MG:_C_END]) + bg_ref[...]
    lane = lax.broadcasted_iota(jnp.int32, z.shape, 1)
    is_forget = (lane % 8) >= MLSTM_HEADS
    logsig = jnp.minimum(z, 0.0) - jnp.log1p(jnp.exp(-jnp.abs(z)))
    mg_ref[0] = jnp.where(is_forget, logsig, z)


def _proj(x, shift, scale, g, w_pack, bg, qg, kg, bdq, bdk, cos_t, sin_t, *, rope, tm):
    B, T, D = x.shape
    nt = T // tm
    full = lambda shape: pl.BlockSpec(shape, lambda b, t: (0,) * len(shape))
    tok = lambda w: pl.BlockSpec((1, tm, w), lambda b, t: (b, t, 0))
    outs = [(MLSTM_W, BF16), (MLSTM_W, BF16), (MLSTM_W, BF16), (GATE_LANES, F32),
            (ATTN_Q_W, BF16), (ATTN_KV_W, BF16), (ATTN_KV_W, BF16)]
    return pl.pallas_call(
        functools.partial(_proj_kernel, rope=rope),
        grid=(B, nt),
        in_specs=[tok(D),
                  pl.BlockSpec((1, 1, D), lambda b, t: (b, 0, 0)),
                  pl.BlockSpec((1, 1, D), lambda b, t: (b, 0, 0)),
                  full((1, D)), full(w_pack.shape), full((1, GATE_LANES)),
                  full((1, ATTN_Q_W)), full((1, ATTN_KV_W)),
                  full((ATTN_Q_W, ATTN_Q_W)), full((ATTN_KV_W, ATTN_KV_W)),
                  pl.BlockSpec((tm, ATTN_KV_W), lambda b, t: (t, 0)),
                  pl.BlockSpec((tm, ATTN_KV_W), lambda b, t: (t, 0))],
        out_specs=[tok(w) for w, _ in outs],
        out_shape=[jax.ShapeDtypeStruct((B, T, w), dt) for w, dt in outs],
        compiler_params=_cparams(("parallel", "parallel")),
        name="proj_rope" if rope else "proj_ctx",
    )(x, shift, scale, g, w_pack, bg, qg, kg, bdq, bdk, cos_t, sin_t)


def _mlstm_kernel(qf_ref, qb_ref, kcf_ref, kcb_ref, klf_ref, klb_ref, vcf_ref, vcb_ref,
                  vlf_ref, vlb_ref, gcf_ref, gcb_ref, glf_ref, glb_ref,
                  hf_ref, hb_ref, c_ref, n_ref, m_ref, *, n_ctx_chunks):
    s = pl.program_id(0)
    L = MLSTM_CHUNK
    n_batch = qf_ref.shape[0]
    in_ctx = s < n_ctx_chunks

    @pl.when(s == 0)
    def _():
        c_ref[...] = jnp.zeros_like(c_ref)
        n_ref[...] = jnp.zeros_like(n_ref)
        m_ref[...] = jnp.zeros_like(m_ref)

    row = lax.broadcasted_iota(jnp.int32, (L, L), 0)
    col = lax.broadcasted_iota(jnp.int32, (L, L), 1)
    tri_f = (col <= row).astype(F32)
    pick = lambda c_ref_, l_ref_, bb: jnp.where(in_ctx, c_ref_[bb], l_ref_[bb])


    dirs = {}
    for bb in range(n_batch):
        for d in range(2):
            q_all = (qf_ref, qb_ref)[d][bb]
            k_all = pick((kcf_ref, kcb_ref)[d], (klf_ref, klb_ref)[d], bb)
            v_all = pick((vcf_ref, vcb_ref)[d], (vlf_ref, vlb_ref)[d], bb)
            g = pick((gcf_ref, gcb_ref)[d], (glf_ref, glb_ref)[d], bb)
            cum_f = jnp.dot(tri_f, g, preferred_element_type=F32, precision=lax.Precision.HIGHEST)
            if d == 0:
                b_all, seen, last = cum_f, col <= row, L - 1
            else:
                b_all, seen, last = cum_f[L - 1:L, :] - cum_f + g, col >= row, 0
            dirs[bb, d] = (q_all, k_all, v_all, g, b_all, g.T, b_all.T, seen, last)

    heads = [(bb, d, hh) for bb in range(n_batch) for d in range(2) for hh in range(MLSTM_HEADS)]
    hsl = lambda hh: slice(hh * MLSTM_DH, (hh + 1) * MLSTM_DH)
    sidx = lambda bb, d, hh: (bb * 2 + d) * MLSTM_HEADS + hh
    gate_cols = lambda d, hh: (d * 2 * MLSTM_HEADS + hh, d * 2 * MLSTM_HEADS + hh + MLSTM_HEADS)

    qk, qc, state = {}, {}, {}
    for key in heads:
        bb, d, hh = key
        q_all, k_all = dirs[bb, d][0], dirs[bb, d][1]
        r = sidx(*key)
        c_s = c_ref[r]
        state[key] = (c_s, n_ref[r], m_ref[r][:, 0:1])
        qk[key] = _dot_nt(q_all[:, hsl(hh)], k_all[:, hsl(hh)])
        qc[key] = _dot(q_all[:, hsl(hh)], c_s.astype(BF16))

    summ, wts = {}, {}
    for (bb, d), (_, _, _, g, b_all, _, _, _, last) in dirs.items():
        a_row = b_all[last:last + 1, :]
        w_log = a_row - b_all + pltpu.roll(g, MLSTM_HEADS, 1)
        m_row = jnp.max(w_log, axis=0, keepdims=True)
        wts[bb, d] = (a_row, m_row, jnp.exp(w_log - m_row))
    for key in heads:
        bb, d, hh = key
        k_all = dirs[bb, d][1]
        cf = gate_cols(d, hh)[1]
        a_row, m_row, w_t = wts[bb, d]
        kw = k_all[:, hsl(hh)].astype(F32) * w_t[:, cf:cf + 1]
        summ[key] = (a_row[:, cf:cf + 1], m_row[:, cf:cf + 1], kw)
    for key in heads:
        bb, d, hh = key
        v_all = dirs[bb, d][2]
        r = sidx(*key)
        c_s, n_s, m_s = state[key]
        a, m_loc, kw = summ[key]
        c_loc = _dot_tn(kw.astype(BF16), v_all[:, hsl(hh)])
        n_loc = jnp.sum(kw, axis=0, keepdims=True)
        m_new = jnp.maximum(a + m_s, m_loc)
        d_old = jnp.exp(a + m_s - m_new)
        d_new = jnp.exp(m_loc - m_new)
        c_ref[r] = d_old * c_s + d_new * c_loc
        n_ref[r] = d_old * n_s + d_new * n_loc
        m_ref[r] = jnp.broadcast_to(m_new, (1, MLSTM_DH))

    rel, m_rel, sc, inter, den = {}, {}, {}, {}, {}
    for key in heads:
        bb, d, hh = key
        _, _, _, _, _, g_t, b_t, seen, _ = dirs[bb, d]
        ci, cf = gate_cols(d, hh)
        rel[key] = jnp.where(seen, g_t[ci:ci + 1, :] - b_t[cf:cf + 1, :], NEG_INF)
        m_rel[key] = jnp.maximum(state[key][2], jnp.max(rel[key], axis=1, keepdims=True))
    for key in heads:
        m_s = state[key][2]
        m_rel_b = jnp.broadcast_to(m_rel[key], (L, MLSTM_DH))
        sc[key] = qk[key] * jnp.exp(rel[key] - m_rel_b)
        inter[key] = jnp.exp(m_s - m_rel_b)
    for key in heads:
        bb, d, hh = key
        qn = dirs[bb, d][0][:, hsl(hh)].astype(F32) * state[key][1]
        den[key] = jnp.sum(sc[key] + inter[key] * qn, axis=1, keepdims=True)
    for key in heads:
        bb, d, hh = key
        _, _, v_all, _, b_all, _, _, _, _ = dirs[bb, d]
        out_ref = (hf_ref, hb_ref)[d]
        cf = gate_cols(d, hh)[1]
        num = _dot(sc[key].astype(BF16), v_all[:, hsl(hh)]) + inter[key] * qc[key]
        floor = jnp.exp(-(b_all[:, cf:cf + 1] + m_rel[key]))
        out_ref[bb, :, hsl(hh)] = num * (1.0 / jnp.maximum(jnp.abs(den[key]), floor))


def _mlstm(q_lat, k_ctx, v_ctx, g_ctx, k_lat, v_lat, g_lat):
    B, T, _ = q_lat.shape
    L = MLSTM_CHUNK
    nl = T // L
    nc = k_ctx.shape[1] // L
    steps = nc + nl

    fwd_c = lambda s: (0, jnp.minimum(s, nc - 1), 0)
    bwd_c = lambda s: (0, jnp.maximum(nc - 1 - s, 0), 0)
    fwd_l = lambda s: (0, jnp.maximum(s - nc, 0), 0)
    bwd_l = lambda s: (0, jnp.minimum(steps - 1 - s, nl - 1), 0)

    blk = lambda w, im: pl.BlockSpec((B, L, w), im)
    pair = lambda w, f, g: [blk(w, f), blk(w, g)]
    nheads = B * 2 * MLSTM_HEADS
    return pl.pallas_call(
        functools.partial(_mlstm_kernel, n_ctx_chunks=nc),
        grid=(steps,),
        in_specs=(pair(MLSTM_W, fwd_l, bwd_l)
                  + pair(MLSTM_W, fwd_c, bwd_c) + pair(MLSTM_W, fwd_l, bwd_l)
                  + pair(MLSTM_W, fwd_c, bwd_c) + pair(MLSTM_W, fwd_l, bwd_l)
                  + pair(GATE_LANES, fwd_c, bwd_c) + pair(GATE_LANES, fwd_l, bwd_l)),
        out_specs=pair(MLSTM_W, fwd_l, bwd_l),
        out_shape=[jax.ShapeDtypeStruct((B, T, MLSTM_W), F32)] * 2,
        scratch_shapes=[pltpu.VMEM((nheads, MLSTM_DH, MLSTM_DH), F32),
                        pltpu.VMEM((nheads, 1, MLSTM_DH), F32),
                        pltpu.VMEM((nheads, 1, MLSTM_DH), F32)],
        compiler_params=_cparams(("arbitrary",)),
        name="mlstm",
    )(q_lat, q_lat, k_ctx, k_ctx, k_lat, k_lat, v_ctx, v_ctx, v_lat, v_lat,
      g_ctx, g_ctx, g_lat, g_lat)


def _attn_kernel(q_ref, kp_ref, kc_ref, kn_ref, vp_ref, vc_ref, vn_ref, kx_ref, vx_ref,
                 sink_ref, o_ref, *, nb):
    n = pl.program_id(0)
    blk = ATTN_BLOCK
    n_batch = q_ref.shape[0]

    qi = lax.broadcasted_iota(jnp.int32, (blk, blk), 0)
    kj = lax.broadcasted_iota(jnp.int32, (blk, blk), 1)
    ok_prev = (kj >= qi) & (n > 0)
    ok_next = (kj <= qi) & (n < nb - 1)
    lane = lax.broadcasted_iota(jnp.int32, (blk, 2 * ATTN_DH), 1)
    low = lane < ATTN_DH
    n_pairs = ATTN_Q_W // (2 * ATTN_DH)
    n_heads = 2 * n_pairs

    def fold(op, cols):
        groups = [c[:, i:i + blk] for c in cols for i in range(0, c.shape[1], blk)]
        return functools.reduce(op, groups)

    s_all, vcat = [], []
    for bb in range(n_batch):
        kcat = jnp.concatenate([kp_ref[bb], kc_ref[bb], kn_ref[bb], kx_ref[bb]], axis=0)
        vrows = jnp.concatenate([vp_ref[bb], vc_ref[bb], vn_ref[bb], vx_ref[bb]], axis=0)
        vcat.append(jnp.concatenate([vrows, jnp.ones_like(vrows)], axis=1))
        pieces = []
        for c in range(n_pairs):
            qc = q_ref[bb, :, c * 128:(c + 1) * 128]
            pieces += [jnp.where(low, qc, jnp.zeros_like(qc)), jnp.where(low, jnp.zeros_like(qc), qc)]
        s_all.append(_dot_nt(jnp.concatenate(pieces, axis=0), kcat))

    parts, maxes, probs, dens = {}, {}, {}, {}
    keys = [(bb, j) for bb in range(n_batch) for j in range(n_heads)]
    for bb, j in keys:
        sc = s_all[bb][j * blk:(j + 1) * blk]
        parts[bb, j] = [jnp.where(ok_prev, sc[:, 0:blk], NEG_INF), sc[:, blk:2 * blk],
                        jnp.where(ok_next, sc[:, 2 * blk:3 * blk], NEG_INF), sc[:, 3 * blk:]]
        row_max = jnp.max(fold(jnp.maximum, parts[bb, j]), axis=1, keepdims=True)
        maxes[bb, j] = jnp.maximum(row_max, sink_ref[j:j + 1, 0:1])
    for bb, j in keys:
        exps = [jnp.exp(part - maxes[bb, j]) for part in parts[bb, j]]
        dens[bb, j] = jnp.exp(sink_ref[j:j + 1, 0:1] - maxes[bb, j])
        probs[bb, j] = jnp.concatenate([p.astype(BF16) for p in exps], axis=1)
    for bb in range(n_batch):
        o_all = _dot(jnp.concatenate([probs[bb, j] for j in range(n_heads)], axis=0), vcat[bb])
        heads_out = []
        for j in range(n_heads):
            rows = slice(j * blk, (j + 1) * blk)
            heads_out.append(o_all[rows, 0:128] / (o_all[rows, 128:256] + dens[bb, j]))
        for c in range(n_pairs):
            pair = jnp.where(low, heads_out[2 * c], heads_out[2 * c + 1])
            o_ref[bb, :, c * 128:(c + 1) * 128] = pair.astype(BF16)


def _attn(aq, ak, av, akx, avx, sink8):
    B, T, _ = aq.shape
    blk = ATTN_BLOCK
    nb = T // blk
    ctx = akx.shape[1]
    kv = lambda im: pl.BlockSpec((B, blk, ATTN_KV_W), im)
    prev = lambda n: (0, jnp.maximum(n - 1, 0), 0)
    cur = lambda n: (0, n, 0)
    nxt = lambda n: (0, jnp.minimum(n + 1, nb - 1), 0)
    cx = pl.BlockSpec((B, ctx, ATTN_KV_W), lambda n: (0, 0, 0))
    return pl.pallas_call(
        functools.partial(_attn_kernel, nb=nb),
        grid=(nb,),
        in_specs=[pl.BlockSpec((B, blk, ATTN_Q_W), cur),
                  kv(prev), kv(cur), kv(nxt), kv(prev), kv(cur), kv(nxt), cx, cx,
                  pl.BlockSpec((8, 128), lambda n: (0, 0))],
        out_specs=pl.BlockSpec((B, blk, ATTN_Q_W), cur),
        out_shape=jax.ShapeDtypeStruct((B, T, ATTN_Q_W), BF16),
        compiler_params=_cparams(("parallel",)),
        name="attn",
    )(aq, ak, ak, ak, av, av, av, akx, avx, sink8)


def _merge_kernel(x_ref, sh_ref, sc_ref, g_ref, g1_ref, hf_ref, hb_ref, a_ref,
                  wg_ref, ng_ref, wbm_ref, wba_ref, wo_ref, o_ref):
    x = x_ref[0]
    h = _modulated(x, g_ref[...], sh_ref[0], sc_ref[0]).astype(BF16)
    pg = _dot(h, wg_ref[...])
    d = x.shape[1]
    hs = hf_ref[0] + hb_ref[0]
    heads = [hs[:, i:i + MLSTM_DH] for i in range(0, MLSTM_W, MLSTM_DH)]
    hn = jnp.concatenate(
        [hh * lax.rsqrt(jnp.mean(hh * hh, axis=1, keepdims=True) + EPS) for hh in heads], axis=1)
    m = hn * ng_ref[...] * _sigmoid(pg[:, 0:MLSTM_W])
    mm = _dot(m.astype(BF16), wbm_ref[...])
    aa = _dot(a_ref[0], wba_ref[...])
    y = _sigmoid(pg[:, MLSTM_W:MLSTM_W + d]) * mm + _sigmoid(pg[:, MLSTM_W + d:]) * aa
    o_ref[0] = x + g1_ref[0] * _dot(y.astype(BF16), wo_ref[...])


def _merge(x, shift, scale, g, g1, hf, hb, a, wg, ng, wbm, wba, wo, *, tm):
    B, T, D = x.shape
    full = lambda arr: pl.BlockSpec(arr.shape, lambda b, t: (0,) * arr.ndim)
    tok = lambda w: pl.BlockSpec((1, tm, w), lambda b, t: (b, t, 0))
    perb = pl.BlockSpec((1, 1, D), lambda b, t: (b, 0, 0))
    return pl.pallas_call(
        _merge_kernel,
        grid=(B, T // tm),
        in_specs=[tok(D), perb, perb, full(g), perb, tok(MLSTM_W), tok(MLSTM_W), tok(ATTN_Q_W),
                  full(wg), full(ng), full(wbm), full(wba), full(wo)],
        out_specs=tok(D),
        out_shape=jax.ShapeDtypeStruct((B, T, D), F32),
        compiler_params=_cparams(("parallel", "parallel")),
        name="merge",
    )(x, shift, scale, g, g1, hf, hb, a, wg, ng, wbm, wba, wo)


def _pquery_kernel(x_ref, sh_ref, sc_ref, g_ref, wq_ref, sk_ref, h_ref, s_ref):
    x = x_ref[0]
    h = _modulated(x, g_ref[...], sh_ref[0], sc_ref[0]).astype(BF16)
    h_ref[0] = h
    q = _dot(h, wq_ref[...]).astype(BF16)
    for j in range(2 * PEER_HEADS):
        s_ref[0, j] = _dot_nt(sk_ref[j], q[:, j * PEER_HALF:(j + 1) * PEER_HALF])


def _pquery(x, shift, scale, g, wq, sk, *, tm):
    B, T, D = x.shape
    nj = 2 * PEER_HEADS
    full = lambda arr: pl.BlockSpec(arr.shape, lambda b, t: (0,) * arr.ndim)
    perb = pl.BlockSpec((1, 1, D), lambda b, t: (b, 0, 0))
    return pl.pallas_call(
        _pquery_kernel,
        grid=(B, T // tm),
        in_specs=[pl.BlockSpec((1, tm, D), lambda b, t: (b, t, 0)), perb, perb,
                  full(g), full(wq), full(sk)],
        out_specs=[pl.BlockSpec((1, tm, D), lambda b, t: (b, t, 0)),
                   pl.BlockSpec((1, nj, PEER_N_KEYS, tm), lambda b, t: (b, 0, 0, t))],
        out_shape=[jax.ShapeDtypeStruct((B, T, D), BF16),
                   jax.ShapeDtypeStruct((B, nj, PEER_N_KEYS, T), F32)],
        compiler_params=_cparams(("parallel", "parallel")),
        name="pquery",
    )(x, shift, scale, g, wq, sk)


def _top_values(val, k):
    tops = []
    for _ in range(k):
        m = jnp.max(val, axis=0, keepdims=True)
        tops.append(m)
        val = jnp.where(val == m, NEG_INF, val)
    return tops


_GELU_K0 = -2.0 * math.sqrt(2.0 / math.pi) * math.log2(math.e)
_GELU_K1 = 0.044715 * _GELU_K0


def _gelu_tanh(x, k0, k1):
    one = jnp.asarray(1.0, x.dtype)
    return x / (one + jnp.exp2(x * (x * x * k1 + k0)))


BF16_ROWS = 16


def _rows_bf16(row, tm):
    tile = jnp.broadcast_to(row, (BF16_ROWS, tm)).astype(BF16)
    return jnp.concatenate([tile] * (PEER_N_KEYS // BF16_ROWS), axis=0)


def _row_tile_bf16(ref, h, i, tm):
    return _rows_bf16(ref[h, pl.ds(i, 1), :], tm)


def _batcher_pairs(n):
    pairs = []
    p = 1
    while p < n:
        k = p
        while k >= 1:
            for j in range(k % p, n - k, 2 * k):
                for i in range(min(k, n - j - k)):
                    if (i + j) // (2 * p) == (i + j + k) // (2 * p):
                        pairs.append((i + j, i + j + k))
            k //= 2
        p *= 2
    return pairs


def _compare_exchange(rows, i, j):
    rows[i], rows[j] = jnp.maximum(rows[i], rows[j]), jnp.minimum(rows[i], rows[j])


def _sorted_top(x, k):
    sub = 8
    assert x.shape[0] == sub * k and k & (k - 1) == 0
    rows = [x[sub * r:sub * (r + 1), :] for r in range(k)]
    for i, j in _batcher_pairs(k):
        _compare_exchange(rows, i, j)
    shift = sub // 2
    while shift >= 1:
        other = [pltpu.roll(r, shift, 0) for r in rows]
        rows = [jnp.maximum(rows[r], other[k - 1 - r]) for r in range(k)]
        d = k // 2
        while d >= 1:
            for i in range(k):
                if i & d == 0:
                    _compare_exchange(rows, i, i + d)
            d //= 2
        shift //= 2
    return rows


def _prefix_count(pred, vals):
    k = len(vals)
    bits, steps = [], []
    s = k // 2
    while s >= 1:
        def pivot(i, base):
            if i == len(bits):
                return vals[base + s - 1]
            return jnp.where(bits[i], pivot(i + 1, base + steps[i]), pivot(i + 1, base))
        bits.append(pred(pivot(0, 0)))
        steps.append(s)
        s //= 2
    n = functools.reduce(jnp.add, [jnp.where(b, float(st), 0.0) for b, st in zip(bits, steps)])
    return jnp.where(pred(vals[k - 1]), float(k), n)


def _peer_router(s_ref, rank_ref, cw_ref, cnt_ref, e2_ref):
    k = PEER_TOPK
    for h in range(PEER_HEADS):
        s1 = s_ref[0, 2 * h]
        s2 = s_ref[0, 2 * h + 1]
        a = [r[0:1, :] for r in _sorted_top(s1, k)]
        b = [r[0:1, :] for r in _sorted_top(s2, k)]
        cand = [a[p] + b[q] for p in range(k) for q in range(k) if (p + 1) * (q + 1) <= k]
        cv = jnp.concatenate(cand, axis=0)
        thr = _top_values(cv, k)[k - 1]
        top = a[0] + b[0]
        z = jnp.sum(jnp.where(cv >= thr, jnp.exp(cv - top), 0.0), axis=0, keepdims=True)
        rank = _prefix_count(lambda v: v > s1, a)
        cnt = _prefix_count(lambda v: v + s2 >= thr, a)
        rank_ref[h] = rank
        cw_ref[h] = jnp.exp(s1 - a[0]) / z
        cnt_ref[h] = cnt.astype(BF16)
        e2_ref[h] = jnp.exp(s2 - b[0]).astype(BF16)


def _peer_kernel(h_ref, s_ref, u_ref, vt_ref, x_ref, g2_ref, gk_ref, o_ref,
                 acc_ref, rank_ref, cw_ref, cnt_ref, e2_ref, wt_ref, *, eb, sb):
    e = pl.program_id(2)
    ne = pl.num_programs(2)
    tm = h_ref.shape[1]

    @pl.when(e == 0)
    def _():
        acc_ref[...] = jnp.zeros_like(acc_ref)
        _peer_router(s_ref, rank_ref, cw_ref, cnt_ref, e2_ref)

    hq = h_ref[0]
    n_i = sb // PEER_N_KEYS
    n_sub = eb // sb
    scores = lambda j: _dot_nt(u_ref[j * sb:(j + 1) * sb, :], hq)
    zero = jnp.zeros((), BF16)
    gk0 = _rows_bf16(gk_ref[0:1, :], tm)
    gk1 = _rows_bf16(gk_ref[1:2, :], tm)
    def output_product(j):
        acc_ref[...] += _dot(vt_ref[:, j * sb:(j + 1) * sb], wt_ref[j % 2])

    acts = {j: scores(j) for j in range(min(2, n_sub))}
    for j in range(n_sub):
        act = acts.pop(j)
        for ii in range(n_i):
            if ii == n_i // 2 and j > 0:
                output_product(j - 1)
            i = e * (eb // PEER_N_KEYS) + j * n_i + ii
            w = None
            for h in range(PEER_HEADS):
                rank_i = _row_tile_bf16(rank_ref, h, i, tm)
                cw_i = _row_tile_bf16(cw_ref, h, i, tm)
                term = jnp.where(rank_i < cnt_ref[h], e2_ref[h], zero) * cw_i
                w = term if w is None else w + term
            act_i = act[ii * PEER_N_KEYS:(ii + 1) * PEER_N_KEYS].astype(BF16)
            wt_ref[j % 2, ii * PEER_N_KEYS:(ii + 1) * PEER_N_KEYS, :] = w * _gelu_tanh(act_i, gk0, gk1)
        if j + 2 < n_sub:
            acts[j + 2] = scores(j + 2)
    output_product(n_sub - 1)

    @pl.when(e == ne - 1)
    def _():
        o_ref[0] = x_ref[0] + g2_ref[0] * acc_ref[...].T


def _peer(h2, s_t, u, vt, x1, g2, *, tm, eb, sb):
    B, T, D = x1.shape
    ne = u.shape[0] // eb
    nj = 2 * PEER_HEADS
    gelu_k = jnp.broadcast_to(jnp.array([[_GELU_K0], [_GELU_K1]], F32), (2, tm))
    return pl.pallas_call(
        functools.partial(_peer_kernel, eb=eb, sb=sb),
        grid=(B, T // tm, ne),
        in_specs=[pl.BlockSpec((1, tm, D), lambda b, t, e: (b, t, 0)),
                  pl.BlockSpec((1, nj, PEER_N_KEYS, tm), lambda b, t, e: (b, 0, 0, t)),
                  pl.BlockSpec((eb, D), lambda b, t, e: (e, 0)),
                  pl.BlockSpec((D, eb), lambda b, t, e: (0, e)),
                  pl.BlockSpec((1, tm, D), lambda b, t, e: (b, t, 0)),
                  pl.BlockSpec((1, 1, D), lambda b, t, e: (b, 0, 0)),
                  pl.BlockSpec((2, tm), lambda b, t, e: (0, 0))],
        out_specs=pl.BlockSpec((1, tm, D), lambda b, t, e: (b, t, 0)),
        out_shape=jax.ShapeDtypeStruct((B, T, D), F32),
        scratch_shapes=[pltpu.VMEM((D, tm), F32),
                        pltpu.VMEM((PEER_HEADS, PEER_N_KEYS, tm), F32),
                        pltpu.VMEM((PEER_HEADS, PEER_N_KEYS, tm), F32),
                        pltpu.VMEM((PEER_HEADS, PEER_N_KEYS, tm), BF16),
                        pltpu.VMEM((PEER_HEADS, PEER_N_KEYS, tm), BF16),
                        pltpu.VMEM((2, sb, tm), BF16)],
        compiler_params=_cparams(("parallel", "parallel", "arbitrary")),
        name="peer",
    )(h2, s_t, u, vt, x1, g2, gelu_k)


_Q_HEAD_ORDER = (0, 4, 1, 5, 2, 6, 3, 7)


def _block_diag_ones(width, seg):
    i = jnp.arange(width)
    return (i[:, None] // seg == i[None, :] // seg).astype(BF16)


def _rope_tables(T):
    rows = T // GRID_W
    nf = ATTN_DH // 4
    row = jnp.broadcast_to(jnp.arange(rows)[:, None], (rows, GRID_W)).reshape(T)
    col = jnp.broadcast_to(jnp.arange(GRID_W)[None, :], (rows, GRID_W)).reshape(T)
    inv = ROPE_BASE ** (-jnp.arange(nf, dtype=F32) / nf)
    ang = jnp.concatenate([row[:, None].astype(F32) * inv, col[:, None].astype(F32) * inv], axis=-1)
    cos, sin = jnp.cos(ang), jnp.sin(ang)
    cos_t = jnp.concatenate([cos, cos] * ATTN_KV_HEADS, axis=-1)
    sin_t = jnp.concatenate([-sin, sin] * ATTN_KV_HEADS, axis=-1)
    return cos_t, sin_t


def _layer(x, ctx, mod, norm_mix_g, norm_ffn_g, w_in, b_mgates, mlstm_norm_g, attn_q_norm_g,
           attn_k_norm_g, attn_sink, w_branch_m, w_branch_a, w_out, peer_w_query, peer_sub_keys,
           peer_u, peer_v, cos_t, sin_t):
    B, T, D = x.shape
    n_ctx = ctx.shape[1]
    mods = jnp.split(mod, 6, axis=-1)
    lat = [m[0:B].reshape(B, 1, D) for m in mods]
    cxm = [jnp.broadcast_to(m[B:B + 1].reshape(1, 1, D), (B, 1, D)) for m in mods]
    sh1, sc1, g1, sh2, sc2, g2 = lat

    W = MLSTM_W
    o0 = 4 * W
    o1 = o0 + 2 * 2 * MLSTM_HEADS
    w_mq, w_mk, w_mv, w_mo = (w_in[:, i * W:(i + 1) * W] for i in range(4))
    w_g16 = w_in[:, o0:o1]
    w_aq = w_in[:, o1:o1 + ATTN_Q_W].reshape(D, ATTN_Q_HEADS, ATTN_DH)
    w_aq = w_aq[:, jnp.array(_Q_HEAD_ORDER), :].reshape(D, ATTN_Q_W)
    o2 = o1 + ATTN_Q_W
    w_ak = w_in[:, o2:o2 + ATTN_KV_W]
    w_av = w_in[:, o2 + ATTN_KV_W:o2 + 2 * ATTN_KV_W]
    w_gates = w_in[:, o2 + 2 * ATTN_KV_W:]
    w_g128 = jnp.pad(w_g16, ((0, 0), (0, GATE_LANES - w_g16.shape[1])))
    w_pack = jnp.concatenate([w_mq, w_mk, w_mv, w_aq, w_ak, w_av, w_g128], axis=1).astype(BF16)
    w_merge_in = jnp.concatenate([w_mo, w_gates], axis=1).astype(BF16)
    bg = jnp.pad(b_mgates, (0, GATE_LANES - b_mgates.shape[0])).reshape(1, GATE_LANES)
    qg = jnp.tile(attn_q_norm_g, ATTN_Q_HEADS).reshape(1, ATTN_Q_W)
    kg = jnp.tile(attn_k_norm_g, ATTN_KV_HEADS).reshape(1, ATTN_KV_W)
    bdq = _block_diag_ones(ATTN_Q_W, ATTN_DH)
    bdk = _block_diag_ones(ATTN_KV_W, ATTN_DH)
    gmix = norm_mix_g.reshape(1, D)

    mq, mk, mv, mg, aq, ak, av = _proj(x, sh1, sc1, gmix, w_pack, bg, qg, kg, bdq, bdk,
                                       cos_t, sin_t, rope=True, tm=min(TOKENS_PROJ, T))
    _, mkc, mvc, mgc, _, akc, avc = _proj(ctx, cxm[0], cxm[1], gmix, w_pack, bg, qg, kg, bdq, bdk,
                                          cos_t, sin_t, rope=False, tm=n_ctx)

    hf, hb = _mlstm(mq, mkc, mvc, mgc, mk, mv, mg)

    sink8 = jnp.broadcast_to(attn_sink[jnp.array(_Q_HEAD_ORDER)].astype(F32)[:, None], (8, 128))
    a = _attn(aq, ak, av, akc, avc, sink8)

    wba = w_branch_a.reshape(ATTN_Q_HEADS, ATTN_DH, D)[jnp.array(_Q_HEAD_ORDER)]
    wba = wba.reshape(ATTN_Q_W, D).astype(BF16)
    x1 = _merge(x, sh1, sc1, gmix, g1, hf, hb, a, w_merge_in, mlstm_norm_g.reshape(1, MLSTM_W),
                w_branch_m.astype(BF16), wba, w_out.astype(BF16), tm=min(TOKENS_MERGE, T))

    sk = peer_sub_keys.reshape(2 * PEER_HEADS, PEER_N_KEYS, PEER_HALF).astype(BF16)
    h2, s_t = _pquery(x1, sh2, sc2, norm_ffn_g.reshape(1, D), peer_w_query.astype(BF16), sk,
                      tm=min(TOKENS_PQUERY, T))
    u = peer_u.astype(BF16)
    vt = peer_v.T.astype(BF16)
    return _peer(h2, s_t, u, vt, x1, g2, tm=min(TOKENS_PEER, T), eb=PEER_EXPERT_BLOCK,
                 sb=PEER_SUB_BLOCK)


def kernel(x, c, ctx, c_ctx, w_ada, b_ada, norm_mix_g, norm_ffn_g, w_in, b_mgates, mlstm_norm_g,
           attn_q_norm_g, attn_k_norm_g, attn_sink, w_branch_m, w_branch_a, w_out, peer_w_query,
           peer_sub_keys, peer_u, peer_v):
    B, T, D = x.shape
    depth = w_ada.shape[0]
    assert depth == 1, "context-stream update for deeper stacks is not implemented"
    cos_t, sin_t = _rope_tables(T)
    cond8 = jnp.zeros((8, D), F32).at[0:B].set(c).at[B].set(c_ctx)
    for layer in range(depth):
        mod = _adaln(cond8, w_ada[layer], b_ada[layer])
        x = _layer(x, ctx, mod, norm_mix_g[layer], norm_ffn_g[layer], w_in[layer], b_mgates[layer],
                   mlstm_norm_g[layer], attn_q_norm_g[layer], attn_k_norm_g[layer], attn_sink[layer],
                   w_branch_m[layer], w_branch_a[layer], w_out[layer], peer_w_query[layer],
                   peer_sub_keys[layer], peer_u[layer], peer_v[layer], cos_t, sin_t)
    return x
```

```python
import functools
import math

import jax
import jax.numpy as jnp
from jax import lax
from jax.experimental import pallas as pl
from jax.experimental.pallas import tpu as pltpu

F32 = jnp.float32
BF16 = jnp.bfloat16

EPS = 1e-6
GRID_W = 64
ROPE_BASE = 10000.0

MLSTM_HEADS = 4
MLSTM_DH = 128
MLSTM_W = MLSTM_HEADS * MLSTM_DH
MLSTM_CHUNK = 128
GATE_LANES = 128

ATTN_Q_HEADS = 8
ATTN_KV_HEADS = 2
ATTN_DH = 64
ATTN_Q_W = ATTN_Q_HEADS * ATTN_DH
ATTN_KV_W = ATTN_KV_HEADS * ATTN_DH
ATTN_BLOCK = 128

PEER_HEADS = 8
PEER_N_KEYS = 128
PEER_HALF = 128
PEER_TOPK = 16

VMEM_LIMIT_BYTES = 56 * 1024 * 1024

TOKENS_PROJ = 1024
TOKENS_MERGE = 512
TOKENS_PQUERY = 1024
TOKENS_PEER = 512
PEER_EXPERT_BLOCK = 2048
PEER_SUB_BLOCK = 1024

NEG_INF = float("-inf")


def _cparams(sem):
    return pltpu.CompilerParams(dimension_semantics=sem, vmem_limit_bytes=VMEM_LIMIT_BYTES)


def _dot(a, b):
    return jnp.dot(a, b, preferred_element_type=F32)


def _dot_nt(a, b):
    return lax.dot_general(a, b, (((1,), (1,)), ((), ())), preferred_element_type=F32)


def _dot_tn(a, b):
    return lax.dot_general(a, b, (((0,), (0,)), ((), ())), preferred_element_type=F32)


def _sigmoid(x):
    return 1.0 / (1.0 + jnp.exp(-x))


def _segsum(x2, bd):
    hi = x2.astype(BF16)
    lo = (x2 - hi.astype(F32)).astype(BF16)
    return _dot(hi, bd) + _dot(lo, bd)


def _modulated(x, g, shift, scale):
    ms = jnp.mean(x * x, axis=-1, keepdims=True)
    return (x * lax.rsqrt(ms + EPS)) * g * (1.0 + scale) + shift


def _adaln_kernel(c_ref, w_ref, b_ref, o_ref):
    c = c_ref[...]
    s = c * _sigmoid(c)
    o_ref[...] = jnp.dot(s, w_ref[...], preferred_element_type=F32,
                         precision=lax.Precision.HIGHEST) + b_ref[...]


def _adaln(cond8, w, b):
    d = cond8.shape[1]
    n = w.shape[1]
    bn = n // 6
    return pl.pallas_call(
        _adaln_kernel,
        grid=(n // bn,),
        in_specs=[pl.BlockSpec((8, d), lambda j: (0, 0)),
                  pl.BlockSpec((d, bn), lambda j: (0, j)),
                  pl.BlockSpec((1, bn), lambda j: (0, j))],
        out_specs=pl.BlockSpec((8, bn), lambda j: (0, j)),
        out_shape=jax.ShapeDtypeStruct((8, n), F32),
        compiler_params=_cparams(("arbitrary",)),
        name="adaln",
    )(cond8, w, b.reshape(1, n))


_C_MQ, _C_MK, _C_MV = 0, 512, 1024
_C_AQ, _C_AK, _C_AV, _C_MG = 1536, 2048, 2176, 2304
_C_END = 2432


def _rope(t, cos_t, sin_t, width):
    lane = lax.broadcasted_iota(jnp.int32, t.shape, 1)
    lower = (lane % ATTN_DH) < (ATTN_DH // 2)
    swapped = jnp.where(lower, pltpu.roll(t, width - ATTN_DH // 2, 1),
                        pltpu.roll(t, ATTN_DH // 2, 1))
    return t * cos_t + swapped * sin_t


def _proj_kernel(x_ref, sh_ref, sc_ref, g_ref, w_ref, bg_ref, qg_ref, kg_ref,
                 bdq_ref, bdk_ref, cos_ref, sin_ref,
                 mq_ref, mk_ref, mv_ref, mg_ref, aq_ref, ak_ref, av_ref, *, rope):
    x = x_ref[0]
    h = _modulated(x, g_ref[...], sh_ref[0], sc_ref[0]).astype(BF16)

    p = _dot(h, w_ref[:, _C_MQ:_C_AQ])
    mq_ref[0] = p[:, 0:512].astype(BF16)
    mk_ref[0] = (p[:, 512:1024] * (MLSTM_DH ** -0.5)).astype(BF16)
    mv_ref[0] = p[:, 1024:1536].astype(BF16)

    pq = _dot(h, w_ref[:, _C_AQ:_C_AK])
    ssq = _segsum(pq * pq, bdq_ref[...])
    q = pq * lax.rsqrt(ssq * (1.0 / ATTN_DH) + EPS) * qg_ref[...]
    if rope:
        cos_t = cos_ref[...]
        sin_t = sin_ref[...]
        q = _rope(q, jnp.concatenate([cos_t] * 4, axis=1), jnp.concatenate([sin_t] * 4, axis=1),
                  ATTN_Q_W)
    aq_ref[0] = (q * (ATTN_DH ** -0.5)).astype(BF16)

    pk = _dot(h, w_ref[:, _C_AK:_C_AV])
    ssk = _segsum(pk * pk, bdk_ref[...])
    k = pk * lax.rsqrt(ssk * (1.0 / ATTN_DH) + EPS) * kg_ref[...]
    if rope:
        k = _rope(k, cos_ref[...], sin_ref[...], ATTN_KV_W)
    ak_ref[0] = k.astype(BF16)

    av_ref[0] = _dot(h, w_ref[:, _C_AV:_C_MG]).astype(BF16)

    z = _dot(h, w_ref[:, _C_MG:_C_END]) + bg_ref[...]
    lane = lax.broadcasted_iota(jnp.int32, z.shape, 1)
    is_forget = (lane % 8) >= MLSTM_HEADS
    logsig = jnp.minimum(z, 0.0) - jnp.log1p(jnp.exp(-jnp.abs(z)))
    mg_ref[0] = jnp.where(is_forget, logsig, z)


def _proj(x, shift, scale, g, w_pack, bg, qg, kg, bdq, bdk, cos_t, sin_t, *, rope, tm):
    B, T, D = x.shape
    nt = T // tm
    full = lambda shape: pl.BlockSpec(shape, lambda b, t: (0,) * len(shape))
    tok = lambda w: pl.BlockSpec((1, tm, w), lambda b, t: (b, t, 0))
    outs = [(MLSTM_W, BF16), (MLSTM_W, BF16), (MLSTM_W, BF16), (GATE_LANES, F32),
            (ATTN_Q_W, BF16), (ATTN_KV_W, BF16), (ATTN_KV_W, BF16)]
    return pl.pallas_call(
        functools.partial(_proj_kernel, rope=rope),
        grid=(B, nt),
        in_specs=[tok(D),
                  pl.BlockSpec((1, 1, D), lambda b, t: (b, 0, 0)),
                  pl.BlockSpec((1, 1, D), lambda b, t: (b, 0, 0)),
                  full((1, D)), full(w_pack.shape), full((1, GATE_LANES)),
                  full((1, ATTN_Q_W)), full((1, ATTN_KV_W)),
                  full((ATTN_Q_W, ATTN_Q_W)), full((ATTN_KV_W, ATTN_KV_W)),
                  pl.BlockSpec((tm, ATTN_KV_W), lambda b, t: (t, 0)),
                  pl.BlockSpec((tm, ATTN_KV_W), lambda b, t: (t, 0))],
        out_specs=[tok(w) for w, _ in outs],
        out_shape=[jax.ShapeDtypeStruct((B, T, w), dt) for w, dt in outs],
        compiler_params=_cparams(("parallel", "parallel")),
        name="proj_rope" if rope else "proj_ctx",
    )(x, shift, scale, g, w_pack, bg, qg, kg, bdq, bdk, cos_t, sin_t)


def _mlstm_kernel(qf_ref, qb_ref, kcf_ref, kcb_ref, klf_ref, klb_ref, vcf_ref, vcb_ref,
                  vlf_ref, vlb_ref, gcf_ref, gcb_ref, glf_ref, glb_ref,
                  hf_ref, hb_ref, c_ref, m_ref, *, n_ctx_chunks):
    s = pl.program_id(0)
    L = MLSTM_CHUNK
    n_batch = qf_ref.shape[0]
    in_ctx = s < n_ctx_chunks

    @pl.when(s == 0)
    def _():
        c_ref[...] = jnp.zeros_like(c_ref)
        m_ref[...] = jnp.zeros_like(m_ref)

    row = lax.broadcasted_iota(jnp.int32, (L, L), 0)
    col = lax.broadcasted_iota(jnp.int32, (L, L), 1)
    tri_f = (col <= row).astype(F32)
    pick = lambda c_ref_, l_ref_, bb: jnp.where(in_ctx, c_ref_[bb], l_ref_[bb])


    dirs = {}
    for bb in range(n_batch):
        for d in range(2):
            q_all = (qf_ref, qb_ref)[d][bb]
            k_all = pick((kcf_ref, kcb_ref)[d], (klf_ref, klb_ref)[d], bb)
            v_all = pick((vcf_ref, vcb_ref)[d], (vlf_ref, vlb_ref)[d], bb)
            g = pick((gcf_ref, gcb_ref)[d], (glf_ref, glb_ref)[d], bb)
            cum_f = jnp.dot(tri_f, g, preferred_element_type=F32, precision=lax.Precision.HIGHEST)
            if d == 0:
                b_all, seen, last = cum_f, col <= row, L - 1
            else:
                b_all, seen, last = cum_f[L - 1:L, :] - cum_f + g, col >= row, 0
            dirs[bb, d] = (q_all, k_all, v_all, g, b_all, g.T, b_all.T, seen, last)

    heads = [(bb, d, hh) for bb in range(n_batch) for d in range(2) for hh in range(MLSTM_HEADS)]
    hsl = lambda hh: slice(hh * MLSTM_DH, (hh + 1) * MLSTM_DH)
    sidx = lambda bb, d, hh: (bb * 2 + d) * MLSTM_HEADS + hh
    gate_cols = lambda d, hh: (d * 2 * MLSTM_HEADS + hh, d * 2 * MLSTM_HEADS + hh + MLSTM_HEADS)

    qk, qc, state = {}, {}, {}
    for key in heads:
        bb, d, hh = key
        q_all, k_all = dirs[bb, d][0], dirs[bb, d][1]
        r = sidx(*key)
        c_s = c_ref[r]
        state[key] = (c_s, m_ref[r][:, 0:1])
        qk[key] = _dot_nt(q_all[:, hsl(hh)], k_all[:, hsl(hh)])
        qc[key] = _dot(q_all[:, hsl(hh)], c_s.astype(BF16))

    summ, wts, v_ext = {}, {}, {}
    for (bb, d), (_, _, _, g, b_all, _, _, _, last) in dirs.items():
        a_row = b_all[last:last + 1, :]
        w_log = a_row - b_all + pltpu.roll(g, MLSTM_HEADS, 1)
        m_row = jnp.max(w_log, axis=0, keepdims=True)
        wts[bb, d] = (a_row, m_row, jnp.exp(w_log - m_row))
    for key in heads:
        bb, d, hh = key
        k_all = dirs[bb, d][1]
        cf = gate_cols(d, hh)[1]
        a_row, m_row, w_t = wts[bb, d]
        kw = k_all[:, hsl(hh)].astype(F32) * w_t[:, cf:cf + 1]
        summ[key] = (a_row[:, cf:cf + 1], m_row[:, cf:cf + 1], kw)
    for key in heads:
        bb, d, hh = key
        v_all = dirs[bb, d][2]
        r = sidx(*key)
        c_s, m_s = state[key]
        a, m_loc, kw = summ[key]
        v_ext[key] = jnp.concatenate([v_all[:, hsl(hh)], jnp.ones((L, MLSTM_DH), BF16)], axis=1)
        c_loc = _dot_tn(kw.astype(BF16), v_ext[key])
        m_new = jnp.maximum(a + m_s, m_loc)
        d_old = jnp.exp(a + m_s - m_new)
        d_new = jnp.exp(m_loc - m_new)
        c_ref[r] = d_old * c_s + d_new * c_loc
        m_ref[r] = jnp.broadcast_to(m_new, (1, MLSTM_DH))

    rel, m_rel, sc, inter, floor = {}, {}, {}, {}, {}
    for key in heads:
        bb, d, hh = key
        _, _, _, _, _, g_t, b_t, seen, _ = dirs[bb, d]
        ci, cf = gate_cols(d, hh)
        rel[key] = jnp.where(seen, g_t[ci:ci + 1, :] - b_t[cf:cf + 1, :], NEG_INF)
        m_rel[key] = jnp.maximum(state[key][1], jnp.max(rel[key], axis=1, keepdims=True))
    for key in heads:
        bb, d, hh = key
        m_s = state[key][1]
        cf = gate_cols(d, hh)[1]
        m_rel_b = jnp.broadcast_to(m_rel[key], (L, MLSTM_DH))
        b_col_b = jnp.broadcast_to(dirs[bb, d][4][:, cf:cf + 1], (L, MLSTM_DH))
        sc[key] = qk[key] * jnp.exp(rel[key] - m_rel_b)
        inter[key] = jnp.exp(m_s - m_rel_b)
        floor[key] = jnp.exp(-(b_col_b + m_rel_b))
    for key in heads:
        bb, d, hh = key
        out_ref = (hf_ref, hb_ref)[d]
        both = _dot(sc[key].astype(BF16), v_ext[key]) + jnp.concatenate([inter[key]] * 2, axis=1) * qc[key]
        den = jnp.maximum(jnp.abs(both[:, MLSTM_DH:]), floor[key])
        out_ref[bb, :, hsl(hh)] = both[:, :MLSTM_DH] / den


def _mlstm(q_lat, k_ctx, v_ctx, g_ctx, k_lat, v_lat, g_lat):
    B, T, _ = q_lat.shape
    L = MLSTM_CHUNK
    nl = T // L
    nc = k_ctx.shape[1] // L
    steps = nc + nl

    fwd_c = lambda s: (0, jnp.minimum(s, nc - 1), 0)
    bwd_c = lambda s: (0, jnp.maximum(nc - 1 - s, 0), 0)
    fwd_l = lambda s: (0, jnp.maximum(s - nc, 0), 0)
    bwd_l = lambda s: (0, jnp.minimum(steps - 1 - s, nl - 1), 0)

    blk = lambda w, im: pl.BlockSpec((B, L, w), im)
    pair = lambda w, f, g: [blk(w, f), blk(w, g)]
    nheads = B * 2 * MLSTM_HEADS
    return pl.pallas_call(
        functools.partial(_mlstm_kernel, n_ctx_chunks=nc),
        grid=(steps,),
        in_specs=(pair(MLSTM_W, fwd_l, bwd_l)
                  + pair(MLSTM_W, fwd_c, bwd_c) + pair(MLSTM_W, fwd_l, bwd_l)
                  + pair(MLSTM_W, fwd_c, bwd_c) + pair(MLSTM_W, fwd_l, bwd_l)
                  + pair(GATE_LANES, fwd_c, bwd_c) + pair(GATE_LANES, fwd_l, bwd_l)),
        out_specs=pair(MLSTM_W, fwd_l, bwd_l),
        out_shape=[jax.ShapeDtypeStruct((B, T, MLSTM_W), F32)] * 2,
        scratch_shapes=[pltpu.VMEM((nheads, MLSTM_DH, 2 * MLSTM_DH), F32),
                        pltpu.VMEM((nheads, 1, MLSTM_DH), F32)],
        compiler_params=_cparams(("arbitrary",)),
        name="mlstm",
    )(q_lat, q_lat, k_ctx, k_ctx, k_lat, k_lat, v_ctx, v_ctx, v_lat, v_lat,
      g_ctx, g_ctx, g_lat, g_lat)


def _attn_kernel(q_ref, kp_ref, kc_ref, kn_ref, vp_ref, vc_ref, vn_ref, kx_ref, vx_ref,
                 sink_ref, o_ref, *, nb):
    n = pl.program_id(0)
    blk = ATTN_BLOCK
    n_batch = q_ref.shape[0]

    qi = lax.broadcasted_iota(jnp.int32, (blk, blk), 0)
    kj = lax.broadcasted_iota(jnp.int32, (blk, blk), 1)
    ok_prev = (kj >= qi) & (n > 0)
    ok_next = (kj <= qi) & (n < nb - 1)
    lane = lax.broadcasted_iota(jnp.int32, (blk, 2 * ATTN_DH), 1)
    low = lane < ATTN_DH
    n_pairs = ATTN_Q_W // (2 * ATTN_DH)
    n_heads = 2 * n_pairs

    def fold(op, cols):
        groups = [c[:, i:i + blk] for c in cols for i in range(0, c.shape[1], blk)]
        return functools.reduce(op, groups)

    s_all, vcat = [], []
    for bb in range(n_batch):
        kcat = jnp.concatenate([kp_ref[bb], kc_ref[bb], kn_ref[bb], kx_ref[bb]], axis=0)
        vrows = jnp.concatenate([vp_ref[bb], vc_ref[bb], vn_ref[bb], vx_ref[bb]], axis=0)
        vcat.append(jnp.concatenate([vrows, jnp.ones_like(vrows)], axis=1))
        pieces = []
        for c in range(n_pairs):
            qc = q_ref[bb, :, c * 128:(c + 1) * 128]
            pieces += [jnp.where(low, qc, jnp.zeros_like(qc)), jnp.where(low, jnp.zeros_like(qc), qc)]
        s_all.append(_dot_nt(jnp.concatenate(pieces, axis=0), kcat))

    parts, maxes, probs, dens = {}, {}, {}, {}
    keys = [(bb, j) for bb in range(n_batch) for j in range(n_heads)]
    for bb, j in keys:
        sc = s_all[bb][j * blk:(j + 1) * blk]
        parts[bb, j] = [jnp.where(ok_prev, sc[:, 0:blk], NEG_INF), sc[:, blk:2 * blk],
                        jnp.where(ok_next, sc[:, 2 * blk:3 * blk], NEG_INF), sc[:, 3 * blk:]]
        row_max = jnp.max(fold(jnp.maximum, parts[bb, j]), axis=1, keepdims=True)
        maxes[bb, j] = jnp.maximum(row_max, sink_ref[j:j + 1, 0:1])
    for bb, j in keys:
        exps = [jnp.exp(part - maxes[bb, j]) for part in parts[bb, j]]
        dens[bb, j] = jnp.exp(sink_ref[j:j + 1, 0:1] - maxes[bb, j])
        probs[bb, j] = jnp.concatenate([p.astype(BF16) for p in exps], axis=1)
    for bb in range(n_batch):
        o_all = _dot(jnp.concatenate([probs[bb, j] for j in range(n_heads)], axis=0), vcat[bb])
        heads_out = []
        for j in range(n_heads):
            rows = slice(j * blk, (j + 1) * blk)
            heads_out.append(o_all[rows, 0:128] / (o_all[rows, 128:256] + dens[bb, j]))
        for c in range(n_pairs):
            pair = jnp.where(low, heads_out[2 * c], heads_out[2 * c + 1])
            o_ref[bb, :, c * 128:(c + 1) * 128] = pair.astype(BF16)


def _attn(aq, ak, av, akx, avx, sink8):
    B, T, _ = aq.shape
    blk = ATTN_BLOCK
    nb = T // blk
    ctx = akx.shape[1]
    kv = lambda im: pl.BlockSpec((B, blk, ATTN_KV_W), im)
    prev = lambda n: (0, jnp.maximum(n - 1, 0), 0)
    cur = lambda n: (0, n, 0)
    nxt = lambda n: (0, jnp.minimum(n + 1, nb - 1), 0)
    cx = pl.BlockSpec((B, ctx, ATTN_KV_W), lambda n: (0, 0, 0))
    return pl.pallas_call(
        functools.partial(_attn_kernel, nb=nb),
        grid=(nb,),
        in_specs=[pl.BlockSpec((B, blk, ATTN_Q_W), cur),
                  kv(prev), kv(cur), kv(nxt), kv(prev), kv(cur), kv(nxt), cx, cx,
                  pl.BlockSpec((8, 128), lambda n: (0, 0))],
        out_specs=pl.BlockSpec((B, blk, ATTN_Q_W), cur),
        out_shape=jax.ShapeDtypeStruct((B, T, ATTN_Q_W), BF16),
        compiler_params=_cparams(("parallel",)),
        name="attn",
    )(aq, ak, ak, ak, av, av, av, akx, avx, sink8)


def _merge_kernel(x_ref, sh_ref, sc_ref, g_ref, g1_ref, hf_ref, hb_ref, a_ref,
                  wg_ref, ng_ref, wbm_ref, wba_ref, wo_ref, o_ref):
    x = x_ref[0]
    h = _modulated(x, g_ref[...], sh_ref[0], sc_ref[0]).astype(BF16)
    pg = _dot(h, wg_ref[...])
    d = x.shape[1]
    hs = hf_ref[0] + hb_ref[0]
    heads = [hs[:, i:i + MLSTM_DH] for i in range(0, MLSTM_W, MLSTM_DH)]
    hn = jnp.concatenate(
        [hh * lax.rsqrt(jnp.mean(hh * hh, axis=1, keepdims=True) + EPS) for hh in heads], axis=1)
    m = hn * ng_ref[...] * _sigmoid(pg[:, 0:MLSTM_W])
    mm = _dot(m.astype(BF16), wbm_ref[...])
    aa = _dot(a_ref[0], wba_ref[...])
    y = _sigmoid(pg[:, MLSTM_W:MLSTM_W + d]) * mm + _sigmoid(pg[:, MLSTM_W + d:]) * aa
    o_ref[0] = x + g1_ref[0] * _dot(y.astype(BF16), wo_ref[...])


def _merge(x, shift, scale, g, g1, hf, hb, a, wg, ng, wbm, wba, wo, *, tm):
    B, T, D = x.shape
    full = lambda arr: pl.BlockSpec(arr.shape, lambda b, t: (0,) * arr.ndim)
    tok = lambda w: pl.BlockSpec((1, tm, w), lambda b, t: (b, t, 0))
    perb = pl.BlockSpec((1, 1, D), lambda b, t: (b, 0, 0))
    return pl.pallas_call(
        _merge_kernel,
        grid=(B, T // tm),
        in_specs=[tok(D), perb, perb, full(g), perb, tok(MLSTM_W), tok(MLSTM_W), tok(ATTN_Q_W),
                  full(wg), full(ng), full(wbm), full(wba), full(wo)],
        out_specs=tok(D),
        out_shape=jax.ShapeDtypeStruct((B, T, D), F32),
        compiler_params=_cparams(("parallel", "parallel")),
        name="merge",
    )(x, shift, scale, g, g1, hf, hb, a, wg, ng, wbm, wba, wo)


def _pquery_kernel(x_ref, sh_ref, sc_ref, g_ref, wq_ref, sk_ref, h_ref, s_ref):
    x = x_ref[0]
    h = _modulated(x, g_ref[...], sh_ref[0], sc_ref[0]).astype(BF16)
    h_ref[0] = h
    q = _dot(h, wq_ref[...]).astype(BF16)
    for j in range(2 * PEER_HEADS):
        s_ref[0, j] = _dot_nt(sk_ref[j], q[:, j * PEER_HALF:(j + 1) * PEER_HALF])


def _pquery(x, shift, scale, g, wq, sk, *, tm):
    B, T, D = x.shape
    nj = 2 * PEER_HEADS
    full = lambda arr: pl.BlockSpec(arr.shape, lambda b, t: (0,) * arr.ndim)
    perb = pl.BlockSpec((1, 1, D), lambda b, t: (b, 0, 0))
    return pl.pallas_call(
        _pquery_kernel,
        grid=(B, T // tm),
        in_specs=[pl.BlockSpec((1, tm, D), lambda b, t: (b, t, 0)), perb, perb,
                  full(g), full(wq), full(sk)],
        out_specs=[pl.BlockSpec((1, tm, D), lambda b, t: (b, t, 0)),
                   pl.BlockSpec((1, nj, PEER_N_KEYS, tm), lambda b, t: (b, 0, 0, t))],
        out_shape=[jax.ShapeDtypeStruct((B, T, D), BF16),
                   jax.ShapeDtypeStruct((B, nj, PEER_N_KEYS, T), F32)],
        compiler_params=_cparams(("parallel", "parallel")),
        name="pquery",
    )(x, shift, scale, g, wq, sk)


def _top_values(val, k):
    tops = []
    for _ in range(k):
        m = jnp.max(val, axis=0, keepdims=True)
        tops.append(m)
        val = jnp.where(val == m, NEG_INF, val)
    return tops


_GELU_K0 = -2.0 * math.sqrt(2.0 / math.pi) * math.log2(math.e)
_GELU_K1 = 0.044715 * _GELU_K0


def _gelu_tanh(x, k0, k1):
    one = jnp.asarray(1.0, x.dtype)
    return x / (one + jnp.exp2(x * (x * x * k1 + k0)))


BF16_ROWS = 16


def _rows_bf16(row, tm):
    tile = jnp.broadcast_to(row, (BF16_ROWS, tm)).astype(BF16)
    return jnp.concatenate([tile] * (PEER_N_KEYS // BF16_ROWS), axis=0)


def _row_tile_bf16(ref, h, i, tm):
    return _rows_bf16(ref[h, pl.ds(i, 1), :], tm)


def _batcher_pairs(n):
    pairs = []
    p = 1
    while p < n:
        k = p
        while k >= 1:
            for j in range(k % p, n - k, 2 * k):
                for i in range(min(k, n - j - k)):
                    if (i + j) // (2 * p) == (i + j + k) // (2 * p):
                        pairs.append((i + j, i + j + k))
            k //= 2
        p *= 2
    return pairs


def _compare_exchange(rows, i, j):
    rows[i], rows[j] = jnp.maximum(rows[i], rows[j]), jnp.minimum(rows[i], rows[j])


def _sorted_top(x, k):
    sub = 8
    assert x.shape[0] == sub * k and k & (k - 1) == 0
    rows = [x[sub * r:sub * (r + 1), :] for r in range(k)]
    for i, j in _batcher_pairs(k):
        _compare_exchange(rows, i, j)
    shift = sub // 2
    while shift >= 1:
        other = [pltpu.roll(r, shift, 0) for r in rows]
        rows = [jnp.maximum(rows[r], other[k - 1 - r]) for r in range(k)]
        d = k // 2
        while d >= 1:
            for i in range(k):
                if i & d == 0:
                    _compare_exchange(rows, i, i + d)
            d //= 2
        shift //= 2
    return rows


def _prefix_count(pred, vals):
    k = len(vals)
    bits, steps = [], []
    s = k // 2
    while s >= 1:
        def pivot(i, base):
            if i == len(bits):
                return vals[base + s - 1]
            return jnp.where(bits[i], pivot(i + 1, base + steps[i]), pivot(i + 1, base))
        bits.append(pred(pivot(0, 0)))
        steps.append(s)
        s //= 2
    n = functools.reduce(jnp.add, [jnp.where(b, float(st), 0.0) for b, st in zip(bits, steps)])
    return jnp.where(pred(vals[k - 1]), float(k), n)


def _peer_router(s_ref, rank_ref, cw_ref, cnt_ref, e2_ref):
    k = PEER_TOPK
    for h in range(PEER_HEADS):
        s1 = s_ref[0, 2 * h]
        s2 = s_ref[0, 2 * h + 1]
        a = [r[0:1, :] for r in _sorted_top(s1, k)]
        b = [r[0:1, :] for r in _sorted_top(s2, k)]
        cand = [a[p] + b[q] for p in range(k) for q in range(k) if (p + 1) * (q + 1) <= k]
        cv = jnp.concatenate(cand, axis=0)
        thr = _top_values(cv, k)[k - 1]
        top = a[0] + b[0]
        z = jnp.sum(jnp.where(cv >= thr, jnp.exp(cv - top), 0.0), axis=0, keepdims=True)
        rank = _prefix_count(lambda v: v > s1, a)
        cnt = _prefix_count(lambda v: v + s2 >= thr, a)
        rank_ref[h] = rank
        cw_ref[h] = jnp.exp(s1 - a[0]) / z
        cnt_ref[h] = cnt.astype(BF16)
        e2_ref[h] = jnp.exp(s2 - b[0]).astype(BF16)


def _peer_kernel(h_ref, s_ref, u_ref, vt_ref, x_ref, g2_ref, gk_ref, o_ref,
                 acc_ref, rank_ref, cw_ref, cnt_ref, e2_ref, wt_ref, *, eb, sb):
    e = pl.program_id(2)
    ne = pl.num_programs(2)
    tm = h_ref.shape[1]

    @pl.when(e == 0)
    def _():
        acc_ref[...] = jnp.zeros_like(acc_ref)
        _peer_router(s_ref, rank_ref, cw_ref, cnt_ref, e2_ref)

    hq = h_ref[0]
    n_i = sb // PEER_N_KEYS
    n_sub = eb // sb
    scores = lambda j: _dot_nt(u_ref[j * sb:(j + 1) * sb, :], hq)
    zero = jnp.zeros((), BF16)
    gk0 = _rows_bf16(gk_ref[0:1, :], tm)
    gk1 = _rows_bf16(gk_ref[1:2, :], tm)
    def output_product(j):
        acc_ref[...] += _dot(vt_ref[:, j * sb:(j + 1) * sb], wt_ref[j % 2])

    acts = {j: scores(j) for j in range(min(2, n_sub))}
    for j in range(n_sub):
        act = acts.pop(j)
        for ii in range(n_i):
            if ii == n_i // 2 and j > 0:
                output_product(j - 1)
            i = e * (eb // PEER_N_KEYS) + j * n_i + ii
            w = None
            for h in range(PEER_HEADS):
                rank_i = _row_tile_bf16(rank_ref, h, i, tm)
                cw_i = _row_tile_bf16(cw_ref, h, i, tm)
                term = jnp.where(rank_i < cnt_ref[h], e2_ref[h], zero) * cw_i
                w = term if w is None else w + term
            act_i = act[ii * PEER_N_KEYS:(ii + 1) * PEER_N_KEYS].astype(BF16)
            wt_ref[j % 2, ii * PEER_N_KEYS:(ii + 1) * PEER_N_KEYS, :] = w * _gelu_tanh(act_i, gk0, gk1)
        if j + 2 < n_sub:
            acts[j + 2] = scores(j + 2)
    output_product(n_sub - 1)

    @pl.when(e == ne - 1)
    def _():
        o_ref[0] = x_ref[0] + g2_ref[0] * acc_ref[...].T


def _peer(h2, s_t, u, vt, x1, g2, *, tm, eb, sb):
    B, T, D = x1.shape
    ne = u.shape[0] // eb
    nj = 2 * PEER_HEADS
    gelu_k = jnp.broadcast_to(jnp.array([[_GELU_K0], [_GELU_K1]], F32), (2, tm))
    return pl.pallas_call(
        functools.partial(_peer_kernel, eb=eb, sb=sb),
        grid=(B, T // tm, ne),
        in_specs=[pl.BlockSpec((1, tm, D), lambda b, t, e: (b, t, 0)),
                  pl.BlockSpec((1, nj, PEER_N_KEYS, tm), lambda b, t, e: (b, 0, 0, t)),
                  pl.BlockSpec((eb, D), lambda b, t, e: (e, 0)),
                  pl.BlockSpec((D, eb), lambda b, t, e: (0, e)),
                  pl.BlockSpec((1, tm, D), lambda b, t, e: (b, t, 0)),
                  pl.BlockSpec((1, 1, D), lambda b, t, e: (b, 0, 0)),
                  pl.BlockSpec((2, tm), lambda b, t, e: (0, 0))],
        out_specs=pl.BlockSpec((1, tm, D), lambda b, t, e: (b, t, 0)),
        out_shape=jax.ShapeDtypeStruct((B, T, D), F32),
        scratch_shapes=[pltpu.VMEM((D, tm), F32),
                        pltpu.VMEM((PEER_HEADS, PEER_N_KEYS, tm), F32),
                        pltpu.VMEM((PEER_HEADS, PEER_N_KEYS, tm), F32),
                        pltpu.VMEM((PEER_HEADS, PEER_N_KEYS, tm), BF16),
                        pltpu.VMEM((PEER_HEADS, PEER_N_KEYS, tm), BF16),
                        pltpu.VMEM((2, sb, tm), BF16)],
        compiler_params=_cparams(("parallel", "parallel", "arbitrary")),
        name="peer",
    )(h2, s_t, u, vt, x1, g2, gelu_k)


_Q_HEAD_ORDER = (0, 4, 1, 5, 2, 6, 3, 7)


def _block_diag_ones(width, seg):
    i = jnp.arange(width)
    return (i[:, None] // seg == i[None, :] // seg).astype(BF16)


def _rope_tables(T):
    rows = T // GRID_W
    nf = ATTN_DH // 4
    row = jnp.broadcast_to(jnp.arange(rows)[:, None], (rows, GRID_W)).reshape(T)
    col = jnp.broadcast_to(jnp.arange(GRID_W)[None, :], (rows, GRID_W)).reshape(T)
    inv = ROPE_BASE ** (-jnp.arange(nf, dtype=F32) / nf)
    ang = jnp.concatenate([row[:, None].astype(F32) * inv, col[:, None].astype(F32) * inv], axis=-1)
    cos, sin = jnp.cos(ang), jnp.sin(ang)
    cos_t = jnp.concatenate([cos, cos] * ATTN_KV_HEADS, axis=-1)
    sin_t = jnp.concatenate([-sin, sin] * ATTN_KV_HEADS, axis=-1)
    return cos_t, sin_t


def _layer(x, ctx, mod, norm_mix_g, norm_ffn_g, w_in, b_mgates, mlstm_norm_g, attn_q_norm_g,
           attn_k_norm_g, attn_sink, w_branch_m, w_branch_a, w_out, peer_w_query, peer_sub_keys,
           peer_u, peer_v, cos_t, sin_t):
    B, T, D = x.shape
    n_ctx = ctx.shape[1]
    mods = jnp.split(mod, 6, axis=-1)
    lat = [m[0:B].reshape(B, 1, D) for m in mods]
    cxm = [jnp.broadcast_to(m[B:B + 1].reshape(1, 1, D), (B, 1, D)) for m in mods]
    sh1, sc1, g1, sh2, sc2, g2 = lat

    W = MLSTM_W
    o0 = 4 * W
    o1 = o0 + 2 * 2 * MLSTM_HEADS
    w_mq, w_mk, w_mv, w_mo = (w_in[:, i * W:(i + 1) * W] for i in range(4))
    w_g16 = w_in[:, o0:o1]
    w_aq = w_in[:, o1:o1 + ATTN_Q_W].reshape(D, ATTN_Q_HEADS, ATTN_DH)
    w_aq = w_aq[:, jnp.array(_Q_HEAD_ORDER), :].reshape(D, ATTN_Q_W)
    o2 = o1 + ATTN_Q_W
    w_ak = w_in[:, o2:o2 + ATTN_KV_W]
    w_av = w_in[:, o2 + ATTN_KV_W:o2 + 2 * ATTN_KV_W]
    w_gates = w_in[:, o2 + 2 * ATTN_KV_W:]
    w_g128 = jnp.pad(w_g16, ((0, 0), (0, GATE_LANES - w_g16.shape[1])))
    w_pack = jnp.concatenate([w_mq, w_mk, w_mv, w_aq, w_ak, w_av, w_g128], axis=1).astype(BF16)
    w_merge_in = jnp.concatenate([w_mo, w_gates], axis=1).astype(BF16)
    bg = jnp.pad(b_mgates, (0, GATE_LANES - b_mgates.shape[0])).reshape(1, GATE_LANES)
    qg = jnp.tile(attn_q_norm_g, ATTN_Q_HEADS).reshape(1, ATTN_Q_W)
    kg = jnp.tile(attn_k_norm_g, ATTN_KV_HEADS).reshape(1, ATTN_KV_W)
    bdq = _block_diag_ones(ATTN_Q_W, ATTN_DH)
    bdk = _block_diag_ones(ATTN_KV_W, ATTN_DH)
    gmix = norm_mix_g.reshape(1, D)

    mq, mk, mv, mg, aq, ak, av = _proj(x, sh1, sc1, gmix, w_pack, bg, qg, kg, bdq, bdk,
                                       cos_t, sin_t, rope=True, tm=min(TOKENS_PROJ, T))
    _, mkc, mvc, mgc, _, akc, avc = _proj(ctx, cxm[0], cxm[1], gmix, w_pack, bg, qg, kg, bdq, bdk,
                                          cos_t, sin_t, rope=False, tm=n_ctx)

    hf, hb = _mlstm(mq, mkc, mvc, mgc, mk, mv, mg)

    sink8 = jnp.broadcast_to(attn_sink[jnp.array(_Q_HEAD_ORDER)].astype(F32)[:, None], (8, 128))
    a = _attn(aq, ak, av, akc, avc, sink8)

    wba = w_branch_a.reshape(ATTN_Q_HEADS, ATTN_DH, D)[jnp.array(_Q_HEAD_ORDER)]
    wba = wba.reshape(ATTN_Q_W, D).astype(BF16)
    x1 = _merge(x, sh1, sc1, gmix, g1, hf, hb, a, w_merge_in, mlstm_norm_g.reshape(1, MLSTM_W),
                w_branch_m.astype(BF16), wba, w_out.astype(BF16), tm=min(TOKENS_MERGE, T))

    sk = peer_sub_keys.reshape(2 * PEER_HEADS, PEER_N_KEYS, PEER_HALF).astype(BF16)
    h2, s_t = _pquery(x1, sh2, sc2, norm_ffn_g.reshape(1, D), peer_w_query.astype(BF16), sk,
                      tm=min(TOKENS_PQUERY, T))
    u = peer_u.astype(BF16)
    vt = peer_v.T.astype(BF16)
    return _peer(h2, s_t, u, vt, x1, g2, tm=min(TOKENS_PEER, T), eb=PEER_EXPERT_BLOCK,
                 sb=PEER_SUB_BLOCK)


def kernel(x, c, ctx, c_ctx, w_ada, b_ada, norm_mix_g, norm_ffn_g, w_in, b_mgates, mlstm_norm_g,
           attn_q_norm_g, attn_k_norm_g, attn_sink, w_branch_m, w_branch_a, w_out, peer_w_query,
           peer_sub_keys, peer_u, peer_v):
    B, T, D = x.shape
    depth = w_ada.shape[0]
    assert depth == 1, "context-stream update for deeper stacks is not implemented"
    cos_t, sin_t = _rope_tables(T)
    cond8 = jnp.zeros((8, D), F32).at[0:B].set(c).at[B].set(c_ctx)
    for layer in range(depth):
        mod = _adaln(cond8, w_ada[layer], b_ada[layer])
        x = _layer(x, ctx, mod, norm_mix_g[layer], norm_ffn_g[layer], w_in[layer], b_mgates[layer],
                   mlstm_norm_g[layer], attn_q_norm_g[layer], attn_k_norm_g[layer], attn_sink[layer],
                   w_branch_m[layer], w_branch_a[layer], w_out[layer], peer_w_query[layer],
                   peer_sub_keys[layer], peer_u[layer], peer_v[layer], cos_t, sin_t)
    return x
```
